```python
import math
import jax, jax.numpy as jnp
from jax import lax
import numpy as np

D_MODEL = 2048
BATCH = 16
SEQ = 2048
DEPTH = 2

RET_HEADS = 8
RET_HEAD_DIM = 128
RET_CHUNK = 128
MLSTM_HEADS = 4
MLSTM_QK_DIM = 128
MLSTM_V_DIM = 256
MLSTM_CHUNK = 128
MLSTM_CONV = 4
MOBA_HEADS = 8
MOBA_HEAD_DIM = 128
MOBA_BLOCK = 256
MOBA_TOPK = 3
MOBA_QCHUNK = 64
REL_BUCKETS = 32
REL_MAX_DIST = 128
D_FF = 4 * D_MODEL
ROPE_BASE = 10000.0
EPS = 1e-6

RET_W = RET_HEADS * RET_HEAD_DIM
MLSTM_QK_W = MLSTM_HEADS * MLSTM_QK_DIM
MLSTM_V_W = MLSTM_HEADS * MLSTM_V_DIM
MOBA_W = MOBA_HEADS * MOBA_HEAD_DIM
N_BRANCH = 3
IN_SPLITS = (RET_W, RET_W, RET_W, RET_W,
             MLSTM_QK_W, MLSTM_QK_W, MLSTM_V_W, MLSTM_V_W, MLSTM_HEADS, MLSTM_HEADS,
             MOBA_W, MOBA_W, MOBA_W,
             N_BRANCH * D_MODEL)
D_IN = sum(IN_SPLITS)

kernel_name = "hybrid_retention_mlstm_moba_gated_block"


def rms_norm(x, w):
    xf = x.astype(jnp.float32)
    y = xf * lax.rsqrt(jnp.mean(xf * xf, axis=-1, keepdims=True) + EPS)
    return (y * w.astype(jnp.float32)).astype(x.dtype)


def head_rms(x):
    return x * lax.rsqrt(jnp.mean(x * x, axis=-1, keepdims=True) + EPS)


def rotary(x, pos):
    half = x.shape[-1] // 2
    inv = ROPE_BASE ** (-jnp.arange(half, dtype=jnp.float32) / half)
    ang = pos.astype(jnp.float32)[:, None] * inv[None, :]
    cos, sin = jnp.cos(ang), jnp.sin(ang)
    x1, x2 = x[..., :half], x[..., half:]
    return jnp.concatenate([x1 * cos - x2 * sin, x1 * sin + x2 * cos], axis=-1)


def causal_dwconv(x, w, b):
    K = w.shape[0]
    S = x.shape[1]
    xp = jnp.pad(x, ((0, 0), (K - 1, 0), (0, 0)))
    y = xp[:, 0:S] * w[0]
    for j in range(1, K):
        y = y + xp[:, j:j + S] * w[j]
    return y + b


def retention(q, k, v, g):
    B, S, H, dh = q.shape
    L = RET_CHUNK
    N = S // L
    pos = jnp.arange(S)
    q = rotary(jnp.swapaxes(q, 1, 2).astype(jnp.float32), pos) * (dh ** -0.5)
    k = rotary(jnp.swapaxes(k, 1, 2).astype(jnp.float32), pos)
    v = jnp.swapaxes(v, 1, 2).astype(jnp.float32)
    q = q.reshape(B, H, N, L, dh)
    k = k.reshape(B, H, N, L, dh)
    v = v.reshape(B, H, N, L, dh)
    log_gamma = jnp.log1p(-jnp.exp2(-5.0 - jnp.arange(H, dtype=jnp.float32)))
    idx = jnp.arange(L, dtype=jnp.float32)
    diff = idx[:, None] - idx[None, :]
    decay = jnp.where(diff >= 0, jnp.exp(jnp.maximum(diff, 0.0) * log_gamma[:, None, None]), 0.0)
    scores = jnp.einsum('bhnld,bhnmd->bhnlm', q, k) * decay[:, None]
    intra = jnp.einsum('bhnlm,bhnme->bhnle', scores, v)
    zeta = jnp.exp((L - 1 - idx)[None, :] * log_gamma[:, None])
    xi = jnp.exp((idx + 1.0)[None, :] * log_gamma[:, None])
    kv = jnp.einsum('bhnld,bhnle->bhnde', k * zeta[:, None, :, None], v)
    gamma_l = jnp.exp(L * log_gamma)

    def step(R, kv_n):
        return R * gamma_l[:, None, None] + kv_n, R

    _, R_prev = lax.scan(step, jnp.zeros((B, H, dh, dh), jnp.float32), jnp.moveaxis(kv, 2, 0))
    R_prev = jnp.moveaxis(R_prev, 0, 2)
    inter = jnp.einsum('bhnld,bhnde->bhnle', q, R_prev) * xi[:, None, :, None]
    o = head_rms((intra + inter).reshape(B, H, S, dh))
    o = jnp.swapaxes(o, 1, 2).reshape(B, S, H * dh)
    return o * jax.nn.silu(g.astype(jnp.float32))


def mlstm(q, k, v, o, i_pre, f_pre, norm_w):
    B, S, H, dk = q.shape
    dv = v.shape[-1]
    L = MLSTM_CHUNK
    N = S // L

    def chunk(t):
        return jnp.swapaxes(t, 1, 2).astype(jnp.float32).reshape((B, H, N, L) + t.shape[3:])

    q = chunk(q) * (dk ** -0.5)
    k = chunk(k)
    v = chunk(v)
    li = chunk(i_pre)
    lf = jax.nn.log_sigmoid(chunk(f_pre))
    bcum = jnp.cumsum(lf, axis=-1)
    gtot = bcum[..., -1]
    causal = jnp.tril(jnp.ones((L, L), dtype=bool))
    dlog = jnp.where(causal, bcum[..., :, None] - bcum[..., None, :] + li[..., None, :], -jnp.inf)
    w_loc = gtot[..., None] - bcum + li
    m_loc = jnp.max(w_loc, axis=-1)
    e_loc = jnp.exp(w_loc - m_loc[..., None])
    C_loc = jnp.einsum('bhnl,bhnld,bhnle->bhnde', e_loc, k, v)
    n_loc = jnp.einsum('bhnl,bhnld->bhnd', e_loc, k)

    def step(carry, xs):
        C, n, m = carry
        Cl, nl, ml, g = xs
        m_new = jnp.maximum(g + m, ml)
        a = jnp.exp(g + m - m_new)
        bb = jnp.exp(ml - m_new)
        C_new = a[..., None, None] * C + bb[..., None, None] * Cl
        n_new = a[..., None] * n + bb[..., None] * nl
        return (C_new, n_new, m_new), (C, n, m)

    init = (jnp.zeros((B, H, dk, dv), jnp.float32), jnp.zeros((B, H, dk), jnp.float32),
            jnp.zeros((B, H), jnp.float32))
    xs = (jnp.moveaxis(C_loc, 2, 0), jnp.moveaxis(n_loc, 2, 0), jnp.moveaxis(m_loc, 2, 0),
          jnp.moveaxis(gtot, 2, 0))
    _, (C_prev, n_prev, m_prev) = lax.scan(step, init, xs)
    C_prev = jnp.moveaxis(C_prev, 0, 2)
    n_prev = jnp.moveaxis(n_prev, 0, 2)
    m_prev = jnp.moveaxis(m_prev, 0, 2)
    inter_log = bcum + m_prev[..., None]
    m_pos = jnp.maximum(inter_log, jnp.max(dlog, axis=-1))
    s_intra = jnp.einsum('bhnld,bhnmd->bhnlm', q, k) * jnp.exp(dlog - m_pos[..., None])
    inter_scale = jnp.exp(inter_log - m_pos)
    num = (jnp.einsum('bhnlm,bhnme->bhnle', s_intra, v)
           + inter_scale[..., None] * jnp.einsum('bhnld,bhnde->bhnle', q, C_prev))
    den = jnp.sum(s_intra, axis=-1) + inter_scale * jnp.einsum('bhnld,bhnd->bhnl', q, n_prev)
    h = num / jnp.maximum(jnp.abs(den), jnp.exp(-m_pos))[..., None]
    h = head_rms(h.reshape(B, H, S, dv))
    h = jnp.swapaxes(h, 1, 2).reshape(B, S, H * dv) * norm_w.astype(jnp.float32)
    return h * jax.nn.sigmoid(o.astype(jnp.float32))


def t5_bucket(dist):
    n = jnp.maximum(dist, 0)
    exact = REL_BUCKETS // 2
    nf = jnp.maximum(n, 1).astype(jnp.float32)
    large = exact + (jnp.log(nf / exact) / math.log(REL_MAX_DIST / exact)
                     * (REL_BUCKETS - exact)).astype(jnp.int32)
    large = jnp.minimum(large, REL_BUCKETS - 1)
    return jnp.where(n < exact, n, large)


def moba(q, k, v, rel_bias):
    B, S, H, dh = q.shape
    BS = MOBA_BLOCK
    QC = MOBA_QCHUNK
    NB = -(-S // BS)
    S_pad = NB * BS
    n_sel = min(MOBA_TOPK, NB - 1)
    nC = S // QC
    q = jnp.swapaxes(q, 1, 2).astype(jnp.float32) * (dh ** -0.5)
    pad = ((0, 0), (0, 0), (0, S_pad - S), (0, 0))
    kb = jnp.pad(jnp.swapaxes(k, 1, 2).astype(jnp.float32), pad).reshape(B, H, NB, BS, dh)
    vb = jnp.pad(jnp.swapaxes(v, 1, 2).astype(jnp.float32), pad).reshape(B, H, NB, BS, dh)
    table_t = rel_bias.T.astype(jnp.float32)
    hid = jnp.arange(H)[None, :, None, None]
    if n_sel > 0:
        kmean = jnp.mean(kb, axis=3)
        gate = jnp.einsum('bhsd,bhnd->bhsn', q, kmean)
        past = jnp.arange(NB)[None, :] < (jnp.arange(S) // BS)[:, None]
        gate = jnp.where(past, gate, -jnp.inf)
        _, sel = lax.top_k(gate, n_sel)
        sel = sel.astype(jnp.int32)
    else:
        sel = jnp.zeros((B, H, S, 0), jnp.int32)
    qc = jnp.moveaxis(q.reshape(B, H, nC, QC, dh), 2, 0)
    selc = jnp.moveaxis(sel.reshape(B, H, nC, QC, n_sel), 2, 0)
    gather = jax.vmap(jax.vmap(lambda blocks, i: blocks[i]))
    ar = jnp.arange(BS)

    def one_chunk(args):
        cid, q_c, sel_c = args
        qpos = cid * QC + jnp.arange(QC)
        own = (cid * QC) // BS
        logits = []
        for s in range(n_sel):
            blk = sel_c[..., s]
            k_s = gather(kb, blk)
            kpos = blk[..., None] * BS + ar
            bias = table_t[hid, t5_bucket(qpos[:, None] - kpos)]
            lg = jnp.einsum('bhqd,bhqjd->bhqj', q_c, k_s) + bias
            logits.append(jnp.where((blk < own)[..., None], lg, -jnp.inf))
        k_o = lax.dynamic_index_in_dim(kb, own, axis=2, keepdims=False)
        v_o = lax.dynamic_index_in_dim(vb, own, axis=2, keepdims=False)
        kpos_o = own * BS + ar
        bias_o = table_t[:, t5_bucket(qpos[:, None] - kpos_o[None, :])]
        lg_o = jnp.einsum('bhqd,bhjd->bhqj', q_c, k_o) + bias_o
        logits.append(jnp.where(kpos_o[None, :] <= qpos[:, None], lg_o, -jnp.inf))
        p = jax.nn.softmax(jnp.concatenate(logits, axis=-1), axis=-1)
        out = jnp.einsum('bhqj,bhje->bhqe', p[..., n_sel * BS:], v_o)
        for s in range(n_sel):
            v_s = gather(vb, sel_c[..., s])
            out = out + jnp.einsum('bhqj,bhqje->bhqe', p[..., s * BS:(s + 1) * BS], v_s)
        return out

    out = lax.map(one_chunk, (jnp.arange(nC, dtype=jnp.int32), qc, selc))
    out = jnp.moveaxis(out, 0, 2).reshape(B, H, S, dh)
    return jnp.swapaxes(out, 1, 2).reshape(B, S, H * dh)


def setup_inputs(seed: int = 0) -> dict:
    key = jax.random.key(seed)
    ks = jax.random.split(key, 20)

    def nrm(k, shape, scale):
        return jax.random.normal(k, shape, jnp.float32) * scale

    x = nrm(ks[0], (BATCH, SEQ, D_MODEL), 1.0)
    w_in = nrm(ks[1], (DEPTH, D_MODEL, D_IN), D_MODEL ** -0.5)
    i_b = nrm(ks[2], (DEPTH, 1, MLSTM_HEADS), 0.1)
    f_b = (jnp.linspace(3.0, 6.0, MLSTM_HEADS, dtype=jnp.float32)[None, None, :]
           + nrm(ks[3], (DEPTH, 1, MLSTM_HEADS), 0.1))
    mlstm_gate_b = jnp.concatenate([i_b, f_b], axis=1)
    mlstm_conv_w = nrm(ks[4], (DEPTH, MLSTM_CONV, 2 * MLSTM_QK_W), MLSTM_CONV ** -0.5)
    mlstm_conv_b = nrm(ks[5], (DEPTH, 2 * MLSTM_QK_W), 0.01)
    mlstm_norm_w = 1.0 + nrm(ks[6], (DEPTH, MLSTM_V_W), 0.02)
    w_branch_ret = nrm(ks[7], (DEPTH, RET_W, D_MODEL), RET_W ** -0.5)
    w_branch_mlstm = nrm(ks[8], (DEPTH, MLSTM_V_W, D_MODEL), MLSTM_V_W ** -0.5)
    w_branch_moba = nrm(ks[9], (DEPTH, MOBA_W, D_MODEL), MOBA_W ** -0.5)
    w_out = nrm(ks[10], (DEPTH, D_MODEL, D_MODEL), D_MODEL ** -0.5)
    norm_mix_w = 1.0 + nrm(ks[11], (DEPTH, D_MODEL), 0.02)
    norm_mlp_w = 1.0 + nrm(ks[12], (DEPTH, D_MODEL), 0.02)
    w_ff1 = nrm(ks[13], (DEPTH, D_MODEL, D_FF), D_MODEL ** -0.5)
    w_ff2 = nrm(ks[14], (DEPTH, D_FF, D_MODEL), D_FF ** -0.5)
    rel_bias = nrm(ks[15], (REL_BUCKETS, MOBA_HEADS), 0.5)
    final_norm_w = 1.0 + nrm(ks[16], (D_MODEL,), 0.02)
    return {"x": x, "w_in": w_in, "mlstm_gate_b": mlstm_gate_b, "mlstm_conv_w": mlstm_conv_w,
            "mlstm_conv_b": mlstm_conv_b, "mlstm_norm_w": mlstm_norm_w,
            "w_branch_ret": w_branch_ret, "w_branch_mlstm": w_branch_mlstm,
            "w_branch_moba": w_branch_moba, "w_out": w_out, "norm_mix_w": norm_mix_w,
            "norm_mlp_w": norm_mlp_w, "w_ff1": w_ff1, "w_ff2": w_ff2, "rel_bias": rel_bias,
            "final_norm_w": final_norm_w}


def reference(x, w_in, mlstm_gate_b, mlstm_conv_w, mlstm_conv_b, mlstm_norm_w,
              w_branch_ret, w_branch_mlstm, w_branch_moba, w_out, norm_mix_w,
              norm_mlp_w, w_ff1, w_ff2, rel_bias, final_norm_w):
    B, S, _ = x.shape
    offsets = np.cumsum(IN_SPLITS)[:-1].tolist()
    for l in range(DEPTH):
        h = rms_norm(x, norm_mix_w[l])
        proj = h @ w_in[l]
        (rq, rk, rv, rg, mq, mk, mv, mo, mi, mf, bq, bk, bv, gates) = jnp.split(proj, offsets, axis=-1)
        y_ret = retention(rq.reshape(B, S, RET_HEADS, RET_HEAD_DIM),
                          rk.reshape(B, S, RET_HEADS, RET_HEAD_DIM),
                          rv.reshape(B, S, RET_HEADS, RET_HEAD_DIM), rg)
        mqk = jax.nn.silu(causal_dwconv(jnp.concatenate([mq, mk], axis=-1), mlstm_conv_w[l], mlstm_conv_b[l]))
        mq_c, mk_c = jnp.split(mqk, 2, axis=-1)
        y_ml = mlstm(mq_c.reshape(B, S, MLSTM_HEADS, MLSTM_QK_DIM),
                     mk_c.reshape(B, S, MLSTM_HEADS, MLSTM_QK_DIM),
                     mv.reshape(B, S, MLSTM_HEADS, MLSTM_V_DIM), mo,
                     mi + mlstm_gate_b[l, 0], mf + mlstm_gate_b[l, 1], mlstm_norm_w[l])
        y_mb = moba(bq.reshape(B, S, MOBA_HEADS, MOBA_HEAD_DIM),
                    bk.reshape(B, S, MOBA_HEADS, MOBA_HEAD_DIM),
                    bv.reshape(B, S, MOBA_HEADS, MOBA_HEAD_DIM), rel_bias)
        g_ret, g_ml, g_mb = jnp.split(gates, 3, axis=-1)
        mixed = (jax.nn.sigmoid(g_ret) * (y_ret.astype(x.dtype) @ w_branch_ret[l])
                 + jax.nn.sigmoid(g_ml) * (y_ml.astype(x.dtype) @ w_branch_mlstm[l])
                 + jax.nn.sigmoid(g_mb) * (y_mb.astype(x.dtype) @ w_branch_moba[l]))
        x = x + mixed @ w_out[l]
        h2 = rms_norm(x, norm_mlp_w[l])
        x = x + jnp.square(jax.nn.relu(h2 @ w_ff1[l])) @ w_ff2[l]
    return rms_norm(x, final_norm_w)
```

```python
import functools
import math

import numpy as np
import jax
import jax.numpy as jnp
from jax import lax
from jax.experimental import pallas as pl
from jax.experimental.pallas import tpu as pltpu

D_MODEL = 2048
RET_HEADS = 8
RET_HEAD_DIM = 128
RET_CHUNK = 128
MLSTM_HEADS = 4
MLSTM_QK_DIM = 128
MLSTM_V_DIM = 256
MLSTM_CHUNK = 128
MLSTM_CONV = 4
MOBA_HEADS = 8
MOBA_HEAD_DIM = 128
MOBA_BLOCK = 256
MOBA_TOPK = 3
REL_BUCKETS = 32
REL_MAX_DIST = 128
D_FF = 4 * D_MODEL
ROPE_BASE = 10000.0
EPS = 1e-6

RET_W = RET_HEADS * RET_HEAD_DIM
MLSTM_QK_W = MLSTM_HEADS * MLSTM_QK_DIM
MLSTM_V_W = MLSTM_HEADS * MLSTM_V_DIM
MOBA_W = MOBA_HEADS * MOBA_HEAD_DIM

LANES = 128
SUBLANES = 8
VMEM_LIMIT_BYTES = 56 * 1024 * 1024

_OFF_RQ = 0
_OFF_MI = 4 * RET_W + 2 * MLSTM_QK_W + 2 * MLSTM_V_W
_OFF_BQ = _OFF_MI + 2 * MLSTM_HEADS
_OFF_GATES = _OFF_BQ + 3 * MOBA_W
_N_HEADMAJOR = _OFF_MI + 3 * MOBA_W
_BLK_RQ, _BLK_RK, _BLK_RV, _BLK_RG = 0, 8, 16, 24
_BLK_MQ, _BLK_MK, _BLK_MV, _BLK_MO = 32, 36, 40, 48
_BLK_BQ, _BLK_BK, _BLK_BV = 56, 64, 72

NEG_BIG = -1e30

BF16 = jnp.bfloat16
F32 = jnp.float32


def _cparams(sem):
    return pltpu.CompilerParams(dimension_semantics=sem, vmem_limit_bytes=VMEM_LIMIT_BYTES)


def _rms_rows(xf, w_row):
    ms = jnp.mean(xf * xf, axis=-1, keepdims=True)
    return xf * lax.rsqrt(ms + EPS) * w_row


def _norm_matmul_kernel(x_ref, nw_ref, w_ref, o_ref, hn_ref, *, head_major, act, row_chunk):
    j = pl.program_id(1)
    tm = x_ref.shape[0]

    @pl.when(j == 0)
    def _():
        def body(c, carry):
            r = pl.multiple_of(c * row_chunk, row_chunk)
            xf = x_ref[pl.ds(r, row_chunk), :]
            hn_ref[pl.ds(r, row_chunk), :] = _rms_rows(xf, nw_ref[...]).astype(BF16)
            return carry
        lax.fori_loop(0, tm // row_chunk, body, 0)

    acc = jnp.dot(hn_ref[...], w_ref[...], preferred_element_type=F32)
    if act == "sigmoid":
        acc = jax.nn.sigmoid(acc)
    if head_major:
        for c in range(o_ref.shape[0]):
            o_ref[c] = acc[:, c * LANES:(c + 1) * LANES].astype(o_ref.dtype)
    else:
        o_ref[...] = acc.astype(o_ref.dtype)


def _norm_matmul(x2, nw, w, *, tm, tn, head_major, act, out_dtype, name):
    T, D = x2.shape
    N = w.shape[1]
    grid = (T // tm, N // tn)
    if head_major:
        out_shape = jax.ShapeDtypeStruct((N // LANES, T, LANES), out_dtype)
        out_spec = pl.BlockSpec((tn // LANES, tm, LANES), lambda i, j: (j, i, 0))
    else:
        out_shape = jax.ShapeDtypeStruct((T, N), out_dtype)
        out_spec = pl.BlockSpec((tm, tn), lambda i, j: (i, j))
    kern = functools.partial(_norm_matmul_kernel, head_major=head_major, act=act, row_chunk=128)
    return pl.pallas_call(
        kern,
        out_shape=out_shape,
        grid=grid,
        in_specs=[pl.BlockSpec((tm, D), lambda i, j: (i, 0)),
                  pl.BlockSpec((1, D), lambda i, j: (0, 0)),
                  pl.BlockSpec((D, tn), lambda i, j: (0, j))],
        out_specs=out_spec,
        scratch_shapes=[pltpu.VMEM((tm, D), BF16)],
        compiler_params=_cparams(("parallel", "arbitrary")),
        name=name,
    )(x2, nw, w)


def _retention_kernel(q_ref, k_ref, v_ref, g_ref, cos_ref, sin_ref, decay_ref, xi_ref, zeta_ref,
                      o_ref, r_ref):
    L = RET_CHUNK
    S = q_ref.shape[0]
    r_ref[...] = jnp.zeros_like(r_ref)
    decay = decay_ref[...]
    xi = xi_ref[...]
    zeta = zeta_ref[...]
    gamma_l = xi[L - 1:L, :]

    def rope(x, cos, sin_signed):
        return x * cos + pltpu.roll(x, RET_HEAD_DIM // 2, 1) * sin_signed

    def body(n, carry):
        s0 = pl.multiple_of(n * L, L)
        cos = cos_ref[pl.ds(s0, L), :]
        sin = sin_ref[pl.ds(s0, L), :]
        q = rope(q_ref[pl.ds(s0, L), :].astype(F32), cos, sin)
        k = rope(k_ref[pl.ds(s0, L), :].astype(F32), cos, sin)
        v = v_ref[pl.ds(s0, L), :]
        qb = q.astype(BF16)
        scores = lax.dot_general(qb, k.astype(BF16), (((1,), (1,)), ((), ())),
                                 preferred_element_type=F32) * decay
        intra = jnp.dot(scores.astype(BF16), v, preferred_element_type=F32)
        r_prev = r_ref[...]
        inter = jnp.dot(qb, r_prev.astype(BF16), preferred_element_type=F32) * xi
        o = intra + inter
        o = o * lax.rsqrt(jnp.mean(o * o, axis=-1, keepdims=True) + EPS)
        g = g_ref[pl.ds(s0, L), :].astype(F32)
        o_ref[pl.ds(s0, L), :] = (o * (g * jax.nn.sigmoid(g))).astype(o_ref.dtype)
        kz_t = (k * zeta).T.astype(BF16)
        r_ref[...] = r_prev * gamma_l + jnp.dot(kz_t, v, preferred_element_type=F32)
        return carry

    lax.fori_loop(0, S // L, body, 0)


def _retention(proj, cos_t, sin_t, decay, xi_b, zeta_b, B, S):
    H, dh, L = RET_HEADS, RET_HEAD_DIM, RET_CHUNK

    def pspec(base):
        return pl.BlockSpec((None, S, LANES), lambda b, h, base=base: (base + h, b, 0))

    return pl.pallas_call(
        _retention_kernel,
        out_shape=jax.ShapeDtypeStruct((B * S, RET_W), BF16),
        grid=(B, H),
        in_specs=[pspec(_BLK_RQ), pspec(_BLK_RK), pspec(_BLK_RV), pspec(_BLK_RG),
                  pl.BlockSpec((S, dh), lambda b, h: (0, 0)),
                  pl.BlockSpec((S, dh), lambda b, h: (0, 0)),
                  pl.BlockSpec((None, L, L), lambda b, h: (h, 0, 0)),
                  pl.BlockSpec((None, L, dh), lambda b, h: (h, 0, 0)),
                  pl.BlockSpec((None, L, dh), lambda b, h: (h, 0, 0))],
        out_specs=pl.BlockSpec((S, dh), lambda b, h: (b, h)),
        scratch_shapes=[pltpu.VMEM((dh, dh), F32)],
        compiler_params=_cparams(("parallel", "parallel")),
        name="retention",
    )(proj, proj, proj, proj, cos_t, sin_t, decay, xi_b, zeta_b)


def _mlstm_kernel(q_ref, k_ref, v_ref, og_ref, gp_ref, gb_ref, cwq_ref, cwk_ref, cbq_ref, cbk_ref,
                  nw_ref, o_ref, pad_ref, qc_ref, kc_ref, c_ref, n_ref, m_ref):
    L = MLSTM_CHUNK
    S = q_ref.shape[0]
    NH = MLSTM_HEADS
    KC = MLSTM_CONV
    h = pl.program_id(1)
    n_chunks = S // L

    def conv_silu(src_ref, cw_ref, cb_ref, dst_ref, scale):
        pad_ref[0:SUBLANES, :] = jnp.zeros((SUBLANES, LANES), F32)
        pad_ref[SUBLANES:SUBLANES + S, :] = src_ref[...].astype(F32)
        cw = cw_ref[...]
        cb = cb_ref[...]
        for c in range(n_chunks):
            acc = cb
            for t in range(KC):
                r0 = SUBLANES + c * L - (KC - 1) + t
                acc = acc + pad_ref[r0:r0 + L, :] * cw[t:t + 1, :]
            y = acc * jax.nn.sigmoid(acc)
            dst_ref[c * L:(c + 1) * L, :] = y * scale

    conv_silu(q_ref, cwq_ref, cbq_ref, qc_ref, MLSTM_QK_DIM ** -0.5)
    conv_silu(k_ref, cwk_ref, cbk_ref, kc_ref, 1.0)

    c_ref[...] = jnp.zeros_like(c_ref)
    n_ref[...] = jnp.zeros_like(n_ref)
    m_ref[...] = jnp.zeros_like(m_ref)

    lane = lax.broadcasted_iota(jnp.int32, (L, LANES), 1)
    sub = lax.broadcasted_iota(jnp.int32, (L, LANES), 0)
    sub_col = lax.broadcasted_iota(jnp.int32, (L, 1), 0)
    tri = (sub >= lane).astype(F32)
    causal = sub >= lane
    is_f = (lane >= NH) & (lane < 2 * NH)
    nw = nw_ref[...]

    def body(n, carry):
        s0 = pl.multiple_of(n * L, L)
        gp = gp_ref[pl.ds(s0, L), :] + gb_ref[...]
        logsig = jnp.minimum(gp, 0.0) - jnp.log1p(jnp.exp(-jnp.abs(gp)))
        a = jnp.where(is_f, logsig, gp)
        bc = jnp.dot(tri, a, preferred_element_type=F32, precision=lax.Precision.HIGHEST)
        a_t = a.T
        bc_t = bc.T
        li_col = jnp.sum(jnp.where(lane == h, a, 0.0), axis=1, keepdims=True)
        b_col = jnp.sum(jnp.where(lane == h + NH, bc, 0.0), axis=1, keepdims=True)
        li_row = jnp.sum(jnp.where(sub == h, a_t, 0.0), axis=0, keepdims=True)
        b_row = jnp.sum(jnp.where(sub == h + NH, bc_t, 0.0), axis=0, keepdims=True)
        gtot = jnp.max(jnp.where(sub_col == L - 1, b_col, NEG_BIG), axis=0, keepdims=True)
        m_prev = m_ref[...][:, 0:1]

        q = qc_ref[pl.ds(s0, L), :]
        k = kc_ref[pl.ds(s0, L), :]
        v = jnp.concatenate([v_ref[0, pl.ds(s0, L), :], v_ref[1, pl.ds(s0, L), :]], axis=1)
        qb = q.astype(BF16)

        dlog = jnp.where(causal, b_col - b_row + li_row, NEG_BIG)
        inter_log = b_col + m_prev
        m_pos = jnp.maximum(inter_log, jnp.max(dlog, axis=1, keepdims=True))
        qk = lax.dot_general(qb, k.astype(BF16), (((1,), (1,)), ((), ())),
                             preferred_element_type=F32)
        s_intra = qk * jnp.exp(dlog - m_pos)
        inter_scale = jnp.exp(inter_log - m_pos)
        c_prev = c_ref[...]
        n_prev = n_ref[...]
        num = (jnp.dot(s_intra.astype(BF16), v, preferred_element_type=F32)
               + inter_scale * jnp.dot(qb, c_prev.astype(BF16), preferred_element_type=F32))
        den = (jnp.sum(s_intra, axis=1, keepdims=True)
               + inter_scale * jnp.sum(q * n_prev, axis=1, keepdims=True))
        hh = num / jnp.maximum(jnp.abs(den), jnp.exp(-m_pos))
        hh = hh * lax.rsqrt(jnp.mean(hh * hh, axis=-1, keepdims=True) + EPS) * nw
        og = jnp.concatenate([og_ref[0, pl.ds(s0, L), :], og_ref[1, pl.ds(s0, L), :]],
                             axis=1).astype(F32)
        o_ref[pl.ds(s0, L), :] = (hh * jax.nn.sigmoid(og)).astype(o_ref.dtype)

        w_loc = gtot - b_col + li_col
        m_loc = jnp.max(w_loc, axis=0, keepdims=True)
        ke = k * jnp.exp(w_loc - m_loc)
        c_loc = jnp.dot(ke.T.astype(BF16), v, preferred_element_type=F32)
        n_loc = jnp.sum(ke, axis=0, keepdims=True)
        m_new = jnp.maximum(gtot + m_prev, m_loc)
        a_dec = jnp.exp(gtot + m_prev - m_new)
        b_inc = jnp.exp(m_loc - m_new)
        c_ref[...] = a_dec * c_prev + b_inc * c_loc
        n_ref[...] = a_dec * n_prev + b_inc * n_loc
        m_ref[...] = jnp.broadcast_to(m_new, m_ref.shape)
        return carry

    lax.fori_loop(0, n_chunks, body, 0)


def _mlstm(proj, gpre, gate_b_row, conv_w, conv_b, norm_w, B, S):
    NH, dk, dv = MLSTM_HEADS, MLSTM_QK_DIM, MLSTM_V_DIM
    KC = MLSTM_CONV
    nvb = dv // LANES

    def pspec(base):
        return pl.BlockSpec((None, S, LANES), lambda b, h, base=base: (base + h, b, 0))

    def pspec2(base):
        return pl.BlockSpec((nvb, S, LANES), lambda b, h, base=base: (base // nvb + h, b, 0))

    return pl.pallas_call(
        _mlstm_kernel,
        out_shape=jax.ShapeDtypeStruct((B * S, MLSTM_V_W), BF16),
        grid=(B, NH),
        in_specs=[pspec(_BLK_MQ), pspec(_BLK_MK), pspec2(_BLK_MV), pspec2(_BLK_MO),
                  pl.BlockSpec((S, LANES), lambda b, h: (b, 0)),
                  pl.BlockSpec((1, LANES), lambda b, h: (0, 0)),
                  pl.BlockSpec((KC, dk), lambda b, h: (0, h)),
                  pl.BlockSpec((KC, dk), lambda b, h: (0, NH + h)),
                  pl.BlockSpec((1, dk), lambda b, h: (0, h)),
                  pl.BlockSpec((1, dk), lambda b, h: (0, NH + h)),
                  pl.BlockSpec((1, dv), lambda b, h: (0, h))],
        out_specs=pl.BlockSpec((S, dv), lambda b, h: (b, h)),
        scratch_shapes=[pltpu.VMEM((S + SUBLANES, dk), F32),
                        pltpu.VMEM((S, dk), F32),
                        pltpu.VMEM((S, dk), F32),
                        pltpu.VMEM((dk, dv), F32),
                        pltpu.VMEM((1, dk), F32),
                        pltpu.VMEM((1, LANES), F32)],
        compiler_params=_cparams(("parallel", "parallel")),
        name="mlstm",
    )(proj, proj, proj, proj, gpre, gate_b_row, conv_w, conv_w, conv_b, conv_b, norm_w)


def _moba_kernel(q_ref, k_ref, v_ref, bias_ref, far_ref, o_ref,
                 kmean_ref, vt_ref, gate_ref, sel_ref, m_ref, l_ref, acc_ref):
    BS = MOBA_BLOCK
    S = k_ref.shape[0]
    NB = S // BS
    i = pl.program_id(2)
    nt = (((1,), (1,)), ((), ()))

    @pl.when(i == 0)
    def _():
        for j in range(NB):
            kj = k_ref[j * BS:(j + 1) * BS, :].astype(F32)
            kmean_ref[j:j + 1, :] = jnp.mean(kj, axis=0, keepdims=True)
            vt_ref[j] = v_ref[j * BS:(j + 1) * BS, :].astype(F32).T.astype(BF16)

    q = q_ref[...]
    gate = lax.dot_general(kmean_ref[...], q.astype(F32), nt, preferred_element_type=F32,
                           precision=lax.Precision.HIGHEST)
    blk = lax.broadcasted_iota(jnp.int32, gate.shape, 0)
    gate_ref[...] = jnp.where(blk < i, gate, -jnp.inf)
    rows = [gate_ref[n:n + 1, :] for n in range(NB)]
    for n in range(NB):
        rank = jnp.zeros_like(rows[n])
        for mth in range(NB):
            if mth == n:
                continue
            ahead = rows[mth] > rows[n]
            if mth < n:
                ahead = ahead | (rows[mth] == rows[n])
            rank = rank + ahead.astype(F32)
        sel_ref[n:n + 1, :] = jnp.where(rank < MOBA_TOPK, jnp.where(n < i, 1.0, 0.0), 0.0)

    kl = lax.broadcasted_iota(jnp.int32, (BS, BS), 0)
    ql = lax.broadcasted_iota(jnp.int32, (BS, BS), 1)

    def tile(j, logits_t, first):
        m_prev = m_ref[...]
        m_new = jnp.maximum(m_prev, jnp.max(logits_t, axis=0, keepdims=True))
        p = jnp.exp(logits_t - m_new)
        pv = jnp.dot(vt_ref[j], p.astype(BF16), preferred_element_type=F32)
        if first:
            l_ref[...] = jnp.sum(p, axis=0, keepdims=True)
            acc_ref[...] = pv
        else:
            alpha = jnp.exp(m_prev - m_new)
            l_ref[...] = alpha * l_ref[...] + jnp.sum(p, axis=0, keepdims=True)
            acc_ref[...] = alpha * acc_ref[...] + pv
        m_ref[...] = m_new

    m_ref[...] = jnp.full(m_ref.shape, NEG_BIG, F32)
    k_own = k_ref[pl.ds(pl.multiple_of(i * BS, BS), BS), :]
    lg = lax.dot_general(k_own, q, nt, preferred_element_type=F32) + bias_ref[0]
    tile(i, jnp.where(kl <= ql, lg, NEG_BIG), True)

    @pl.when(i >= 1)
    def _():
        j = i - 1
        kj = k_ref[pl.ds(pl.multiple_of(j * BS, BS), BS), :]
        lgp = lax.dot_general(kj, q, nt, preferred_element_type=F32) + bias_ref[1]
        sel = sel_ref[pl.ds(j, 1), :]
        tile(j, jnp.where(sel > 0.5, lgp, NEG_BIG), False)

    def far_body(j, carry):
        kj = k_ref[pl.ds(pl.multiple_of(j * BS, BS), BS), :]
        lgf = lax.dot_general(kj, q, nt, preferred_element_type=F32) + far_ref[...]
        sel = sel_ref[pl.ds(j, 1), :]
        tile(j, jnp.where(sel > 0.5, lgf, NEG_BIG), False)
        return carry

    lax.fori_loop(0, jnp.maximum(i - 1, 0), far_body, 0)

    out_t = acc_ref[...] / l_ref[...]
    o_ref[...] = out_t.T.astype(o_ref.dtype)


def _moba(proj, bias_tiles, far_rows, B, S):
    H, dh, BS = MOBA_HEADS, MOBA_HEAD_DIM, MOBA_BLOCK
    NB = S // BS

    def kvspec(base):
        return pl.BlockSpec((None, S, LANES), lambda b, h, i, base=base: (base + h, b, 0))

    return pl.pallas_call(
        _moba_kernel,
        out_shape=jax.ShapeDtypeStruct((B * S, MOBA_W), BF16),
        grid=(B, H, NB),
        in_specs=[pl.BlockSpec((None, BS, LANES), lambda b, h, i: (_BLK_BQ + h, b * NB + i, 0)),
                  kvspec(_BLK_BK), kvspec(_BLK_BV),
                  pl.BlockSpec((None, 2, BS, BS), lambda b, h, i: (h, 0, 0, 0)),
                  pl.BlockSpec((None, 1, BS), lambda b, h, i: (h, 0, 0))],
        out_specs=pl.BlockSpec((BS, dh), lambda b, h, i: (b * NB + i, h)),
        scratch_shapes=[pltpu.VMEM((NB, dh), F32),
                        pltpu.VMEM((NB, dh, BS), BF16),
                        pltpu.VMEM((NB, BS), F32),
                        pltpu.VMEM((NB, BS), F32),
                        pltpu.VMEM((1, BS), F32),
                        pltpu.VMEM((1, BS), F32),
                        pltpu.VMEM((dh, BS), F32)],
        compiler_params=_cparams(("parallel", "parallel", "arbitrary")),
        name="moba",
    )(proj, proj, proj, bias_tiles, far_rows)


def _merge_kernel(yr_ref, ym_ref, yb_ref, gr_ref, gm_ref, gb_ref, x_ref,
                  wr_ref, wm_ref, wb_ref, wo_ref, o_ref):
    mixed = gr_ref[...].astype(F32) * jnp.dot(yr_ref[...], wr_ref[...], preferred_element_type=F32)
    mixed = mixed + gm_ref[...].astype(F32) * jnp.dot(ym_ref[...], wm_ref[...],
                                                      preferred_element_type=F32)
    mixed = mixed + gb_ref[...].astype(F32) * jnp.dot(yb_ref[...], wb_ref[...],
                                                      preferred_element_type=F32)
    o_ref[...] = x_ref[...] + jnp.dot(mixed.astype(BF16), wo_ref[...], preferred_element_type=F32)


def _merge(y_ret, y_ml, y_mb, sgates, x2, w_r, w_m, w_b, w_o, *, tm):
    T, D = x2.shape

    def resident(shape):
        return pl.BlockSpec(shape, lambda i: (0, 0), pipeline_mode=pl.Buffered(1))

    return pl.pallas_call(
        _merge_kernel,
        out_shape=jax.ShapeDtypeStruct((T, D), F32),
        grid=(T // tm,),
        in_specs=[pl.BlockSpec((tm, RET_W), lambda i: (i, 0)),
                  pl.BlockSpec((tm, MLSTM_V_W), lambda i: (i, 0)),
                  pl.BlockSpec((tm, MOBA_W), lambda i: (i, 0)),
                  pl.BlockSpec((tm, D), lambda i: (i, 0)),
                  pl.BlockSpec((tm, D), lambda i: (i, 1)),
                  pl.BlockSpec((tm, D), lambda i: (i, 2)),
                  pl.BlockSpec((tm, D), lambda i: (i, 0)),
                  resident((RET_W, D)), resident((MLSTM_V_W, D)), resident((MOBA_W, D)),
                  resident((D, D))],
        out_specs=pl.BlockSpec((tm, D), lambda i: (i, 0)),
        compiler_params=_cparams(("parallel",)),
        name="merge",
    )(y_ret, y_ml, y_mb, sgates, sgates, sgates, x2, w_r, w_m, w_b, w_o)


def _ffn_kernel(x_ref, nw_ref, w1_ref, w2_ref, o_ref, hn_ref, acc_ref, *, row_chunk):
    j = pl.program_id(1)
    tm = x_ref.shape[0]

    @pl.when(j == 0)
    def _():
        def body(c, carry):
            r = pl.multiple_of(c * row_chunk, row_chunk)
            hn_ref[pl.ds(r, row_chunk), :] = _rms_rows(x_ref[pl.ds(r, row_chunk), :],
                                                       nw_ref[...]).astype(BF16)
            return carry
        lax.fori_loop(0, tm // row_chunk, body, 0)
        acc_ref[...] = jnp.zeros_like(acc_ref)

    u = jnp.dot(hn_ref[...], w1_ref[...], preferred_element_type=F32)
    u = jnp.maximum(u, 0.0)
    acc_ref[...] += jnp.dot((u * u).astype(BF16), w2_ref[...], preferred_element_type=F32)

    @pl.when(j == pl.num_programs(1) - 1)
    def _():
        o_ref[...] = x_ref[...] + acc_ref[...]


def _ffn(x2, nw, w1, w2, *, tm, tf):
    T, D = x2.shape
    F = w1.shape[1]
    return pl.pallas_call(
        functools.partial(_ffn_kernel, row_chunk=128),
        out_shape=jax.ShapeDtypeStruct((T, D), F32),
        grid=(T // tm, F // tf),
        in_specs=[pl.BlockSpec((tm, D), lambda i, j: (i, 0)),
                  pl.BlockSpec((1, D), lambda i, j: (0, 0)),
                  pl.BlockSpec((D, tf), lambda i, j: (0, j)),
                  pl.BlockSpec((tf, D), lambda i, j: (j, 0))],
        out_specs=pl.BlockSpec((tm, D), lambda i, j: (i, 0)),
        scratch_shapes=[pltpu.VMEM((tm, D), BF16), pltpu.VMEM((tm, D), F32)],
        compiler_params=_cparams(("parallel", "arbitrary")),
        name="ffn",
    )(x2, nw, w1, w2)


def _final_norm_kernel(x_ref, nw_ref, o_ref):
    o_ref[...] = _rms_rows(x_ref[...], nw_ref[...])


def _final_norm(x2, nw, *, tm):
    T, D = x2.shape
    return pl.pallas_call(
        _final_norm_kernel,
        out_shape=jax.ShapeDtypeStruct((T, D), F32),
        grid=(T // tm,),
        in_specs=[pl.BlockSpec((tm, D), lambda i: (i, 0)),
                  pl.BlockSpec((1, D), lambda i: (0, 0))],
        out_specs=pl.BlockSpec((tm, D), lambda i: (i, 0)),
        compiler_params=_cparams(("parallel",)),
        name="final_norm",
    )(x2, nw)


def _rope_tables(S):
    half = RET_HEAD_DIM // 2
    inv = ROPE_BASE ** (-jnp.arange(half, dtype=F32) / half)
    ang = jnp.arange(S).astype(F32)[:, None] * inv[None, :]
    cos, sin = jnp.cos(ang), jnp.sin(ang)
    return jnp.concatenate([cos, cos], axis=-1), jnp.concatenate([-sin, sin], axis=-1)


def _retention_tables():
    H, L, dh = RET_HEADS, RET_CHUNK, RET_HEAD_DIM
    log_gamma = jnp.log1p(-jnp.exp2(-5.0 - jnp.arange(H, dtype=F32)))
    idx = jnp.arange(L, dtype=F32)
    diff = idx[:, None] - idx[None, :]
    decay = jnp.where(diff >= 0, jnp.exp(jnp.maximum(diff, 0.0) * log_gamma[:, None, None]), 0.0)
    zeta = jnp.exp((L - 1 - idx)[None, :] * log_gamma[:, None])
    xi = jnp.exp((idx + 1.0)[None, :] * log_gamma[:, None])
    return (decay, jnp.broadcast_to(xi[:, :, None], (H, L, dh)),
            jnp.broadcast_to(zeta[:, :, None], (H, L, dh)))


def _t5_bucket(dist):
    n = jnp.maximum(dist, 0)
    exact = REL_BUCKETS // 2
    nf = jnp.maximum(n, 1).astype(F32)
    large = exact + (jnp.log(nf / exact) / math.log(REL_MAX_DIST / exact)
                     * (REL_BUCKETS - exact)).astype(jnp.int32)
    large = jnp.minimum(large, REL_BUCKETS - 1)
    return jnp.where(n < exact, n, large)


def _moba_bias_tables(rel_bias, S):
    BS = MOBA_BLOCK
    assert BS + 1 >= REL_MAX_DIST and S % BS == 0
    table_t = rel_bias.T.astype(F32)
    kl = jnp.arange(BS)[:, None]
    ql = jnp.arange(BS)[None, :]
    own = table_t[:, _t5_bucket(ql - kl)]
    prev = table_t[:, _t5_bucket(ql - kl + BS)]
    far = table_t[:, REL_BUCKETS - 1]
    tiles = jnp.stack([own, prev], axis=1)
    far_rows = jnp.broadcast_to(far[:, None, None], (far.shape[0], 1, BS))
    return tiles, far_rows


def kernel(x, w_in, mlstm_gate_b, mlstm_conv_w, mlstm_conv_b, mlstm_norm_w, w_branch_ret,
           w_branch_mlstm, w_branch_moba, w_out, norm_mix_w, norm_mlp_w, w_ff1, w_ff2, rel_bias,
           final_norm_w):
    B, S, D = x.shape
    T = B * S
    depth = w_in.shape[0]
    x2 = x.reshape(T, D)

    cos_t, sin_t = _rope_tables(S)
    decay, xi_b, zeta_b = _retention_tables()
    bias_tiles, far_rows = _moba_bias_tables(rel_bias, S)

    for l in range(depth):
        wl = w_in[l]
        w_rq = wl[:, _OFF_RQ:_OFF_RQ + RET_W] * (RET_HEAD_DIM ** -0.5)
        w_bq = wl[:, _OFF_BQ:_OFF_BQ + MOBA_W] * (MOBA_HEAD_DIM ** -0.5)
        w_main = jnp.concatenate([w_rq, wl[:, RET_W:_OFF_MI], w_bq,
                                  wl[:, _OFF_BQ + MOBA_W:_OFF_GATES]], axis=1).astype(BF16)
        w_gpre = jnp.pad(wl[:, _OFF_MI:_OFF_BQ], ((0, 0), (0, LANES - 2 * MLSTM_HEADS))).astype(BF16)
        w_gates = wl[:, _OFF_GATES:].astype(BF16)
        nw_mix = norm_mix_w[l].reshape(1, D)

        proj = _norm_matmul(x2, nw_mix, w_main, tm=1024, tn=512, head_major=True, act=None,
                            out_dtype=BF16, name="in_proj")
        gpre = _norm_matmul(x2, nw_mix, w_gpre, tm=1024, tn=LANES, head_major=False, act=None,
                            out_dtype=F32, name="gate_pre")
        sgates = _norm_matmul(x2, nw_mix, w_gates, tm=1024, tn=512, head_major=False,
                              act="sigmoid", out_dtype=BF16, name="branch_gates")

        y_ret = _retention(proj, cos_t, sin_t, decay, xi_b, zeta_b, B, S)
        gate_b_row = jnp.pad(mlstm_gate_b[l].reshape(1, 2 * MLSTM_HEADS),
                             ((0, 0), (0, LANES - 2 * MLSTM_HEADS)))
        y_ml = _mlstm(proj, gpre, gate_b_row, mlstm_conv_w[l], mlstm_conv_b[l].reshape(1, -1),
                      mlstm_norm_w[l].reshape(1, -1), B, S)
        y_mb = _moba(proj, bias_tiles, far_rows, B, S)

        x2 = _merge(y_ret, y_ml, y_mb, sgates, x2,
                    w_branch_ret[l].astype(BF16), w_branch_mlstm[l].astype(BF16),
                    w_branch_moba[l].astype(BF16), w_out[l].astype(BF16), tm=256)
        x2 = _ffn(x2, norm_mlp_w[l].reshape(1, D), w_ff1[l].astype(BF16), w_ff2[l].astype(BF16),
                  tm=512, tf=512)

    return _final_norm(x2, final_norm_w.reshape(1, D), tm=512).reshape(B, S, D)
```

```python
import functools
import math

import numpy as np
import jax
import jax.numpy as jnp
from jax import lax
from jax.experimental import pallas as pl
from jax.experimental.pallas import tpu as pltpu

D_MODEL = 2048
RET_HEADS = 8
RET_HEAD_DIM = 128
RET_CHUNK = 128
MLSTM_HEADS = 4
MLSTM_QK_DIM = 128
MLSTM_V_DIM = 256
MLSTM_CHUNK = 128
MLSTM_CONV = 4
MOBA_HEADS = 8
MOBA_HEAD_DIM = 128
MOBA_BLOCK = 256
MOBA_TOPK = 3
REL_BUCKETS = 32
REL_MAX_DIST = 128
D_FF = 4 * D_MODEL
ROPE_BASE = 10000.0
EPS = 1e-6

RET_W = RET_HEADS * RET_HEAD_DIM
MLSTM_QK_W = MLSTM_HEADS * MLSTM_QK_DIM
MLSTM_V_W = MLSTM_HEADS * MLSTM_V_DIM
MOBA_W = MOBA_HEADS * MOBA_HEAD_DIM

LANES = 128
SUBLANES = 8
VMEM_LIMIT_BYTES = 56 * 1024 * 1024

_OFF_RQ = 0
_OFF_MI = 4 * RET_W + 2 * MLSTM_QK_W + 2 * MLSTM_V_W
_OFF_BQ = _OFF_MI + 2 * MLSTM_HEADS
_OFF_GATES = _OFF_BQ + 3 * MOBA_W
_N_HEADMAJOR = _OFF_MI + 3 * MOBA_W
_BLK_RQ, _BLK_RK, _BLK_RV, _BLK_RG = 0, 8, 16, 24
_BLK_MQ, _BLK_MK, _BLK_MV, _BLK_MO = 32, 36, 40, 48
_BLK_BQ, _BLK_BK, _BLK_BV = 56, 64, 72

NEG_BIG = -1e30

BF16 = jnp.bfloat16
F32 = jnp.float32


def _cparams(sem):
    return pltpu.CompilerParams(dimension_semantics=sem, vmem_limit_bytes=VMEM_LIMIT_BYTES)


def _rms_rows(xf, w_row):
    ms = jnp.mean(xf * xf, axis=-1, keepdims=True)
    return xf * lax.rsqrt(ms + EPS) * w_row


def _norm_matmul_kernel(x_ref, nw_ref, w_ref, o_ref, hn_ref, *, head_major, act, row_chunk):
    j = pl.program_id(1)
    tm = x_ref.shape[0]

    @pl.when(j == 0)
    def _():
        def body(c, carry):
            r = pl.multiple_of(c * row_chunk, row_chunk)
            xf = x_ref[pl.ds(r, row_chunk), :]
            hn_ref[pl.ds(r, row_chunk), :] = _rms_rows(xf, nw_ref[...]).astype(BF16)
            return carry
        lax.fori_loop(0, tm // row_chunk, body, 0)

    acc = jnp.dot(hn_ref[...], w_ref[...], preferred_element_type=F32)
    if act == "sigmoid":
        acc = jax.nn.sigmoid(acc)
    if head_major:
        for c in range(o_ref.shape[0]):
            o_ref[c] = acc[:, c * LANES:(c + 1) * LANES].astype(o_ref.dtype)
    else:
        o_ref[...] = acc.astype(o_ref.dtype)


def _norm_matmul(x2, nw, w, *, tm, tn, head_major, act, out_dtype, name):
    T, D = x2.shape
    N = w.shape[1]
    grid = (T // tm, N // tn)
    if head_major:
        out_shape = jax.ShapeDtypeStruct((N // LANES, T, LANES), out_dtype)
        out_spec = pl.BlockSpec((tn // LANES, tm, LANES), lambda i, j: (j, i, 0))
    else:
        out_shape = jax.ShapeDtypeStruct((T, N), out_dtype)
        out_spec = pl.BlockSpec((tm, tn), lambda i, j: (i, j))
    kern = functools.partial(_norm_matmul_kernel, head_major=head_major, act=act, row_chunk=128)
    return pl.pallas_call(
        kern,
        out_shape=out_shape,
        grid=grid,
        in_specs=[pl.BlockSpec((tm, D), lambda i, j: (i, 0)),
                  pl.BlockSpec((1, D), lambda i, j: (0, 0)),
                  pl.BlockSpec((D, tn), lambda i, j: (0, j))],
        out_specs=out_spec,
        scratch_shapes=[pltpu.VMEM((tm, D), BF16)],
        compiler_params=_cparams(("parallel", "arbitrary")),
        name=name,
    )(x2, nw, w)


def _retention_kernel(q_ref, k_ref, v_ref, g_ref, cos_ref, sin_ref, decay_ref, xi_ref, zeta_ref,
                      o_ref):
    L = RET_CHUNK
    S = q_ref.shape[0]
    decay = decay_ref[...]
    xi = xi_ref[...]
    zeta = zeta_ref[...]
    gamma_l = xi[L - 1:L, :]

    def rope(x, cos, sin_signed):
        return x * cos + pltpu.roll(x, RET_HEAD_DIM // 2, 1) * sin_signed

    r_prev = jnp.zeros((RET_HEAD_DIM, RET_HEAD_DIM), F32)
    for n in range(S // L):
        rows = slice(n * L, (n + 1) * L)
        cos = cos_ref[rows, :]
        sin = sin_ref[rows, :]
        q = rope(q_ref[rows, :].astype(F32), cos, sin)
        k = rope(k_ref[rows, :].astype(F32), cos, sin)
        v = v_ref[rows, :]
        qb = q.astype(BF16)
        scores = lax.dot_general(qb, k.astype(BF16), (((1,), (1,)), ((), ())),
                                 preferred_element_type=F32) * decay
        intra = jnp.dot(scores.astype(BF16), v, preferred_element_type=F32)
        inter = jnp.dot(qb, r_prev.astype(BF16), preferred_element_type=F32) * xi
        o = intra + inter
        o = o * lax.rsqrt(jnp.mean(o * o, axis=-1, keepdims=True) + EPS)
        g = g_ref[rows, :].astype(F32)
        o_ref[rows, :] = (o * (g * jax.nn.sigmoid(g))).astype(o_ref.dtype)
        kz_t = (k * zeta).T.astype(BF16)
        r_prev = r_prev * gamma_l + jnp.dot(kz_t, v, preferred_element_type=F32)


def _retention(proj, cos_t, sin_t, decay, xi_b, zeta_b, B, S):
    H, dh, L = RET_HEADS, RET_HEAD_DIM, RET_CHUNK

    def pspec(base):
        return pl.BlockSpec((None, S, LANES), lambda b, h, base=base: (base + h, b, 0))

    return pl.pallas_call(
        _retention_kernel,
        out_shape=jax.ShapeDtypeStruct((B * S, RET_W), BF16),
        grid=(B, H),
        in_specs=[pspec(_BLK_RQ), pspec(_BLK_RK), pspec(_BLK_RV), pspec(_BLK_RG),
                  pl.BlockSpec((S, dh), lambda b, h: (0, 0)),
                  pl.BlockSpec((S, dh), lambda b, h: (0, 0)),
                  pl.BlockSpec((None, L, L), lambda b, h: (h, 0, 0)),
                  pl.BlockSpec((None, L, dh), lambda b, h: (h, 0, 0)),
                  pl.BlockSpec((None, L, dh), lambda b, h: (h, 0, 0))],
        out_specs=pl.BlockSpec((S, dh), lambda b, h: (b, h)),
        compiler_params=_cparams(("parallel", "parallel")),
        name="retention",
    )(proj, proj, proj, proj, cos_t, sin_t, decay, xi_b, zeta_b)


def _mlstm_kernel(q_ref, k_ref, v_ref, og_ref, gp_ref, gb_ref, cwq_ref, cwk_ref, cbq_ref, cbk_ref,
                  nw_ref, o_ref, pad_ref, qc_ref, kc_ref, a_ref, bc_ref, at_ref, bt_ref):
    L = MLSTM_CHUNK
    S = q_ref.shape[0]
    NH = MLSTM_HEADS
    KC = MLSTM_CONV
    h = pl.program_id(1)
    n_chunks = S // L

    def conv_silu(src_ref, cw_ref, cb_ref, dst_ref, scale):
        pad_ref[0:SUBLANES, :] = jnp.zeros((SUBLANES, LANES), F32)
        pad_ref[SUBLANES:SUBLANES + S, :] = src_ref[...].astype(F32)
        cw = cw_ref[...]
        cb = cb_ref[...]
        for c in range(n_chunks):
            acc = cb
            for t in range(KC):
                r0 = SUBLANES + c * L - (KC - 1) + t
                acc = acc + pad_ref[r0:r0 + L, :] * cw[t:t + 1, :]
            y = acc * jax.nn.sigmoid(acc)
            dst_ref[c * L:(c + 1) * L, :] = y * scale

    conv_silu(q_ref, cwq_ref, cbq_ref, qc_ref, MLSTM_QK_DIM ** -0.5)
    conv_silu(k_ref, cwk_ref, cbk_ref, kc_ref, 1.0)

    lane = lax.broadcasted_iota(jnp.int32, (L, LANES), 1)
    sub = lax.broadcasted_iota(jnp.int32, (L, LANES), 0)
    sub_col = lax.broadcasted_iota(jnp.int32, (L, 1), 0)
    causal = sub >= lane
    nw = nw_ref[...]

    @pl.when(h == 0)
    def _():
        tri = (sub >= lane).astype(F32)
        is_f = (lane >= NH) & (lane < 2 * NH)
        for n in range(n_chunks):
            rows = slice(n * L, (n + 1) * L)
            gp = gp_ref[rows, :] + gb_ref[...]
            logsig = jnp.minimum(gp, 0.0) - jnp.log1p(jnp.exp(-jnp.abs(gp)))
            a = jnp.where(is_f, logsig, gp)
            bc = jnp.dot(tri, a, preferred_element_type=F32, precision=lax.Precision.HIGHEST)
            a_ref[rows, :] = a
            bc_ref[rows, :] = bc
            at_ref[n * SUBLANES:(n + 1) * SUBLANES, :] = a.T[0:SUBLANES, :]
            bt_ref[n * SUBLANES:(n + 1) * SUBLANES, :] = bc.T[0:SUBLANES, :]

    c_prev = jnp.zeros((MLSTM_QK_DIM, MLSTM_V_DIM), F32)
    n_prev = jnp.zeros((1, MLSTM_QK_DIM), F32)
    m_prev = jnp.zeros((1, 1), F32)
    for n in range(n_chunks):
        rows = slice(n * L, (n + 1) * L)
        li_col = jnp.sum(jnp.where(lane == h, a_ref[rows, :], 0.0), axis=1, keepdims=True)
        b_col = jnp.sum(jnp.where(lane == h + NH, bc_ref[rows, :], 0.0), axis=1, keepdims=True)
        li_row = at_ref[pl.ds(n * SUBLANES + h, 1), :]
        b_row = bt_ref[pl.ds(n * SUBLANES + NH + h, 1), :]
        gtot = jnp.max(jnp.where(sub_col == L - 1, b_col, NEG_BIG), axis=0, keepdims=True)

        q = qc_ref[rows, :]
        k = kc_ref[rows, :]
        v = jnp.concatenate([v_ref[0, rows, :], v_ref[1, rows, :]], axis=1)
        qb = q.astype(BF16)

        dlog = jnp.where(causal, b_col - b_row + li_row, NEG_BIG)
        inter_log = b_col + m_prev
        m_pos = jnp.maximum(inter_log, jnp.max(dlog, axis=1, keepdims=True))
        qk = lax.dot_general(qb, k.astype(BF16), (((1,), (1,)), ((), ())),
                             preferred_element_type=F32)
        s_intra = qk * jnp.exp(dlog - m_pos)
        inter_scale = jnp.exp(inter_log - m_pos)
        num = (jnp.dot(s_intra.astype(BF16), v, preferred_element_type=F32)
               + inter_scale * jnp.dot(qb, c_prev.astype(BF16), preferred_element_type=F32))
        den = (jnp.sum(s_intra, axis=1, keepdims=True)
               + inter_scale * jnp.sum(q * n_prev, axis=1, keepdims=True))
        inv = 1.0 / jnp.maximum(jnp.abs(den), jnp.exp(-m_pos))
        ms = jnp.mean(num * num, axis=-1, keepdims=True)
        row_scale = inv * lax.rsqrt(inv * inv * ms + EPS)
        og = jnp.concatenate([og_ref[0, rows, :], og_ref[1, rows, :]], axis=1).astype(F32)
        o_ref[rows, :] = (num * row_scale * nw * jax.nn.sigmoid(og)).astype(o_ref.dtype)

        w_loc = gtot - b_col + li_col
        m_loc = jnp.max(w_loc, axis=0, keepdims=True)
        ke = k * jnp.exp(w_loc - m_loc)
        c_loc = jnp.dot(ke.T.astype(BF16), v, preferred_element_type=F32)
        n_loc = jnp.sum(ke, axis=0, keepdims=True)
        m_new = jnp.maximum(gtot + m_prev, m_loc)
        a_dec = jnp.exp(gtot + m_prev - m_new)
        b_inc = jnp.exp(m_loc - m_new)
        c_prev = a_dec * c_prev + b_inc * c_loc
        n_prev = a_dec * n_prev + b_inc * n_loc
        m_prev = m_new


def _mlstm(proj, gpre, gate_b_row, conv_w, conv_b, norm_w, B, S):
    NH, dk, dv = MLSTM_HEADS, MLSTM_QK_DIM, MLSTM_V_DIM
    KC = MLSTM_CONV
    nvb = dv // LANES

    def pspec(base):
        return pl.BlockSpec((None, S, LANES), lambda b, h, base=base: (base + h, b, 0))

    def pspec2(base):
        return pl.BlockSpec((nvb, S, LANES), lambda b, h, base=base: (base // nvb + h, b, 0))

    return pl.pallas_call(
        _mlstm_kernel,
        out_shape=jax.ShapeDtypeStruct((B * S, MLSTM_V_W), BF16),
        grid=(B, NH),
        in_specs=[pspec(_BLK_MQ), pspec(_BLK_MK), pspec2(_BLK_MV), pspec2(_BLK_MO),
                  pl.BlockSpec((S, LANES), lambda b, h: (b, 0)),
                  pl.BlockSpec((1, LANES), lambda b, h: (0, 0)),
                  pl.BlockSpec((KC, dk), lambda b, h: (0, h)),
                  pl.BlockSpec((KC, dk), lambda b, h: (0, NH + h)),
                  pl.BlockSpec((1, dk), lambda b, h: (0, h)),
                  pl.BlockSpec((1, dk), lambda b, h: (0, NH + h)),
                  pl.BlockSpec((1, dv), lambda b, h: (0, h))],
        out_specs=pl.BlockSpec((S, dv), lambda b, h: (b, h)),
        scratch_shapes=[pltpu.VMEM((S + SUBLANES, dk), F32),
                        pltpu.VMEM((S, dk), F32),
                        pltpu.VMEM((S, dk), F32),
                        pltpu.VMEM((S, LANES), F32),
                        pltpu.VMEM((S, LANES), F32),
                        pltpu.VMEM((S // MLSTM_CHUNK * SUBLANES, MLSTM_CHUNK), F32),
                        pltpu.VMEM((S // MLSTM_CHUNK * SUBLANES, MLSTM_CHUNK), F32)],
        compiler_params=_cparams(("parallel", "arbitrary")),
        name="mlstm",
    )(proj, proj, proj, proj, gpre, gate_b_row, conv_w, conv_w, conv_b, conv_b, norm_w)


def _moba_kernel(q_ref, k_ref, v_ref, bias_ref, far_ref, o_ref, kmean_ref, gate_ref, lg_ref):
    BS = MOBA_BLOCK
    S = k_ref.shape[0]
    NB = S // BS
    nt = (((1,), (1,)), ((), ()))

    for j in range(NB):
        kj = k_ref[j * BS:(j + 1) * BS, :].astype(F32)
        kmean_ref[j:j + 1, :] = jnp.mean(kj, axis=0, keepdims=True)
    vts = [v_ref[j * BS:(j + 1) * BS, :].astype(F32).T.astype(BF16) for j in range(NB)]
    far = far_ref[...]
    kl = lax.broadcasted_iota(jnp.int32, (BS, BS), 0)
    ql = lax.broadcasted_iota(jnp.int32, (BS, BS), 1)
    causal = kl <= ql

    for i in range(NB):
        q = q_ref[i * BS:(i + 1) * BS, :]
        if i <= MOBA_TOPK:
            add_rows = [None] * i
        else:
            gate_ref[...] = lax.dot_general(kmean_ref[...], q.astype(F32), nt,
                                            preferred_element_type=F32,
                                            precision=lax.Precision.HIGHEST)
            rows = [gate_ref[n:n + 1, :] for n in range(i)]
            add_rows = []
            for n in range(i):
                rank = jnp.zeros_like(rows[n])
                for mth in range(i):
                    if mth == n:
                        continue
                    ahead = rows[mth] > rows[n]
                    if mth < n:
                        ahead = ahead | (rows[mth] == rows[n])
                    rank = rank + jnp.where(ahead, 1.0, 0.0)
                add_rows.append(jnp.where(rank < MOBA_TOPK, 0.0, NEG_BIG))

        slot = i % 2
        m = None
        for j in range(i + 1):
            lg = lax.dot_general(k_ref[j * BS:(j + 1) * BS, :], q, nt, preferred_element_type=F32)
            if j == i:
                lg = jnp.where(causal, lg + bias_ref[0], NEG_BIG)
            else:
                lg = lg + (bias_ref[1] if j == i - 1 else far)
                if add_rows[j] is not None:
                    lg = lg + add_rows[j]
            lg_ref[slot, j] = lg
            mj = jnp.max(lg, axis=0, keepdims=True)
            m = mj if m is None else jnp.maximum(m, mj)

        l = None
        acc = None
        for j in range(i + 1):
            p = jnp.exp(lg_ref[slot, j] - m)
            lj = jnp.sum(p, axis=0, keepdims=True)
            pv = jnp.dot(vts[j], p.astype(BF16), preferred_element_type=F32)
            l = lj if l is None else l + lj
            acc = pv if acc is None else acc + pv
        o_ref[i * BS:(i + 1) * BS, :] = (acc * (1.0 / l)).T.astype(o_ref.dtype)


def _moba(proj, bias_tiles, far_rows, B, S):
    H, dh, BS = MOBA_HEADS, MOBA_HEAD_DIM, MOBA_BLOCK
    NB = S // BS

    def pspec(base):
        return pl.BlockSpec((None, S, LANES), lambda b, h, base=base: (base + h, b, 0))

    return pl.pallas_call(
        _moba_kernel,
        out_shape=jax.ShapeDtypeStruct((B * S, MOBA_W), BF16),
        grid=(B, H),
        in_specs=[pspec(_BLK_BQ), pspec(_BLK_BK), pspec(_BLK_BV),
                  pl.BlockSpec((None, 2, BS, BS), lambda b, h: (h, 0, 0, 0)),
                  pl.BlockSpec((None, 1, BS), lambda b, h: (h, 0, 0))],
        out_specs=pl.BlockSpec((S, dh), lambda b, h: (b, h)),
        scratch_shapes=[pltpu.VMEM((NB, dh), F32),
                        pltpu.VMEM((NB, BS), F32),
                        pltpu.VMEM((2, NB, BS, BS), F32)],
        compiler_params=_cparams(("parallel", "parallel")),
        name="moba",
    )(proj, proj, proj, bias_tiles, far_rows)


def _merge_kernel(yr_ref, ym_ref, yb_ref, gr_ref, gm_ref, gb_ref, x_ref,
                  wr_ref, wm_ref, wb_ref, wo_ref, o_ref):
    mixed = gr_ref[...].astype(F32) * jnp.dot(yr_ref[...], wr_ref[...], preferred_element_type=F32)
    mixed = mixed + gm_ref[...].astype(F32) * jnp.dot(ym_ref[...], wm_ref[...],
                                                      preferred_element_type=F32)
    mixed = mixed + gb_ref[...].astype(F32) * jnp.dot(yb_ref[...], wb_ref[...],
                                                      preferred_element_type=F32)
    o_ref[...] = x_ref[...] + jnp.dot(mixed.astype(BF16), wo_ref[...], preferred_element_type=F32)


def _merge(y_ret, y_ml, y_mb, sgates, x2, w_r, w_m, w_b, w_o, *, tm):
    T, D = x2.shape

    def resident(shape):
        return pl.BlockSpec(shape, lambda i: (0, 0), pipeline_mode=pl.Buffered(1))

    return pl.pallas_call(
        _merge_kernel,
        out_shape=jax.ShapeDtypeStruct((T, D), F32),
        grid=(T // tm,),
        in_specs=[pl.BlockSpec((tm, RET_W), lambda i: (i, 0)),
                  pl.BlockSpec((tm, MLSTM_V_W), lambda i: (i, 0)),
                  pl.BlockSpec((tm, MOBA_W), lambda i: (i, 0)),
                  pl.BlockSpec((tm, D), lambda i: (i, 0)),
                  pl.BlockSpec((tm, D), lambda i: (i, 1)),
                  pl.BlockSpec((tm, D), lambda i: (i, 2)),
                  pl.BlockSpec((tm, D), lambda i: (i, 0)),
                  resident((RET_W, D)), resident((MLSTM_V_W, D)), resident((MOBA_W, D)),
                  resident((D, D))],
        out_specs=pl.BlockSpec((tm, D), lambda i: (i, 0)),
        compiler_params=_cparams(("parallel",)),
        name="merge",
    )(y_ret, y_ml, y_mb, sgates, sgates, sgates, x2, w_r, w_m, w_b, w_o)


def _ffn_kernel(x_ref, nw_ref, w1_ref, w2_ref, o_ref, hn_ref, acc_ref, *, row_chunk):
    j = pl.program_id(1)
    tm = x_ref.shape[0]

    @pl.when(j == 0)
    def _():
        def body(c, carry):
            r = pl.multiple_of(c * row_chunk, row_chunk)
            hn_ref[pl.ds(r, row_chunk), :] = _rms_rows(x_ref[pl.ds(r, row_chunk), :],
                                                       nw_ref[...]).astype(BF16)
            return carry
        lax.fori_loop(0, tm // row_chunk, body, 0)
        acc_ref[...] = jnp.zeros_like(acc_ref)

    u = jnp.dot(hn_ref[...], w1_ref[...], preferred_element_type=F32)
    u = jnp.maximum(u, 0.0)
    acc_ref[...] += jnp.dot((u * u).astype(BF16), w2_ref[...], preferred_element_type=F32)

    @pl.when(j == pl.num_programs(1) - 1)
    def _():
        o_ref[...] = x_ref[...] + acc_ref[...]


def _ffn(x2, nw, w1, w2, *, tm, tf):
    T, D = x2.shape
    F = w1.shape[1]
    return pl.pallas_call(
        functools.partial(_ffn_kernel, row_chunk=128),
        out_shape=jax.ShapeDtypeStruct((T, D), F32),
        grid=(T // tm, F // tf),
        in_specs=[pl.BlockSpec((tm, D), lambda i, j: (i, 0)),
                  pl.BlockSpec((1, D), lambda i, j: (0, 0)),
                  pl.BlockSpec((D, tf), lambda i, j: (0, j)),
                  pl.BlockSpec((tf, D), lambda i, j: (j, 0))],
        out_specs=pl.BlockSpec((tm, D), lambda i, j: (i, 0)),
        scratch_shapes=[pltpu.VMEM((tm, D), BF16), pltpu.VMEM((tm, D), F32)],
        compiler_params=_cparams(("parallel", "arbitrary")),
        name="ffn",
    )(x2, nw, w1, w2)


def _final_norm_kernel(x_ref, nw_ref, o_ref):
    o_ref[...] = _rms_rows(x_ref[...], nw_ref[...])


def _final_norm(x2, nw, *, tm):
    T, D = x2.shape
    return pl.pallas_call(
        _final_norm_kernel,
        out_shape=jax.ShapeDtypeStruct((T, D), F32),
        grid=(T // tm,),
        in_specs=[pl.BlockSpec((tm, D), lambda i: (i, 0)),
                  pl.BlockSpec((1, D), lambda i: (0, 0))],
        out_specs=pl.BlockSpec((tm, D), lambda i: (i, 0)),
        compiler_params=_cparams(("parallel",)),
        name="final_norm",
    )(x2, nw)


def _rope_tables(S):
    half = RET_HEAD_DIM // 2
    inv = ROPE_BASE ** (-jnp.arange(half, dtype=F32) / half)
    ang = jnp.arange(S).astype(F32)[:, None] * inv[None, :]
    cos, sin = jnp.cos(ang), jnp.sin(ang)
    return jnp.concatenate([cos, cos], axis=-1), jnp.concatenate([-sin, sin], axis=-1)


def _retention_tables():
    H, L, dh = RET_HEADS, RET_CHUNK, RET_HEAD_DIM
    log_gamma = jnp.log1p(-jnp.exp2(-5.0 - jnp.arange(H, dtype=F32)))
    idx = jnp.arange(L, dtype=F32)
    diff = idx[:, None] - idx[None, :]
    decay = jnp.where(diff >= 0, jnp.exp(jnp.maximum(diff, 0.0) * log_gamma[:, None, None]), 0.0)
    zeta = jnp.exp((L - 1 - idx)[None, :] * log_gamma[:, None])
    xi = jnp.exp((idx + 1.0)[None, :] * log_gamma[:, None])
    return (decay, jnp.broadcast_to(xi[:, :, None], (H, L, dh)),
            jnp.broadcast_to(zeta[:, :, None], (H, L, dh)))


def _t5_bucket(dist):
    n = jnp.maximum(dist, 0)
    exact = REL_BUCKETS // 2
    nf = jnp.maximum(n, 1).astype(F32)
    large = exact + (jnp.log(nf / exact) / math.log(REL_MAX_DIST / exact)
                     * (REL_BUCKETS - exact)).astype(jnp.int32)
    large = jnp.minimum(large, REL_BUCKETS - 1)
    return jnp.where(n < exact, n, large)


def _moba_bias_tables(rel_bias, S):
    BS = MOBA_BLOCK
    assert BS + 1 >= REL_MAX_DIST and S % BS == 0
    table_t = rel_bias.T.astype(F32)
    kl = jnp.arange(BS)[:, None]
    ql = jnp.arange(BS)[None, :]
    buckets = jnp.stack([_t5_bucket(ql - kl), _t5_bucket(ql - kl + BS)])
    onehot = (buckets[..., None] == jnp.arange(REL_BUCKETS)).astype(F32)
    tiles = jnp.einsum("tkqb,hb->htkq", onehot, table_t,
                       precision=lax.Precision.HIGHEST)
    far = table_t[:, REL_BUCKETS - 1]
    far_rows = jnp.broadcast_to(far[:, None, None], (far.shape[0], 1, BS))
    return tiles, far_rows


def kernel(x, w_in, mlstm_gate_b, mlstm_conv_w, mlstm_conv_b, mlstm_norm_w, w_branch_ret,
           w_branch_mlstm, w_branch_moba, w_out, norm_mix_w, norm_mlp_w, w_ff1, w_ff2, rel_bias,
           final_norm_w):
    B, S, D = x.shape
    T = B * S
    depth = w_in.shape[0]
    x2 = x.reshape(T, D)

    cos_t, sin_t = _rope_tables(S)
    decay, xi_b, zeta_b = _retention_tables()
    bias_tiles, far_rows = _moba_bias_tables(rel_bias, S)

    for l in range(depth):
        wl = w_in[l]
        w_rq = wl[:, _OFF_RQ:_OFF_RQ + RET_W] * (RET_HEAD_DIM ** -0.5)
        w_bq = wl[:, _OFF_BQ:_OFF_BQ + MOBA_W] * (MOBA_HEAD_DIM ** -0.5)
        w_main = jnp.concatenate([w_rq, wl[:, RET_W:_OFF_MI], w_bq,
                                  wl[:, _OFF_BQ + MOBA_W:_OFF_GATES]], axis=1).astype(BF16)
        w_gpre = jnp.pad(wl[:, _OFF_MI:_OFF_BQ], ((0, 0), (0, LANES - 2 * MLSTM_HEADS))).astype(BF16)
        w_gates = wl[:, _OFF_GATES:].astype(BF16)
        nw_mix = norm_mix_w[l].reshape(1, D)

        proj = _norm_matmul(x2, nw_mix, w_main, tm=1024, tn=512, head_major=True, act=None,
                            out_dtype=BF16, name="in_proj")
        gpre = _norm_matmul(x2, nw_mix, w_gpre, tm=1024, tn=LANES, head_major=False, act=None,
                            out_dtype=F32, name="gate_pre")
        sgates = _norm_matmul(x2, nw_mix, w_gates, tm=1024, tn=512, head_major=False,
                              act="sigmoid", out_dtype=BF16, name="branch_gates")

        y_ret = _retention(proj, cos_t, sin_t, decay, xi_b, zeta_b, B, S)
        gate_b_row = jnp.pad(mlstm_gate_b[l].reshape(1, 2 * MLSTM_HEADS),
                             ((0, 0), (0, LANES - 2 * MLSTM_HEADS)))
        y_ml = _mlstm(proj, gpre, gate_b_row, mlstm_conv_w[l], mlstm_conv_b[l].reshape(1, -1),
                      mlstm_norm_w[l].reshape(1, -1), B, S)
        y_mb = _moba(proj, bias_tiles, far_rows, B, S)

        x2 = _merge(y_ret, y_ml, y_mb, sgates, x2,
                    w_branch_ret[l].astype(BF16), w_branch_mlstm[l].astype(BF16),
                    w_branch_moba[l].astype(BF16), w_out[l].astype(BF16), tm=256)
        x2 = _ffn(x2, norm_mlp_w[l].reshape(1, D), w_ff1[l].astype(BF16), w_ff2[l].astype(BF16),
                  tm=512, tf=512)

    return _final_norm(x2, final_norm_w.reshape(1, D), tm=512).reshape(B, S, D)
```

```python
import functools
import math

import numpy as np
import jax
import jax.numpy as jnp
from jax import lax
from jax.experimental import pallas as pl
from jax.experimental.pallas import tpu as pltpu

D_MODEL = 2048
RET_HEADS = 8
RET_HEAD_DIM = 128
RET_CHUNK = 128
MLSTM_HEADS = 4
MLSTM_QK_DIM = 128
MLSTM_V_DIM = 256
MLSTM_CHUNK = 128
MLSTM_CONV = 4
MOBA_HEADS = 8
MOBA_HEAD_DIM = 128
MOBA_BLOCK = 256
MOBA_TOPK = 3
REL_BUCKETS = 32
REL_MAX_DIST = 128
D_FF = 4 * D_MODEL
ROPE_BASE = 10000.0
EPS = 1e-6

RET_W = RET_HEADS * RET_HEAD_DIM
MLSTM_QK_W = MLSTM_HEADS * MLSTM_QK_DIM
MLSTM_V_W = MLSTM_HEADS * MLSTM_V_DIM
MOBA_W = MOBA_HEADS * MOBA_HEAD_DIM

LANES = 128
SUBLANES = 8
VMEM_LIMIT_BYTES = 56 * 1024 * 1024

_OFF_RQ = 0
_OFF_MI = 4 * RET_W + 2 * MLSTM_QK_W + 2 * MLSTM_V_W
_OFF_BQ = _OFF_MI + 2 * MLSTM_HEADS
_OFF_GATES = _OFF_BQ + 3 * MOBA_W
_N_HEADMAJOR = _OFF_MI + 3 * MOBA_W
_BLK_RQ, _BLK_RK, _BLK_RV, _BLK_RG = 0, 8, 16, 24
_BLK_MQ, _BLK_MK, _BLK_MV, _BLK_MO = 32, 36, 40, 48
_BLK_BQ, _BLK_BK, _BLK_BV = 56, 64, 72

NEG_BIG = -1e30
LOG2E = 1.0 / math.log(2.0)

_TILES = {
    "in_proj": dict(tm=1024, tn=1024),
    "gates": dict(tm=1024, tn=1024),
    "merge": dict(tm=256),
    "ffn": dict(tm=512, tf=1024),
}

BF16 = jnp.bfloat16
F32 = jnp.float32


def _cparams(sem):
    return pltpu.CompilerParams(dimension_semantics=sem, vmem_limit_bytes=VMEM_LIMIT_BYTES)


def _rms_rows(xf, w_row):
    ms = jnp.mean(xf * xf, axis=-1, keepdims=True)
    return xf * lax.rsqrt(ms + EPS) * w_row


def _norm_rows_to(x_ref, nw_ref, hn_ref, row_chunk):
    def body(c, carry):
        r = pl.multiple_of(c * row_chunk, row_chunk)
        hn_ref[pl.ds(r, row_chunk), :] = _rms_rows(x_ref[pl.ds(r, row_chunk), :],
                                                   nw_ref[...]).astype(BF16)
        return carry
    lax.fori_loop(0, x_ref.shape[0] // row_chunk, body, 0)


def _in_proj_kernel(x_ref, nw_ref, w_ref, o_ref, hn_ref, *, row_chunk):
    @pl.when(pl.program_id(1) == 0)
    def _():
        _norm_rows_to(x_ref, nw_ref, hn_ref, row_chunk)

    acc = jnp.dot(hn_ref[...], w_ref[...], preferred_element_type=F32)
    for c in range(o_ref.shape[0]):
        o_ref[c] = acc[:, c * LANES:(c + 1) * LANES].astype(o_ref.dtype)


def _in_proj(x2, nw, w, *, tm, tn):
    T, D = x2.shape
    N = w.shape[1]
    return pl.pallas_call(
        functools.partial(_in_proj_kernel, row_chunk=128),
        out_shape=(jax.ShapeDtypeStruct((N // LANES, T, LANES), BF16),
                   jax.ShapeDtypeStruct((T, D), BF16)),
        grid=(T // tm, N // tn),
        in_specs=[pl.BlockSpec((tm, D), lambda i, j: (i, 0)),
                  pl.BlockSpec((1, D), lambda i, j: (0, 0)),
                  pl.BlockSpec((D, tn), lambda i, j: (0, j))],
        out_specs=(pl.BlockSpec((tn // LANES, tm, LANES), lambda i, j: (j, i, 0)),
                   pl.BlockSpec((tm, D), lambda i, j: (i, 0))),
        compiler_params=_cparams(("parallel", "arbitrary")),
        name="in_proj",
    )(x2, nw, w)


def _matmul_kernel(h_ref, w_ref, o_ref, *, act):
    acc = jnp.dot(h_ref[...], w_ref[...], preferred_element_type=F32)
    if act == "sigmoid":
        acc = jax.nn.sigmoid(acc)
    o_ref[...] = acc.astype(o_ref.dtype)


def _matmul(hn, w, *, tm, tn, act, out_dtype, name):
    T, D = hn.shape
    N = w.shape[1]
    return pl.pallas_call(
        functools.partial(_matmul_kernel, act=act),
        out_shape=jax.ShapeDtypeStruct((T, N), out_dtype),
        grid=(T // tm, N // tn),
        in_specs=[pl.BlockSpec((tm, D), lambda i, j: (i, 0)),
                  pl.BlockSpec((D, tn), lambda i, j: (0, j))],
        out_specs=pl.BlockSpec((tm, tn), lambda i, j: (i, j)),
        compiler_params=_cparams(("parallel", "parallel")),
        name=name,
    )(hn, w)


def _retention_kernel(q_ref, k_ref, v_ref, g_ref, cos_ref, sin_ref, decay_ref, xi_ref, zeta_ref,
                      o_ref):
    L = RET_CHUNK
    S = q_ref.shape[0]
    decay = decay_ref[...]
    xi = xi_ref[...]
    zeta = zeta_ref[...]
    gamma_l = xi[L - 1:L, :]

    def rope(x, cos, sin_signed):
        return x * cos + pltpu.roll(x, RET_HEAD_DIM // 2, 1) * sin_signed

    r_prev = jnp.zeros((RET_HEAD_DIM, RET_HEAD_DIM), F32)
    for n in range(S // L):
        rows = slice(n * L, (n + 1) * L)
        cos = cos_ref[rows, :]
        sin = sin_ref[rows, :]
        q = rope(q_ref[rows, :].astype(F32), cos, sin)
        k = rope(k_ref[rows, :].astype(F32), cos, sin)
        v = v_ref[rows, :]
        qb = q.astype(BF16)
        scores = lax.dot_general(qb, k.astype(BF16), (((1,), (1,)), ((), ())),
                                 preferred_element_type=F32) * decay
        intra = jnp.dot(scores.astype(BF16), v, preferred_element_type=F32)
        inter = jnp.dot(qb, r_prev.astype(BF16), preferred_element_type=F32) * xi
        o = intra + inter
        o = o * lax.rsqrt(jnp.mean(o * o, axis=-1, keepdims=True) + EPS)
        g = g_ref[rows, :].astype(F32)
        o_ref[rows, :] = (o * (g * jax.nn.sigmoid(g))).astype(o_ref.dtype)
        kz_t = (k * zeta).T.astype(BF16)
        r_prev = r_prev * gamma_l + jnp.dot(kz_t, v, preferred_element_type=F32)


def _retention(proj, cos_t, sin_t, decay, xi_b, zeta_b, B, S):
    H, dh, L = RET_HEADS, RET_HEAD_DIM, RET_CHUNK

    def pspec(base):
        return pl.BlockSpec((None, S, LANES), lambda b, h, base=base: (base + h, b, 0))

    return pl.pallas_call(
        _retention_kernel,
        out_shape=jax.ShapeDtypeStruct((B * S, RET_W), BF16),
        grid=(B, H),
        in_specs=[pspec(_BLK_RQ), pspec(_BLK_RK), pspec(_BLK_RV), pspec(_BLK_RG),
                  pl.BlockSpec((S, dh), lambda b, h: (0, 0)),
                  pl.BlockSpec((S, dh), lambda b, h: (0, 0)),
                  pl.BlockSpec((None, L, L), lambda b, h: (h, 0, 0)),
                  pl.BlockSpec((None, L, dh), lambda b, h: (h, 0, 0)),
                  pl.BlockSpec((None, L, dh), lambda b, h: (h, 0, 0))],
        out_specs=pl.BlockSpec((S, dh), lambda b, h: (b, h)),
        compiler_params=_cparams(("parallel", "parallel")),
        name="retention",
    )(proj, proj, proj, proj, cos_t, sin_t, decay, xi_b, zeta_b)


def _mlstm_kernel(q_ref, k_ref, v_ref, og_ref, gp_ref, gb_ref, cwq_ref, cwk_ref, cbq_ref, cbk_ref,
                  nw_ref, o_ref, pad_ref, qc_ref, kc_ref, a_ref, bc_ref, at_ref, bt_ref):
    L = MLSTM_CHUNK
    S = q_ref.shape[0]
    NH = MLSTM_HEADS
    KC = MLSTM_CONV
    h = pl.program_id(1)
    n_chunks = S // L

    def conv_silu(src_ref, cw_ref, cb_ref, dst_ref, scale):
        pad_ref[0:SUBLANES, :] = jnp.zeros((SUBLANES, LANES), F32)
        pad_ref[SUBLANES:SUBLANES + S, :] = src_ref[...].astype(F32)
        cw = cw_ref[...]
        cb = cb_ref[...]
        for c in range(n_chunks):
            acc = cb
            for t in range(KC):
                r0 = SUBLANES + c * L - (KC - 1) + t
                acc = acc + pad_ref[r0:r0 + L, :] * cw[t:t + 1, :]
            y = acc * jax.nn.sigmoid(acc)
            dst_ref[c * L:(c + 1) * L, :] = y * scale

    conv_silu(q_ref, cwq_ref, cbq_ref, qc_ref, MLSTM_QK_DIM ** -0.5)
    conv_silu(k_ref, cwk_ref, cbk_ref, kc_ref, 1.0)

    lane = lax.broadcasted_iota(jnp.int32, (L, LANES), 1)
    sub = lax.broadcasted_iota(jnp.int32, (L, LANES), 0)
    sub_col = lax.broadcasted_iota(jnp.int32, (L, 1), 0)
    causal = sub >= lane
    nw = nw_ref[...]

    @pl.when(h == 0)
    def _():
        tri = (sub >= lane).astype(F32)
        is_f = (lane >= NH) & (lane < 2 * NH)
        for n in range(n_chunks):
            rows = slice(n * L, (n + 1) * L)
            gp = gp_ref[rows, :] + gb_ref[...]
            logsig = jnp.minimum(gp, 0.0) - jnp.log1p(jnp.exp(-jnp.abs(gp)))
            a = jnp.where(is_f, logsig, gp)
            bc = jnp.dot(tri, a, preferred_element_type=F32, precision=lax.Precision.HIGHEST)
            a_ref[rows, :] = a
            bc_ref[rows, :] = bc
            at_ref[n * SUBLANES:(n + 1) * SUBLANES, :] = a.T[0:SUBLANES, :]
            bt_ref[n * SUBLANES:(n + 1) * SUBLANES, :] = bc.T[0:SUBLANES, :]

    c_prev = jnp.zeros((MLSTM_QK_DIM, MLSTM_V_DIM), F32)
    n_prev = jnp.zeros((1, MLSTM_QK_DIM), F32)
    m_prev = jnp.zeros((1, 1), F32)
    for n in range(n_chunks):
        rows = slice(n * L, (n + 1) * L)
        li_col = jnp.sum(jnp.where(lane == h, a_ref[rows, :], 0.0), axis=1, keepdims=True)
        b_col = jnp.sum(jnp.where(lane == h + NH, bc_ref[rows, :], 0.0), axis=1, keepdims=True)
        li_row = at_ref[pl.ds(n * SUBLANES + h, 1), :]
        b_row = bt_ref[pl.ds(n * SUBLANES + NH + h, 1), :]
        gtot = jnp.max(jnp.where(sub_col == L - 1, b_col, NEG_BIG), axis=0, keepdims=True)

        q = qc_ref[rows, :]
        k = kc_ref[rows, :]
        v = jnp.concatenate([v_ref[0, rows, :], v_ref[1, rows, :]], axis=1)
        qb = q.astype(BF16)

        dlog = jnp.where(causal, b_col - b_row + li_row, NEG_BIG)
        inter_log = b_col + m_prev
        m_pos = jnp.maximum(inter_log, jnp.max(dlog, axis=1, keepdims=True))
        qk = lax.dot_general(qb, k.astype(BF16), (((1,), (1,)), ((), ())),
                             preferred_element_type=F32)
        s_intra = qk * jnp.exp(dlog - m_pos)
        inter_scale = jnp.exp(inter_log - m_pos)
        num = (jnp.dot(s_intra.astype(BF16), v, preferred_element_type=F32)
               + inter_scale * jnp.dot(qb, c_prev.astype(BF16), preferred_element_type=F32))
        den = (jnp.sum(s_intra, axis=1, keepdims=True)
               + inter_scale * jnp.sum(q * n_prev, axis=1, keepdims=True))
        inv = 1.0 / jnp.maximum(jnp.abs(den), jnp.exp(-m_pos))
        ms = jnp.mean(num * num, axis=-1, keepdims=True)
        row_scale = inv * lax.rsqrt(inv * inv * ms + EPS)
        og = jnp.concatenate([og_ref[0, rows, :], og_ref[1, rows, :]], axis=1).astype(F32)
        o_ref[rows, :] = (num * row_scale * nw * jax.nn.sigmoid(og)).astype(o_ref.dtype)

        w_loc = gtot - b_col + li_col
        m_loc = jnp.max(w_loc, axis=0, keepdims=True)
        ke = k * jnp.exp(w_loc - m_loc)
        c_loc = jnp.dot(ke.T.astype(BF16), v, preferred_element_type=F32)
        n_loc = jnp.sum(ke, axis=0, keepdims=True)
        m_new = jnp.maximum(gtot + m_prev, m_loc)
        a_dec = jnp.exp(gtot + m_prev - m_new)
        b_inc = jnp.exp(m_loc - m_new)
        c_prev = a_dec * c_prev + b_inc * c_loc
        n_prev = a_dec * n_prev + b_inc * n_loc
        m_prev = m_new


def _mlstm(proj, gpre, gate_b_row, conv_w, conv_b, norm_w, B, S):
    NH, dk, dv = MLSTM_HEADS, MLSTM_QK_DIM, MLSTM_V_DIM
    KC = MLSTM_CONV
    nvb = dv // LANES

    def pspec(base):
        return pl.BlockSpec((None, S, LANES), lambda b, h, base=base: (base + h, b, 0))

    def pspec2(base):
        return pl.BlockSpec((nvb, S, LANES), lambda b, h, base=base: (base // nvb + h, b, 0))

    return pl.pallas_call(
        _mlstm_kernel,
        out_shape=jax.ShapeDtypeStruct((B * S, MLSTM_V_W), BF16),
        grid=(B, NH),
        in_specs=[pspec(_BLK_MQ), pspec(_BLK_MK), pspec2(_BLK_MV), pspec2(_BLK_MO),
                  pl.BlockSpec((S, LANES), lambda b, h: (b, 0)),
                  pl.BlockSpec((1, LANES), lambda b, h: (0, 0)),
                  pl.BlockSpec((KC, dk), lambda b, h: (0, h)),
                  pl.BlockSpec((KC, dk), lambda b, h: (0, NH + h)),
                  pl.BlockSpec((1, dk), lambda b, h: (0, h)),
                  pl.BlockSpec((1, dk), lambda b, h: (0, NH + h)),
                  pl.BlockSpec((1, dv), lambda b, h: (0, h))],
        out_specs=pl.BlockSpec((S, dv), lambda b, h: (b, h)),
        scratch_shapes=[pltpu.VMEM((S + SUBLANES, dk), F32),
                        pltpu.VMEM((S, dk), F32),
                        pltpu.VMEM((S, dk), F32),
                        pltpu.VMEM((S, LANES), F32),
                        pltpu.VMEM((S, LANES), F32),
                        pltpu.VMEM((S // MLSTM_CHUNK * SUBLANES, MLSTM_CHUNK), F32),
                        pltpu.VMEM((S // MLSTM_CHUNK * SUBLANES, MLSTM_CHUNK), F32)],
        compiler_params=_cparams(("parallel", "arbitrary")),
        name="mlstm",
    )(proj, proj, proj, proj, gpre, gate_b_row, conv_w, conv_w, conv_b, conv_b, norm_w)


def _moba_kernel(q_ref, k_ref, v_ref, bvec_ref, far_ref, o_ref, kmean_ref, gate_ref, lg_ref,
                 bias_ref):
    BS = MOBA_BLOCK
    S = k_ref.shape[0]
    NB = S // BS
    nt = (((1,), (1,)), ((), ()))

    toep = pltpu.roll(jnp.broadcast_to(bvec_ref[...], (BS, 2 * BS)), 0, 1, stride=1, stride_axis=0)
    bias_ref[0] = toep[:, 0:BS]
    bias_ref[1] = toep[:, BS:2 * BS]

    for j in range(NB):
        kj = k_ref[j * BS:(j + 1) * BS, :].astype(F32)
        kmean_ref[j:j + 1, :] = jnp.mean(kj, axis=0, keepdims=True)
    ones_rows = jnp.ones((2 * SUBLANES, BS), BF16)
    vts = [jnp.concatenate([v_ref[j * BS:(j + 1) * BS, :].astype(F32).T.astype(BF16), ones_rows],
                           axis=0) for j in range(NB)]
    dh = MOBA_HEAD_DIM
    far = far_ref[...]
    kl = lax.broadcasted_iota(jnp.int32, (BS, BS), 0)
    ql = lax.broadcasted_iota(jnp.int32, (BS, BS), 1)
    causal = kl <= ql

    for i in range(NB):
        q = q_ref[i * BS:(i + 1) * BS, :]
        if i <= MOBA_TOPK:
            add_rows = [None] * i
        else:
            gate_ref[...] = lax.dot_general(kmean_ref[...], q.astype(F32), nt,
                                            preferred_element_type=F32,
                                            precision=lax.Precision.HIGHEST)
            rows = [gate_ref[n:n + 1, :] for n in range(i)]
            add_rows = []
            for n in range(i):
                rank = jnp.zeros_like(rows[n])
                for mth in range(i):
                    if mth == n:
                        continue
                    ahead = rows[mth] > rows[n]
                    if mth < n:
                        ahead = ahead | (rows[mth] == rows[n])
                    rank = rank + jnp.where(ahead, 1.0, 0.0)
                add_rows.append(jnp.where(rank < MOBA_TOPK, 0.0, NEG_BIG))

        slot = i % 2
        m = None
        for j in range(i + 1):
            lg = lax.dot_general(k_ref[j * BS:(j + 1) * BS, :], q, nt, preferred_element_type=F32)
            if j == i:
                lg = jnp.where(causal, lg + bias_ref[0], NEG_BIG)
            elif j == i - 1:
                lg = lg + bias_ref[1]
                if add_rows[j] is not None:
                    lg = lg + add_rows[j]
            else:
                lg = lg + (far if add_rows[j] is None else far + add_rows[j])
            lg_ref[slot, j] = lg
            mj = jnp.max(lg, axis=0, keepdims=True)
            m = mj if m is None else jnp.maximum(m, mj)

        acc = None
        for j in range(i + 1):
            p = jnp.exp2(lg_ref[slot, j] - m)
            pv = jnp.dot(vts[j], p.astype(BF16), preferred_element_type=F32)
            acc = pv if acc is None else acc + pv
        l = jnp.max(acc[dh:dh + SUBLANES, :], axis=0, keepdims=True)
        out_t = acc[0:dh, :] * (1.0 / l)
        o_ref[i * BS:(i + 1) * BS, :] = out_t.T.astype(o_ref.dtype)


def _moba(proj, bias_tiles, far_rows, B, S):
    H, dh, BS = MOBA_HEADS, MOBA_HEAD_DIM, MOBA_BLOCK
    NB = S // BS

    def pspec(base):
        return pl.BlockSpec((None, S, LANES), lambda b, h, base=base: (base + h, b, 0))

    return pl.pallas_call(
        _moba_kernel,
        out_shape=jax.ShapeDtypeStruct((B * S, MOBA_W), BF16),
        grid=(B, H),
        in_specs=[pspec(_BLK_BQ), pspec(_BLK_BK), pspec(_BLK_BV),
                  pl.BlockSpec((None, 1, 2 * BS), lambda b, h: (h, 0, 0)),
                  pl.BlockSpec((None, 1, BS), lambda b, h: (h, 0, 0))],
        out_specs=pl.BlockSpec((S, dh), lambda b, h: (b, h)),
        scratch_shapes=[pltpu.VMEM((NB, dh), F32),
                        pltpu.VMEM((NB, BS), F32),
                        pltpu.VMEM((2, NB, BS, BS), F32),
                        pltpu.VMEM((2, BS, BS), F32)],
        compiler_params=_cparams(("parallel", "parallel")),
        name="moba",
    )(proj, proj, proj, bias_tiles, far_rows)


def _merge_kernel(yr_ref, ym_ref, yb_ref, gr_ref, gm_ref, gb_ref, x_ref,
                  wr_ref, wm_ref, wb_ref, wo_ref, o_ref):
    mixed = gr_ref[...].astype(F32) * jnp.dot(yr_ref[...], wr_ref[...], preferred_element_type=F32)
    mixed = mixed + gm_ref[...].astype(F32) * jnp.dot(ym_ref[...], wm_ref[...],
                                                      preferred_element_type=F32)
    mixed = mixed + gb_ref[...].astype(F32) * jnp.dot(yb_ref[...], wb_ref[...],
                                                      preferred_element_type=F32)
    o_ref[...] = x_ref[...] + jnp.dot(mixed.astype(BF16), wo_ref[...], preferred_element_type=F32)


def _merge(y_ret, y_ml, y_mb, sgates, x2, w_r, w_m, w_b, w_o, *, tm):
    T, D = x2.shape

    def resident(shape):
        return pl.BlockSpec(shape, lambda i: (0, 0), pipeline_mode=pl.Buffered(1))

    return pl.pallas_call(
        _merge_kernel,
        out_shape=jax.ShapeDtypeStruct((T, D), F32),
        grid=(T // tm,),
        in_specs=[pl.BlockSpec((tm, RET_W), lambda i: (i, 0)),
                  pl.BlockSpec((tm, MLSTM_V_W), lambda i: (i, 0)),
                  pl.BlockSpec((tm, MOBA_W), lambda i: (i, 0)),
                  pl.BlockSpec((tm, D), lambda i: (i, 0)),
                  pl.BlockSpec((tm, D), lambda i: (i, 1)),
                  pl.BlockSpec((tm, D), lambda i: (i, 2)),
                  pl.BlockSpec((tm, D), lambda i: (i, 0)),
                  resident((RET_W, D)), resident((MLSTM_V_W, D)), resident((MOBA_W, D)),
                  resident((D, D))],
        out_specs=pl.BlockSpec((tm, D), lambda i: (i, 0)),
        compiler_params=_cparams(("parallel",)),
        name="merge",
    )(y_ret, y_ml, y_mb, sgates, sgates, sgates, x2, w_r, w_m, w_b, w_o)


def _ffn_kernel(x_ref, nw_ref, fw_ref, w1_ref, w2_ref, o_ref, hn_ref, acc_ref, *,
                row_chunk, final_norm):
    j = pl.program_id(1)

    @pl.when(j == 0)
    def _():
        _norm_rows_to(x_ref, nw_ref, hn_ref, row_chunk)
        acc_ref[...] = jnp.zeros_like(acc_ref)

    u = jnp.dot(hn_ref[...], w1_ref[...], preferred_element_type=F32)
    u = jnp.maximum(u, 0.0)
    acc_ref[...] += jnp.dot((u * u).astype(BF16), w2_ref[...], preferred_element_type=F32)

    @pl.when(j == pl.num_programs(1) - 1)
    def _():
        def body(c, carry):
            r = pl.multiple_of(c * row_chunk, row_chunk)
            y = x_ref[pl.ds(r, row_chunk), :] + acc_ref[pl.ds(r, row_chunk), :]
            if final_norm:
                y = _rms_rows(y, fw_ref[...])
            o_ref[pl.ds(r, row_chunk), :] = y
            return carry
        lax.fori_loop(0, x_ref.shape[0] // row_chunk, body, 0)


def _ffn(x2, nw, fw, w1, w2, *, tm, tf, final_norm):
    T, D = x2.shape
    F = w1.shape[1]
    return pl.pallas_call(
        functools.partial(_ffn_kernel, row_chunk=128, final_norm=final_norm),
        out_shape=jax.ShapeDtypeStruct((T, D), F32),
        grid=(T // tm, F // tf),
        in_specs=[pl.BlockSpec((tm, D), lambda i, j: (i, 0)),
                  pl.BlockSpec((1, D), lambda i, j: (0, 0)),
                  pl.BlockSpec((1, D), lambda i, j: (0, 0)),
                  pl.BlockSpec((D, tf), lambda i, j: (0, j)),
                  pl.BlockSpec((tf, D), lambda i, j: (j, 0))],
        out_specs=pl.BlockSpec((tm, D), lambda i, j: (i, 0)),
        scratch_shapes=[pltpu.VMEM((tm, D), BF16), pltpu.VMEM((tm, D), F32)],
        compiler_params=_cparams(("parallel", "arbitrary")),
        name="ffn",
    )(x2, nw, fw, w1, w2)


def _rope_tables(S):
    half = RET_HEAD_DIM // 2
    inv = ROPE_BASE ** (-jnp.arange(half, dtype=F32) / half)
    ang = jnp.arange(S).astype(F32)[:, None] * inv[None, :]
    cos, sin = jnp.cos(ang), jnp.sin(ang)
    return jnp.concatenate([cos, cos], axis=-1), jnp.concatenate([-sin, sin], axis=-1)


def _retention_tables():
    H, L, dh = RET_HEADS, RET_CHUNK, RET_HEAD_DIM
    log_gamma = jnp.log1p(-jnp.exp2(-5.0 - jnp.arange(H, dtype=F32)))
    idx = jnp.arange(L, dtype=F32)
    diff = idx[:, None] - idx[None, :]
    decay = jnp.where(diff >= 0, jnp.exp(jnp.maximum(diff, 0.0) * log_gamma[:, None, None]), 0.0)
    zeta = jnp.exp((L - 1 - idx)[None, :] * log_gamma[:, None])
    xi = jnp.exp((idx + 1.0)[None, :] * log_gamma[:, None])
    return (decay, jnp.broadcast_to(xi[:, :, None], (H, L, dh)),
            jnp.broadcast_to(zeta[:, :, None], (H, L, dh)))


def _t5_bucket(dist):
    n = jnp.maximum(dist, 0)
    exact = REL_BUCKETS // 2
    nf = jnp.maximum(n, 1).astype(F32)
    large = exact + (jnp.log(nf / exact) / math.log(REL_MAX_DIST / exact)
                     * (REL_BUCKETS - exact)).astype(jnp.int32)
    large = jnp.minimum(large, REL_BUCKETS - 1)
    return jnp.where(n < exact, n, large)


def _moba_bias_tables(rel_bias, S):
    BS = MOBA_BLOCK
    assert BS + 1 >= REL_MAX_DIST and S % BS == 0
    table_t = rel_bias.T.astype(F32) * LOG2E
    onehot = (_t5_bucket(jnp.arange(2 * BS))[:, None] == jnp.arange(REL_BUCKETS)).astype(F32)
    by_dist = jnp.einsum("db,hb->hd", onehot, table_t, precision=lax.Precision.HIGHEST)
    far = table_t[:, REL_BUCKETS - 1]
    far_rows = jnp.broadcast_to(far[:, None, None], (far.shape[0], 1, BS))
    return by_dist[:, None, :], far_rows


def kernel(x, w_in, mlstm_gate_b, mlstm_conv_w, mlstm_conv_b, mlstm_norm_w, w_branch_ret,
           w_branch_mlstm, w_branch_moba, w_out, norm_mix_w, norm_mlp_w, w_ff1, w_ff2, rel_bias,
           final_norm_w):
    B, S, D = x.shape
    T = B * S
    depth = w_in.shape[0]
    x2 = x.reshape(T, D)

    cos_t, sin_t = _rope_tables(S)
    decay, xi_b, zeta_b = _retention_tables()
    bias_by_dist, far_rows = _moba_bias_tables(rel_bias, S)
    final_w = final_norm_w.reshape(1, D)

    for l in range(depth):
        wl = w_in[l]
        w_rq = wl[:, _OFF_RQ:_OFF_RQ + RET_W] * (RET_HEAD_DIM ** -0.5)
        w_bq = wl[:, _OFF_BQ:_OFF_BQ + MOBA_W] * (MOBA_HEAD_DIM ** -0.5 * LOG2E)
        w_main = jnp.concatenate([w_rq, wl[:, RET_W:_OFF_MI], w_bq,
                                  wl[:, _OFF_BQ + MOBA_W:_OFF_GATES]], axis=1).astype(BF16)
        w_gpre = jnp.pad(wl[:, _OFF_MI:_OFF_BQ], ((0, 0), (0, LANES - 2 * MLSTM_HEADS))).astype(BF16)
        w_gates = wl[:, _OFF_GATES:].astype(BF16)

        proj, hn = _in_proj(x2, norm_mix_w[l].reshape(1, D), w_main, **_TILES["in_proj"])
        gpre = _matmul(hn, w_gpre, tm=_TILES["gates"]["tm"], tn=LANES, act=None, out_dtype=F32,
                       name="gate_pre")
        sgates = _matmul(hn, w_gates, act="sigmoid", out_dtype=BF16, name="branch_gates",
                         **_TILES["gates"])

        y_ret = _retention(proj, cos_t, sin_t, decay, xi_b, zeta_b, B, S)
        gate_b_row = jnp.pad(mlstm_gate_b[l].reshape(1, 2 * MLSTM_HEADS),
                             ((0, 0), (0, LANES - 2 * MLSTM_HEADS)))
        y_ml = _mlstm(proj, gpre, gate_b_row, mlstm_conv_w[l], mlstm_conv_b[l].reshape(1, -1),
                      mlstm_norm_w[l].reshape(1, -1), B, S)
        y_mb = _moba(proj, bias_by_dist, far_rows, B, S)

        x2 = _merge(y_ret, y_ml, y_mb, sgates, x2,
                    w_branch_ret[l].astype(BF16), w_branch_mlstm[l].astype(BF16),
                    w_branch_moba[l].astype(BF16), w_out[l].astype(BF16), **_TILES["merge"])
        x2 = _ffn(x2, norm_mlp_w[l].reshape(1, D), final_w, w_ff1[l].astype(BF16),
                  w_ff2[l].astype(BF16), final_norm=(l == depth - 1), **_TILES["ffn"])

    return x2.reshape(B, S, D)
```

```python
import functools
import math

import numpy as np
import jax
import jax.numpy as jnp
from jax import lax
from jax.experimental import pallas as pl
from jax.experimental.pallas import tpu as pltpu

D_MODEL = 2048
RET_HEADS = 8
RET_HEAD_DIM = 128
RET_CHUNK = 128
MLSTM_HEADS = 4
MLSTM_QK_DIM = 128
MLSTM_V_DIM = 256
MLSTM_CHUNK = 128
MLSTM_CONV = 4
MOBA_HEADS = 8
MOBA_HEAD_DIM = 128
MOBA_BLOCK = 256
MOBA_TOPK = 3
REL_BUCKETS = 32
REL_MAX_DIST = 128
D_FF = 4 * D_MODEL
ROPE_BASE = 10000.0
EPS = 1e-6

RET_W = RET_HEADS * RET_HEAD_DIM
MLSTM_QK_W = MLSTM_HEADS * MLSTM_QK_DIM
MLSTM_V_W = MLSTM_HEADS * MLSTM_V_DIM
MOBA_W = MOBA_HEADS * MOBA_HEAD_DIM

LANES = 128
SUBLANES = 8
VMEM_LIMIT_BYTES = 56 * 1024 * 1024

_OFF_RQ = 0
_OFF_MI = 4 * RET_W + 2 * MLSTM_QK_W + 2 * MLSTM_V_W
_OFF_BQ = _OFF_MI + 2 * MLSTM_HEADS
_OFF_GATES = _OFF_BQ + 3 * MOBA_W
_N_HEADMAJOR = _OFF_MI + 3 * MOBA_W
_BLK_RQ, _BLK_RK, _BLK_RV, _BLK_RG = 0, 8, 16, 24
_BLK_MQ, _BLK_MK, _BLK_MV, _BLK_MO = 32, 36, 40, 48
_BLK_BQ, _BLK_BK, _BLK_BV = 56, 64, 72

RET_SKEW = 1
NEG_BIG = -1e30
LOG2E = 1.0 / math.log(2.0)

_TILES = {
    "in_proj": dict(tm=1024, tn=1024),
    "gates": dict(tm=1024, tn=1024),
    "merge": dict(tm=256),
    "ffn": dict(tm=512, tf=1024),
}

BF16 = jnp.bfloat16
F32 = jnp.float32


def _cparams(sem):
    return pltpu.CompilerParams(dimension_semantics=sem, vmem_limit_bytes=VMEM_LIMIT_BYTES)


def _rms_rows(xf, w_row):
    ms = jnp.mean(xf * xf, axis=-1, keepdims=True)
    return xf * lax.rsqrt(ms + EPS) * w_row


def _norm_rows_to(x_ref, nw_ref, hn_ref, row_chunk):
    def body(c, carry):
        r = pl.multiple_of(c * row_chunk, row_chunk)
        hn_ref[pl.ds(r, row_chunk), :] = _rms_rows(x_ref[pl.ds(r, row_chunk), :],
                                                   nw_ref[...]).astype(BF16)
        return carry
    lax.fori_loop(0, x_ref.shape[0] // row_chunk, body, 0)


def _in_proj_kernel(x_ref, nw_ref, w_ref, ws_ref, o_ref, hn_ref, os_ref, *, row_chunk):
    @pl.when(pl.program_id(1) == 0)
    def _():
        _norm_rows_to(x_ref, nw_ref, hn_ref, row_chunk)
        os_ref[...] = jnp.dot(hn_ref[...], ws_ref[...], preferred_element_type=F32)

    acc = jnp.dot(hn_ref[...], w_ref[...], preferred_element_type=F32)
    for c in range(o_ref.shape[0]):
        o_ref[c] = acc[:, c * LANES:(c + 1) * LANES].astype(o_ref.dtype)


def _in_proj(x2, nw, w, w_side, *, tm, tn):
    T, D = x2.shape
    N = w.shape[1]
    NS = w_side.shape[1]
    return pl.pallas_call(
        functools.partial(_in_proj_kernel, row_chunk=128),
        out_shape=(jax.ShapeDtypeStruct((N // LANES, T, LANES), BF16),
                   jax.ShapeDtypeStruct((T, D), BF16),
                   jax.ShapeDtypeStruct((T, NS), F32)),
        grid=(T // tm, N // tn),
        in_specs=[pl.BlockSpec((tm, D), lambda i, j: (i, 0)),
                  pl.BlockSpec((1, D), lambda i, j: (0, 0)),
                  pl.BlockSpec((D, tn), lambda i, j: (0, j)),
                  pl.BlockSpec((D, NS), lambda i, j: (0, 0))],
        out_specs=(pl.BlockSpec((tn // LANES, tm, LANES), lambda i, j: (j, i, 0)),
                   pl.BlockSpec((tm, D), lambda i, j: (i, 0)),
                   pl.BlockSpec((tm, NS), lambda i, j: (i, 0))),
        compiler_params=_cparams(("parallel", "arbitrary")),
        name="in_proj",
    )(x2, nw, w, w_side)


def _matmul_kernel(h_ref, w_ref, o_ref, *, act):
    acc = jnp.dot(h_ref[...], w_ref[...], preferred_element_type=F32)
    if act == "sigmoid":
        acc = jax.nn.sigmoid(acc)
    o_ref[...] = acc.astype(o_ref.dtype)


def _matmul(hn, w, *, tm, tn, act, out_dtype, name):
    T, D = hn.shape
    N = w.shape[1]
    return pl.pallas_call(
        functools.partial(_matmul_kernel, act=act),
        out_shape=jax.ShapeDtypeStruct((T, N), out_dtype),
        grid=(T // tm, N // tn),
        in_specs=[pl.BlockSpec((tm, D), lambda i, j: (i, 0)),
                  pl.BlockSpec((D, tn), lambda i, j: (0, j))],
        out_specs=pl.BlockSpec((tm, tn), lambda i, j: (i, j)),
        compiler_params=_cparams(("parallel", "parallel")),
        name=name,
    )(hn, w)


def _retention_kernel(q_ref, k_ref, v_ref, g_ref, cos_ref, sin_ref, decay_ref, xi_ref, zeta_ref,
                      o_ref):
    L = RET_CHUNK
    S = q_ref.shape[0]
    decay = decay_ref[...]
    xi = xi_ref[...]
    zeta = zeta_ref[...]
    gamma_l = xi[L - 1:L, :]

    def rope(x, cos, sin_signed):
        return x * cos + pltpu.roll(x, RET_HEAD_DIM // 2, 1) * sin_signed

    def state_free(n):
        rows = slice(n * L, (n + 1) * L)
        cos = cos_ref[rows, :]
        sin = sin_ref[rows, :]
        q = rope(q_ref[rows, :].astype(F32), cos, sin)
        k = rope(k_ref[rows, :].astype(F32), cos, sin)
        v = v_ref[rows, :]
        qb = q.astype(BF16)
        scores = lax.dot_general(qb, k.astype(BF16), (((1,), (1,)), ((), ())),
                                 preferred_element_type=F32) * decay
        kv = jnp.dot((k * zeta).T.astype(BF16), v, preferred_element_type=F32)
        return qb, scores.astype(BF16), v, kv

    def finish(n, parts, r_prev):
        qb, scores, v, kv = parts
        rows = slice(n * L, (n + 1) * L)
        intra = jnp.dot(scores, v, preferred_element_type=F32)
        inter = jnp.dot(qb, r_prev.astype(BF16), preferred_element_type=F32) * xi
        o = intra + inter
        o = o * lax.rsqrt(jnp.mean(o * o, axis=-1, keepdims=True) + EPS)
        g = g_ref[rows, :].astype(F32)
        o_ref[rows, :] = (o * (g * jax.nn.sigmoid(g))).astype(o_ref.dtype)
        return r_prev * gamma_l + kv

    n_chunks = S // L
    r_prev = jnp.zeros((RET_HEAD_DIM, RET_HEAD_DIM), F32)
    pending = [state_free(n) for n in range(min(RET_SKEW, n_chunks))]
    for n in range(n_chunks):
        if n + RET_SKEW < n_chunks:
            pending.append(state_free(n + RET_SKEW))
        r_prev = finish(n, pending.pop(0), r_prev)


def _retention(proj, cos_t, sin_t, decay, xi_b, zeta_b, B, S):
    H, dh, L = RET_HEADS, RET_HEAD_DIM, RET_CHUNK

    def pspec(base):
        return pl.BlockSpec((None, S, LANES), lambda b, h, base=base: (base + h, b, 0))

    return pl.pallas_call(
        _retention_kernel,
        out_shape=jax.ShapeDtypeStruct((B * S, RET_W), BF16),
        grid=(B, H),
        in_specs=[pspec(_BLK_RQ), pspec(_BLK_RK), pspec(_BLK_RV), pspec(_BLK_RG),
                  pl.BlockSpec((S, dh), lambda b, h: (0, 0)),
                  pl.BlockSpec((S, dh), lambda b, h: (0, 0)),
                  pl.BlockSpec((None, L, L), lambda b, h: (h, 0, 0)),
                  pl.BlockSpec((None, L, dh), lambda b, h: (h, 0, 0)),
                  pl.BlockSpec((None, L, dh), lambda b, h: (h, 0, 0))],
        out_specs=pl.BlockSpec((S, dh), lambda b, h: (b, h)),
        compiler_params=_cparams(("parallel", "parallel")),
        name="retention",
    )(proj, proj, proj, proj, cos_t, sin_t, decay, xi_b, zeta_b)


def _mlstm_kernel(q_ref, k_ref, v_ref, og_ref, gp_ref, gb_ref, cwq_ref, cwk_ref, cbq_ref, cbk_ref,
                  nw_ref, o_ref, pad_ref, qc_ref, kc_ref, a_ref, bc_ref, at_ref, bt_ref):
    L = MLSTM_CHUNK
    S = q_ref.shape[0]
    NH = MLSTM_HEADS
    KC = MLSTM_CONV
    h = pl.program_id(1)
    n_chunks = S // L

    def conv_silu(src_ref, cw_ref, cb_ref, dst_ref, scale):
        pad_ref[0:SUBLANES, :] = jnp.zeros((SUBLANES, LANES), F32)
        pad_ref[SUBLANES:SUBLANES + S, :] = src_ref[...].astype(F32)
        cw = cw_ref[...]
        cb = cb_ref[...]
        for c in range(n_chunks):
            acc = cb
            for t in range(KC):
                r0 = SUBLANES + c * L - (KC - 1) + t
                acc = acc + pad_ref[r0:r0 + L, :] * cw[t:t + 1, :]
            y = acc * jax.nn.sigmoid(acc)
            dst_ref[c * L:(c + 1) * L, :] = y * scale

    conv_silu(q_ref, cwq_ref, cbq_ref, qc_ref, MLSTM_QK_DIM ** -0.5)
    conv_silu(k_ref, cwk_ref, cbk_ref, kc_ref, 1.0)

    lane = lax.broadcasted_iota(jnp.int32, (L, LANES), 1)
    sub = lax.broadcasted_iota(jnp.int32, (L, LANES), 0)
    sub_col = lax.broadcasted_iota(jnp.int32, (L, 1), 0)
    causal = sub >= lane
    nw = nw_ref[...]

    @pl.when(h == 0)
    def _():
        tri = (sub >= lane).astype(F32)
        is_f = (lane >= NH) & (lane < 2 * NH)
        for n in range(n_chunks):
            rows = slice(n * L, (n + 1) * L)
            gp = gp_ref[rows, :] + gb_ref[...]
            logsig = jnp.minimum(gp, 0.0) - jnp.log1p(jnp.exp(-jnp.abs(gp)))
            a = jnp.where(is_f, logsig, gp)
            bc = jnp.dot(tri, a, preferred_element_type=F32, precision=lax.Precision.HIGHEST)
            a_ref[rows, :] = a
            bc_ref[rows, :] = bc
            at_ref[n * SUBLANES:(n + 1) * SUBLANES, :] = a.T[0:SUBLANES, :]
            bt_ref[n * SUBLANES:(n + 1) * SUBLANES, :] = bc.T[0:SUBLANES, :]

    def state_free(n):
        rows = slice(n * L, (n + 1) * L)
        li_col = jnp.sum(jnp.where(lane == h, a_ref[rows, :], 0.0), axis=1, keepdims=True)
        b_col = jnp.sum(jnp.where(lane == h + NH, bc_ref[rows, :], 0.0), axis=1, keepdims=True)
        li_row = at_ref[pl.ds(n * SUBLANES + h, 1), :]
        b_row = bt_ref[pl.ds(n * SUBLANES + NH + h, 1), :]
        gtot = jnp.max(jnp.where(sub_col == L - 1, b_col, NEG_BIG), axis=0, keepdims=True)
        q = qc_ref[rows, :]
        k = kc_ref[rows, :]
        v = jnp.concatenate([v_ref[0, rows, :], v_ref[1, rows, :]], axis=1)
        qb = q.astype(BF16)
        dlog = jnp.where(causal, b_col - b_row + li_row, NEG_BIG)
        dmax = jnp.max(dlog, axis=1, keepdims=True)
        qk = lax.dot_general(qb, k.astype(BF16), (((1,), (1,)), ((), ())),
                             preferred_element_type=F32)
        return dict(li_col=li_col, b_col=b_col, gtot=gtot, q=q, k=k, qb=qb, v=v, dlog=dlog,
                    dmax=dmax, qk=qk)

    def finish(n, p, state):
        c_prev, n_prev, m_prev = state
        rows = slice(n * L, (n + 1) * L)
        inter_log = p["b_col"] + m_prev
        m_pos = jnp.maximum(inter_log, p["dmax"])
        s_intra = p["qk"] * jnp.exp(p["dlog"] - m_pos)
        inter_scale = jnp.exp(inter_log - m_pos)
        num = (jnp.dot(s_intra.astype(BF16), p["v"], preferred_element_type=F32)
               + inter_scale * jnp.dot(p["qb"], c_prev.astype(BF16), preferred_element_type=F32))
        den = (jnp.sum(s_intra, axis=1, keepdims=True)
               + inter_scale * jnp.sum(p["q"] * n_prev, axis=1, keepdims=True))
        inv = 1.0 / jnp.maximum(jnp.abs(den), jnp.exp(-m_pos))
        ms = jnp.mean(num * num, axis=-1, keepdims=True)
        row_scale = inv * lax.rsqrt(inv * inv * ms + EPS)
        og = jnp.concatenate([og_ref[0, rows, :], og_ref[1, rows, :]], axis=1).astype(F32)
        o_ref[rows, :] = (num * row_scale * nw * jax.nn.sigmoid(og)).astype(o_ref.dtype)
        w_loc = p["gtot"] - p["b_col"] + p["li_col"]
        m_loc = jnp.max(w_loc, axis=0, keepdims=True)
        ke = p["k"] * jnp.exp(w_loc - m_loc)
        c_loc = jnp.dot(ke.T.astype(BF16), p["v"], preferred_element_type=F32)
        n_loc = jnp.sum(ke, axis=0, keepdims=True)
        m_new = jnp.maximum(p["gtot"] + m_prev, m_loc)
        a_dec = jnp.exp(p["gtot"] + m_prev - m_new)
        b_inc = jnp.exp(m_loc - m_new)
        return (a_dec * c_prev + b_inc * c_loc, a_dec * n_prev + b_inc * n_loc, m_new)

    state = (jnp.zeros((MLSTM_QK_DIM, MLSTM_V_DIM), F32), jnp.zeros((1, MLSTM_QK_DIM), F32),
             jnp.zeros((1, 1), F32))
    for n in range(n_chunks):
        state = finish(n, state_free(n), state)


def _mlstm(proj, gpre, gate_b_row, conv_w, conv_b, norm_w, B, S):
    NH, dk, dv = MLSTM_HEADS, MLSTM_QK_DIM, MLSTM_V_DIM
    KC = MLSTM_CONV
    nvb = dv // LANES

    def pspec(base):
        return pl.BlockSpec((None, S, LANES), lambda b, h, base=base: (base + h, b, 0))

    def pspec2(base):
        return pl.BlockSpec((nvb, S, LANES), lambda b, h, base=base: (base // nvb + h, b, 0))

    return pl.pallas_call(
        _mlstm_kernel,
        out_shape=jax.ShapeDtypeStruct((B * S, MLSTM_V_W), BF16),
        grid=(B, NH),
        in_specs=[pspec(_BLK_MQ), pspec(_BLK_MK), pspec2(_BLK_MV), pspec2(_BLK_MO),
                  pl.BlockSpec((S, LANES), lambda b, h: (b, 0)),
                  pl.BlockSpec((1, LANES), lambda b, h: (0, 0)),
                  pl.BlockSpec((KC, dk), lambda b, h: (0, h)),
                  pl.BlockSpec((KC, dk), lambda b, h: (0, NH + h)),
                  pl.BlockSpec((1, dk), lambda b, h: (0, h)),
                  pl.BlockSpec((1, dk), lambda b, h: (0, NH + h)),
                  pl.BlockSpec((1, dv), lambda b, h: (0, h))],
        out_specs=pl.BlockSpec((S, dv), lambda b, h: (b, h)),
        scratch_shapes=[pltpu.VMEM((S + SUBLANES, dk), F32),
                        pltpu.VMEM((S, dk), F32),
                        pltpu.VMEM((S, dk), F32),
                        pltpu.VMEM((S, LANES), F32),
                        pltpu.VMEM((S, LANES), F32),
                        pltpu.VMEM((S // MLSTM_CHUNK * SUBLANES, MLSTM_CHUNK), F32),
                        pltpu.VMEM((S // MLSTM_CHUNK * SUBLANES, MLSTM_CHUNK), F32)],
        compiler_params=_cparams(("parallel", "arbitrary")),
        name="mlstm",
    )(proj, proj, proj, proj, gpre, gate_b_row, conv_w, conv_w, conv_b, conv_b, norm_w)


def _moba_kernel(q_ref, k_ref, v_ref, bvec_ref, far_ref, o_ref, kmean_ref, gate_ref, lg_ref,
                 bias_ref):
    BS = MOBA_BLOCK
    S = k_ref.shape[0]
    NB = S // BS
    nt = (((1,), (1,)), ((), ()))

    toep = pltpu.roll(jnp.broadcast_to(bvec_ref[...], (BS, 2 * BS)), 0, 1, stride=1, stride_axis=0)
    bias_ref[0] = toep[:, 0:BS]
    bias_ref[1] = toep[:, BS:2 * BS]

    for j in range(NB):
        kj = k_ref[j * BS:(j + 1) * BS, :].astype(F32)
        kmean_ref[j:j + 1, :] = jnp.mean(kj, axis=0, keepdims=True)
    ones_rows = jnp.ones((2 * SUBLANES, BS), BF16)
    vts = [jnp.concatenate([v_ref[j * BS:(j + 1) * BS, :].astype(F32).T.astype(BF16), ones_rows],
                           axis=0) for j in range(NB)]
    dh = MOBA_HEAD_DIM
    far = far_ref[...]
    kl = lax.broadcasted_iota(jnp.int32, (BS, BS), 0)
    ql = lax.broadcasted_iota(jnp.int32, (BS, BS), 1)
    causal = kl <= ql

    def select_rows(i, q):
        if i <= MOBA_TOPK:
            return [None] * i
        gate_ref[i % 2] = lax.dot_general(kmean_ref[...], q.astype(F32), nt,
                                          preferred_element_type=F32,
                                          precision=lax.Precision.HIGHEST)
        rows = [gate_ref[i % 2, n:n + 1, :] for n in range(i)]
        add_rows = []
        for n in range(i):
            rank = jnp.zeros_like(rows[n])
            for mth in range(i):
                if mth == n:
                    continue
                ahead = rows[mth] > rows[n]
                if mth < n:
                    ahead = ahead | (rows[mth] == rows[n])
                rank = rank + jnp.where(ahead, 1.0, 0.0)
            add_rows.append(jnp.where(rank < MOBA_TOPK, 0.0, NEG_BIG))
        return add_rows

    def logits_tile(i, j, q, add_rows):
        lg = lax.dot_general(k_ref[j * BS:(j + 1) * BS, :], q, nt, preferred_element_type=F32)
        if j == i:
            lg = jnp.where(causal, lg + bias_ref[0], NEG_BIG)
        elif j == i - 1:
            lg = lg + bias_ref[1]
            if add_rows[j] is not None:
                lg = lg + add_rows[j]
        else:
            lg = lg + (far if add_rows[j] is None else far + add_rows[j])
        lg_ref[i % 2, j] = lg
        return jnp.max(lg, axis=0, keepdims=True)

    def value_tile(i, j, m):
        p = jnp.exp2(lg_ref[i % 2, j] - m)
        return jnp.dot(vts[j], p.astype(BF16), preferred_element_type=F32)

    def merge(a, b, op):
        return b if a is None else op(a, b)

    m_cur = logits_tile(0, 0, q_ref[0:BS, :], [])
    for i in range(NB):
        m_next = None
        if i + 1 < NB:
            q_next = q_ref[(i + 1) * BS:(i + 2) * BS, :]
            rows_next = select_rows(i + 1, q_next)
        acc = None
        for j in range(i + 2):
            if i + 1 < NB:
                m_next = merge(m_next, logits_tile(i + 1, j, q_next, rows_next), jnp.maximum)
            if j <= i:
                acc = merge(acc, value_tile(i, j, m_cur), jnp.add)
        l = jnp.max(acc[dh:dh + SUBLANES, :], axis=0, keepdims=True)
        out_t = acc[0:dh, :] * (1.0 / l)
        o_ref[i * BS:(i + 1) * BS, :] = out_t.T.astype(o_ref.dtype)
        m_cur = m_next


def _moba(proj, bias_tiles, far_rows, B, S):
    H, dh, BS = MOBA_HEADS, MOBA_HEAD_DIM, MOBA_BLOCK
    NB = S // BS

    def pspec(base):
        return pl.BlockSpec((None, S, LANES), lambda b, h, base=base: (base + h, b, 0))

    return pl.pallas_call(
        _moba_kernel,
        out_shape=jax.ShapeDtypeStruct((B * S, MOBA_W), BF16),
        grid=(B, H),
        in_specs=[pspec(_BLK_BQ), pspec(_BLK_BK), pspec(_BLK_BV),
                  pl.BlockSpec((None, 1, 2 * BS), lambda b, h: (h, 0, 0)),
                  pl.BlockSpec((None, 1, BS), lambda b, h: (h, 0, 0))],
        out_specs=pl.BlockSpec((S, dh), lambda b, h: (b, h)),
        scratch_shapes=[pltpu.VMEM((NB, dh), F32),
                        pltpu.VMEM((2, NB, BS), F32),
                        pltpu.VMEM((2, NB, BS, BS), F32),
                        pltpu.VMEM((2, BS, BS), F32)],
        compiler_params=_cparams(("parallel", "parallel")),
        name="moba",
    )(proj, proj, proj, bias_tiles, far_rows)


def _merge_kernel(yr_ref, ym_ref, yb_ref, gr_ref, gm_ref, gb_ref, x_ref,
                  wr_ref, wm_ref, wb_ref, wo_ref, o_ref):
    mixed = gr_ref[...].astype(F32) * jnp.dot(yr_ref[...], wr_ref[...], preferred_element_type=F32)
    mixed = mixed + gm_ref[...].astype(F32) * jnp.dot(ym_ref[...], wm_ref[...],
                                                      preferred_element_type=F32)
    mixed = mixed + gb_ref[...].astype(F32) * jnp.dot(yb_ref[...], wb_ref[...],
                                                      preferred_element_type=F32)
    o_ref[...] = x_ref[...] + jnp.dot(mixed.astype(BF16), wo_ref[...], preferred_element_type=F32)


def _merge(y_ret, y_ml, y_mb, sgates, x2, w_r, w_m, w_b, w_o, *, tm):
    T, D = x2.shape

    def resident(shape):
        return pl.BlockSpec(shape, lambda i: (0, 0), pipeline_mode=pl.Buffered(1))

    return pl.pallas_call(
        _merge_kernel,
        out_shape=jax.ShapeDtypeStruct((T, D), F32),
        grid=(T // tm,),
        in_specs=[pl.BlockSpec((tm, RET_W), lambda i: (i, 0)),
                  pl.BlockSpec((tm, MLSTM_V_W), lambda i: (i, 0)),
                  pl.BlockSpec((tm, MOBA_W), lambda i: (i, 0)),
                  pl.BlockSpec((tm, D), lambda i: (i, 0)),
                  pl.BlockSpec((tm, D), lambda i: (i, 1)),
                  pl.BlockSpec((tm, D), lambda i: (i, 2)),
                  pl.BlockSpec((tm, D), lambda i: (i, 0)),
                  resident((RET_W, D)), resident((MLSTM_V_W, D)), resident((MOBA_W, D)),
                  resident((D, D))],
        out_specs=pl.BlockSpec((tm, D), lambda i: (i, 0)),
        compiler_params=_cparams(("parallel",)),
        name="merge",
    )(y_ret, y_ml, y_mb, sgates, sgates, sgates, x2, w_r, w_m, w_b, w_o)


def _ffn_kernel(x_ref, nw_ref, fw_ref, w1_ref, w2_ref, o_ref, hn_ref, acc_ref, *,
                row_chunk, final_norm):
    j = pl.program_id(1)

    @pl.when(j == 0)
    def _():
        _norm_rows_to(x_ref, nw_ref, hn_ref, row_chunk)
        acc_ref[...] = jnp.zeros_like(acc_ref)

    u = jnp.dot(hn_ref[...], w1_ref[...], preferred_element_type=F32)
    u = jnp.maximum(u, 0.0)
    acc_ref[...] += jnp.dot((u * u).astype(BF16), w2_ref[...], preferred_element_type=F32)

    @pl.when(j == pl.num_programs(1) - 1)
    def _():
        def body(c, carry):
            r = pl.multiple_of(c * row_chunk, row_chunk)
            y = x_ref[pl.ds(r, row_chunk), :] + acc_ref[pl.ds(r, row_chunk), :]
            if final_norm:
                y = _rms_rows(y, fw_ref[...])
            o_ref[pl.ds(r, row_chunk), :] = y
            return carry
        lax.fori_loop(0, x_ref.shape[0] // row_chunk, body, 0)


def _ffn(x2, nw, fw, w1, w2, *, tm, tf, final_norm):
    T, D = x2.shape
    F = w1.shape[1]
    return pl.pallas_call(
        functools.partial(_ffn_kernel, row_chunk=128, final_norm=final_norm),
        out_shape=jax.ShapeDtypeStruct((T, D), F32),
        grid=(T // tm, F // tf),
        in_specs=[pl.BlockSpec((tm, D), lambda i, j: (i, 0)),
                  pl.BlockSpec((1, D), lambda i, j: (0, 0)),
                  pl.BlockSpec((1, D), lambda i, j: (0, 0)),
                  pl.BlockSpec((D, tf), lambda i, j: (0, j)),
                  pl.BlockSpec((tf, D), lambda i, j: (j, 0))],
        out_specs=pl.BlockSpec((tm, D), lambda i, j: (i, 0)),
        scratch_shapes=[pltpu.VMEM((tm, D), BF16), pltpu.VMEM((tm, D), F32)],
        compiler_params=_cparams(("parallel", "arbitrary")),
        name="ffn",
    )(x2, nw, fw, w1, w2)


def _rope_tables(S):
    half = RET_HEAD_DIM // 2
    inv = ROPE_BASE ** (-jnp.arange(half, dtype=F32) / half)
    ang = jnp.arange(S).astype(F32)[:, None] * inv[None, :]
    cos, sin = jnp.cos(ang), jnp.sin(ang)
    return jnp.concatenate([cos, cos], axis=-1), jnp.concatenate([-sin, sin], axis=-1)


def _retention_tables():
    H, L, dh = RET_HEADS, RET_CHUNK, RET_HEAD_DIM
    log_gamma = jnp.log1p(-jnp.exp2(-5.0 - jnp.arange(H, dtype=F32)))
    idx = jnp.arange(L, dtype=F32)
    diff = idx[:, None] - idx[None, :]
    decay = jnp.where(diff >= 0, jnp.exp(jnp.maximum(diff, 0.0) * log_gamma[:, None, None]), 0.0)
    zeta = jnp.exp((L - 1 - idx)[None, :] * log_gamma[:, None])
    xi = jnp.exp((idx + 1.0)[None, :] * log_gamma[:, None])
    return (decay, jnp.broadcast_to(xi[:, :, None], (H, L, dh)),
            jnp.broadcast_to(zeta[:, :, None], (H, L, dh)))


def _t5_bucket(dist):
    n = jnp.maximum(dist, 0)
    exact = REL_BUCKETS // 2
    nf = jnp.maximum(n, 1).astype(F32)
    large = exact + (jnp.log(nf / exact) / math.log(REL_MAX_DIST / exact)
                     * (REL_BUCKETS - exact)).astype(jnp.int32)
    large = jnp.minimum(large, REL_BUCKETS - 1)
    return jnp.where(n < exact, n, large)


def _moba_bias_tables(rel_bias, S):
    BS = MOBA_BLOCK
    assert BS + 1 >= REL_MAX_DIST and S % BS == 0
    table_t = rel_bias.T.astype(F32) * LOG2E
    onehot = (_t5_bucket(jnp.arange(2 * BS))[:, None] == jnp.arange(REL_BUCKETS)).astype(F32)
    by_dist = jnp.einsum("db,hb->hd", onehot, table_t, precision=lax.Precision.HIGHEST)
    far = table_t[:, REL_BUCKETS - 1]
    far_rows = jnp.broadcast_to(far[:, None, None], (far.shape[0], 1, BS))
    return by_dist[:, None, :], far_rows


def kernel(x, w_in, mlstm_gate_b, mlstm_conv_w, mlstm_conv_b, mlstm_norm_w, w_branch_ret,
           w_branch_mlstm, w_branch_moba, w_out, norm_mix_w, norm_mlp_w, w_ff1, w_ff2, rel_bias,
           final_norm_w):
    B, S, D = x.shape
    T = B * S
    depth = w_in.shape[0]
    x2 = x.reshape(T, D)

    cos_t, sin_t = _rope_tables(S)
    decay, xi_b, zeta_b = _retention_tables()
    bias_by_dist, far_rows = _moba_bias_tables(rel_bias, S)
    final_w = final_norm_w.reshape(1, D)

    for l in range(depth):
        wl = w_in[l]
        w_rq = wl[:, _OFF_RQ:_OFF_RQ + RET_W] * (RET_HEAD_DIM ** -0.5)
        w_bq = wl[:, _OFF_BQ:_OFF_BQ + MOBA_W] * (MOBA_HEAD_DIM ** -0.5 * LOG2E)
        w_main = jnp.concatenate([w_rq, wl[:, RET_W:_OFF_MI], w_bq,
                                  wl[:, _OFF_BQ + MOBA_W:_OFF_GATES]], axis=1).astype(BF16)
        w_gpre = jnp.pad(wl[:, _OFF_MI:_OFF_BQ], ((0, 0), (0, LANES - 2 * MLSTM_HEADS))).astype(BF16)
        w_gates = wl[:, _OFF_GATES:].astype(BF16)

        proj, hn, gpre = _in_proj(x2, norm_mix_w[l].reshape(1, D), w_main, w_gpre,
                                  **_TILES["in_proj"])
        sgates = _matmul(hn, w_gates, act="sigmoid", out_dtype=BF16, name="branch_gates",
                         **_TILES["gates"])

        y_ret = _retention(proj, cos_t, sin_t, decay, xi_b, zeta_b, B, S)
        gate_b_row = jnp.pad(mlstm_gate_b[l].reshape(1, 2 * MLSTM_HEADS),
                             ((0, 0), (0, LANES - 2 * MLSTM_HEADS)))
        y_ml = _mlstm(proj, gpre, gate_b_row, mlstm_conv_w[l], mlstm_conv_b[l].reshape(1, -1),
                      mlstm_norm_w[l].reshape(1, -1), B, S)
        y_mb = _moba(proj, bias_by_dist, far_rows, B, S)

        x2 = _merge(y_ret, y_ml, y_mb, sgates, x2,
                    w_branch_ret[l].astype(BF16), w_branch_mlstm[l].astype(BF16),
                    w_branch_moba[l].astype(BF16), w_out[l].astype(BF16), **_TILES["merge"])
        x2 = _ffn(x2, norm_mlp_w[l].reshape(1, D), final_w, w_ff1[l].astype(BF16),
                  w_ff2[l].astype(BF16), final_norm=(l == depth - 1), **_TILES["ffn"])

    return x2.reshape(B, S, D)
```

```python
import functools
import math

import numpy as np
import jax
import jax.numpy as jnp
from jax import lax
from jax.experimental import pallas as pl
from jax.experimental.pallas import tpu as pltpu

D_MODEL = 2048
RET_HEADS = 8
RET_HEAD_DIM = 128
RET_CHUNK = 128
MLSTM_HEADS = 4
MLSTM_QK_DIM = 128
MLSTM_V_DIM = 256
MLSTM_CHUNK = 128
MLSTM_CONV = 4
MOBA_HEADS = 8
MOBA_HEAD_DIM = 128
MOBA_BLOCK = 256
MOBA_TOPK = 3
REL_BUCKETS = 32
REL_MAX_DIST = 128
D_FF = 4 * D_MODEL
ROPE_BASE = 10000.0
EPS = 1e-6

RET_W = RET_HEADS * RET_HEAD_DIM
MLSTM_QK_W = MLSTM_HEADS * MLSTM_QK_DIM
MLSTM_V_W = MLSTM_HEADS * MLSTM_V_DIM
MOBA_W = MOBA_HEADS * MOBA_HEAD_DIM

LANES = 128
SUBLANES = 8
VMEM_LIMIT_BYTES = 56 * 1024 * 1024

_OFF_RQ = 0
_OFF_MI = 4 * RET_W + 2 * MLSTM_QK_W + 2 * MLSTM_V_W
_OFF_BQ = _OFF_MI + 2 * MLSTM_HEADS
_OFF_GATES = _OFF_BQ + 3 * MOBA_W
_N_HEADMAJOR = _OFF_MI + 3 * MOBA_W
_BLK_RQ, _BLK_RK, _BLK_RV, _BLK_RG = 0, 8, 16, 24
_BLK_MQ, _BLK_MK, _BLK_MV, _BLK_MO = 32, 36, 40, 48
_BLK_BQ, _BLK_BK, _BLK_BV = 56, 64, 72

RET_SKEW = 1
NEG_BIG = -1e30
LOG2E = 1.0 / math.log(2.0)

_TILES = {
    "in_proj": dict(tm=1024, tn=1024),
    "gates": dict(tm=1024, tn=1024),
    "merge": dict(tm=256),
    "ffn": dict(tm=512, tf=1024),
    "pack": dict(tr=1024),
    "cast": dict(tr=1024, tc=2048),
}

BF16 = jnp.bfloat16
F32 = jnp.float32


def _cparams(sem):
    return pltpu.CompilerParams(dimension_semantics=sem, vmem_limit_bytes=VMEM_LIMIT_BYTES)


def _rms_rows(xf, w_row):
    ms = jnp.mean(xf * xf, axis=-1, keepdims=True)
    return xf * lax.rsqrt(ms + EPS) * w_row


def _norm_rows_to(x_ref, nw_ref, hn_ref, row_chunk):
    def body(c, carry):
        r = pl.multiple_of(c * row_chunk, row_chunk)
        hn_ref[pl.ds(r, row_chunk), :] = _rms_rows(x_ref[pl.ds(r, row_chunk), :],
                                                   nw_ref[...]).astype(BF16)
        return carry
    lax.fori_loop(0, x_ref.shape[0] // row_chunk, body, 0)


def _in_proj_kernel(x_ref, nw_ref, w_ref, ws_ref, o_ref, hn_ref, os_ref, *, row_chunk):
    @pl.when(pl.program_id(1) == 0)
    def _():
        _norm_rows_to(x_ref, nw_ref, hn_ref, row_chunk)
        os_ref[...] = jnp.dot(hn_ref[...], ws_ref[...], preferred_element_type=F32)

    acc = jnp.dot(hn_ref[...], w_ref[...], preferred_element_type=F32)
    for c in range(o_ref.shape[0]):
        o_ref[c] = acc[:, c * LANES:(c + 1) * LANES].astype(o_ref.dtype)


def _in_proj(x2, nw, w_cat, layer, w_side, *, tm, tn):
    T, D = x2.shape
    N = _N_HEADMAJOR
    NS = w_side.shape[1]
    return pl.pallas_call(
        functools.partial(_in_proj_kernel, row_chunk=128),
        out_shape=(jax.ShapeDtypeStruct((N // LANES, T, LANES), BF16),
                   jax.ShapeDtypeStruct((T, D), BF16),
                   jax.ShapeDtypeStruct((T, NS), F32)),
        grid=(T // tm, N // tn),
        in_specs=[pl.BlockSpec((tm, D), lambda i, j: (i, 0)),
                  pl.BlockSpec((1, D), lambda i, j: (0, 0)),
                  pl.BlockSpec((None, D, tn), lambda i, j: (layer, 0, j)),
                  pl.BlockSpec((D, NS), lambda i, j: (0, 0))],
        out_specs=(pl.BlockSpec((tn // LANES, tm, LANES), lambda i, j: (j, i, 0)),
                   pl.BlockSpec((tm, D), lambda i, j: (i, 0)),
                   pl.BlockSpec((tm, NS), lambda i, j: (i, 0))),
        compiler_params=_cparams(("parallel", "arbitrary")),
        name="in_proj",
    )(x2, nw, w_cat, w_side)


def _matmul_kernel(h_ref, w_ref, o_ref, *, act):
    acc = jnp.dot(h_ref[...], w_ref[...], preferred_element_type=F32)
    if act == "sigmoid":
        acc = jax.nn.sigmoid(acc)
    o_ref[...] = acc.astype(o_ref.dtype)


def _matmul(hn, w_cat, layer, col0, n_cols, *, tm, tn, act, out_dtype, name):
    T, D = hn.shape
    assert col0 % tn == 0 and n_cols % tn == 0
    return pl.pallas_call(
        functools.partial(_matmul_kernel, act=act),
        out_shape=jax.ShapeDtypeStruct((T, n_cols), out_dtype),
        grid=(T // tm, n_cols // tn),
        in_specs=[pl.BlockSpec((tm, D), lambda i, j: (i, 0)),
                  pl.BlockSpec((None, D, tn), lambda i, j: (layer, 0, col0 // tn + j))],
        out_specs=pl.BlockSpec((tm, tn), lambda i, j: (i, j)),
        compiler_params=_cparams(("parallel", "parallel")),
        name=name,
    )(hn, w_cat)


def _cast_kernel(w_ref, o_ref, *, row_chunk):
    def body(c, carry):
        r = pl.multiple_of(c * row_chunk, row_chunk)
        o_ref[pl.ds(r, row_chunk), :] = w_ref[pl.ds(r, row_chunk), :].astype(BF16)
        return carry
    lax.fori_loop(0, w_ref.shape[0] // row_chunk, body, 0)


def _cast_bf16(w3, *, tr, tc):
    Lw, R, C = w3.shape
    return pl.pallas_call(
        functools.partial(_cast_kernel, row_chunk=128),
        out_shape=jax.ShapeDtypeStruct((Lw, R, C), BF16),
        grid=(Lw, R // tr, C // tc),
        in_specs=[pl.BlockSpec((None, tr, tc), lambda l, i, j: (l, i, j))],
        out_specs=pl.BlockSpec((None, tr, tc), lambda l, i, j: (l, i, j)),
        compiler_params=_cparams(("parallel", "parallel", "parallel")),
        name="cast_bf16",
    )(w3)


_PACK_TC = 1024
_PACK_SHIFT = 2 * MLSTM_HEADS
_PACK_ALIGNED_TILES = _OFF_MI // _PACK_TC


def _pack_w_in_kernel(a_ref, b_ref, s_ref, o_ref, *, row_chunk):
    j = pl.program_id(2)
    tc = a_ref.shape[1]
    n = a_ref.shape[0] // row_chunk

    @pl.when(j < _PACK_ALIGNED_TILES)
    def _():
        def body(c, carry):
            r = pl.multiple_of(c * row_chunk, row_chunk)
            o_ref[pl.ds(r, row_chunk), :] = (a_ref[pl.ds(r, row_chunk), :] * s_ref[...]).astype(BF16)
            return carry
        lax.fori_loop(0, n, body, 0)

    @pl.when(j >= _PACK_ALIGNED_TILES)
    def _():
        def body(c, carry):
            r = pl.multiple_of(c * row_chunk, row_chunk)
            cat = jnp.concatenate([a_ref[pl.ds(r, row_chunk), :], b_ref[pl.ds(r, row_chunk), :]],
                                  axis=1)
            shifted = cat[:, _PACK_SHIFT:_PACK_SHIFT + tc]
            o_ref[pl.ds(r, row_chunk), :] = (shifted * s_ref[...]).astype(BF16)
            return carry
        lax.fori_loop(0, n, body, 0)


def _pack_w_in(w_in, scale_row, *, tr):
    depth, D, d_in = w_in.shape
    tc = _PACK_TC
    n_out = d_in - _PACK_SHIFT
    assert _OFF_MI % tc == 0 and n_out % tc == 0 and tc % LANES == 0
    return pl.pallas_call(
        functools.partial(_pack_w_in_kernel, row_chunk=128),
        out_shape=jax.ShapeDtypeStruct((depth, D, n_out), BF16),
        grid=(depth, D // tr, n_out // tc),
        in_specs=[pl.BlockSpec((None, tr, tc), lambda l, i, j: (l, i, j)),
                  pl.BlockSpec((None, tr, LANES), lambda l, i, j: (l, i, (j + 1) * (tc // LANES))),
                  pl.BlockSpec((1, tc), lambda l, i, j: (0, j))],
        out_specs=pl.BlockSpec((None, tr, tc), lambda l, i, j: (l, i, j)),
        compiler_params=_cparams(("parallel", "parallel", "arbitrary")),
        name="pack_w_in",
    )(w_in, w_in, scale_row)


def _retention_kernel(q_ref, k_ref, v_ref, g_ref, cos_ref, sin_ref, decay_ref, xi_ref, zeta_ref,
                      o_ref):
    L = RET_CHUNK
    S = q_ref.shape[0]
    decay = decay_ref[...]
    xi = xi_ref[...]
    zeta = zeta_ref[...]
    gamma_l = xi[L - 1:L, :]

    def rope(x, cos, sin_signed):
        return x * cos + pltpu.roll(x, RET_HEAD_DIM // 2, 1) * sin_signed

    def state_free(n):
        rows = slice(n * L, (n + 1) * L)
        cos = cos_ref[rows, :]
        sin = sin_ref[rows, :]
        q = rope(q_ref[rows, :].astype(F32), cos, sin)
        k = rope(k_ref[rows, :].astype(F32), cos, sin)
        v = v_ref[rows, :]
        qb = q.astype(BF16)
        scores = lax.dot_general(qb, k.astype(BF16), (((1,), (1,)), ((), ())),
                                 preferred_element_type=F32) * decay
        kv = jnp.dot((k * zeta).T.astype(BF16), v, preferred_element_type=F32)
        return qb, scores.astype(BF16), v, kv

    def finish(n, parts, r_prev):
        qb, scores, v, kv = parts
        rows = slice(n * L, (n + 1) * L)
        intra = jnp.dot(scores, v, preferred_element_type=F32)
        inter = jnp.dot(qb, r_prev.astype(BF16), preferred_element_type=F32) * xi
        o = intra + inter
        o = o * lax.rsqrt(jnp.mean(o * o, axis=-1, keepdims=True) + EPS)
        g = g_ref[rows, :].astype(F32)
        o_ref[rows, :] = (o * (g * jax.nn.sigmoid(g))).astype(o_ref.dtype)
        return r_prev * gamma_l + kv

    n_chunks = S // L
    r_prev = jnp.zeros((RET_HEAD_DIM, RET_HEAD_DIM), F32)
    pending = [state_free(n) for n in range(min(RET_SKEW, n_chunks))]
    for n in range(n_chunks):
        if n + RET_SKEW < n_chunks:
            pending.append(state_free(n + RET_SKEW))
        r_prev = finish(n, pending.pop(0), r_prev)


def _retention(proj, cos_t, sin_t, decay, xi_b, zeta_b, B, S):
    H, dh, L = RET_HEADS, RET_HEAD_DIM, RET_CHUNK

    def pspec(base):
        return pl.BlockSpec((None, S, LANES), lambda b, h, base=base: (base + h, b, 0))

    return pl.pallas_call(
        _retention_kernel,
        out_shape=jax.ShapeDtypeStruct((B * S, RET_W), BF16),
        grid=(B, H),
        in_specs=[pspec(_BLK_RQ), pspec(_BLK_RK), pspec(_BLK_RV), pspec(_BLK_RG),
                  pl.BlockSpec((S, dh), lambda b, h: (0, 0)),
                  pl.BlockSpec((S, dh), lambda b, h: (0, 0)),
                  pl.BlockSpec((None, L, L), lambda b, h: (h, 0, 0)),
                  pl.BlockSpec((None, L, dh), lambda b, h: (h, 0, 0)),
                  pl.BlockSpec((None, L, dh), lambda b, h: (h, 0, 0))],
        out_specs=pl.BlockSpec((S, dh), lambda b, h: (b, h)),
        compiler_params=_cparams(("parallel", "parallel")),
        name="retention",
    )(proj, proj, proj, proj, cos_t, sin_t, decay, xi_b, zeta_b)


def _mlstm_kernel(q_ref, k_ref, v_ref, og_ref, gp_ref, gb_ref, cwq_ref, cwk_ref, cbq_ref, cbk_ref,
                  nw_ref, o_ref, pad_ref, qc_ref, kc_ref, a_ref, bc_ref, at_ref, bt_ref):
    L = MLSTM_CHUNK
    S = q_ref.shape[0]
    NH = MLSTM_HEADS
    KC = MLSTM_CONV
    h = pl.program_id(1)
    n_chunks = S // L

    def conv_silu(src_ref, cw_ref, cb_ref, dst_ref, scale):
        pad_ref[0:SUBLANES, :] = jnp.zeros((SUBLANES, LANES), F32)
        pad_ref[SUBLANES:SUBLANES + S, :] = src_ref[...].astype(F32)
        cw = cw_ref[...]
        cb = cb_ref[...]
        for c in range(n_chunks):
            acc = cb
            for t in range(KC):
                r0 = SUBLANES + c * L - (KC - 1) + t
                acc = acc + pad_ref[r0:r0 + L, :] * cw[t:t + 1, :]
            y = acc * jax.nn.sigmoid(acc)
            dst_ref[c * L:(c + 1) * L, :] = y * scale

    conv_silu(q_ref, cwq_ref, cbq_ref, qc_ref, MLSTM_QK_DIM ** -0.5)
    conv_silu(k_ref, cwk_ref, cbk_ref, kc_ref, 1.0)

    lane = lax.broadcasted_iota(jnp.int32, (L, LANES), 1)
    sub = lax.broadcasted_iota(jnp.int32, (L, LANES), 0)
    sub_col = lax.broadcasted_iota(jnp.int32, (L, 1), 0)
    causal = sub >= lane
    nw = nw_ref[...]

    @pl.when(h == 0)
    def _():
        tri = (sub >= lane).astype(F32)
        is_f = (lane >= NH) & (lane < 2 * NH)
        for n in range(n_chunks):
            rows = slice(n * L, (n + 1) * L)
            gp = gp_ref[rows, :] + gb_ref[...]
            logsig = jnp.minimum(gp, 0.0) - jnp.log1p(jnp.exp(-jnp.abs(gp)))
            a = jnp.where(is_f, logsig, gp)
            bc = jnp.dot(tri, a, preferred_element_type=F32, precision=lax.Precision.HIGHEST)
            a_ref[rows, :] = a
            bc_ref[rows, :] = bc
            at_ref[n * SUBLANES:(n + 1) * SUBLANES, :] = a.T[0:SUBLANES, :]
            bt_ref[n * SUBLANES:(n + 1) * SUBLANES, :] = bc.T[0:SUBLANES, :]

    def state_free(n):
        rows = slice(n * L, (n + 1) * L)
        li_col = jnp.sum(jnp.where(lane == h, a_ref[rows, :], 0.0), axis=1, keepdims=True)
        b_col = jnp.sum(jnp.where(lane == h + NH, bc_ref[rows, :], 0.0), axis=1, keepdims=True)
        li_row = at_ref[pl.ds(n * SUBLANES + h, 1), :]
        b_row = bt_ref[pl.ds(n * SUBLANES + NH + h, 1), :]
        gtot = jnp.max(jnp.where(sub_col == L - 1, b_col, NEG_BIG), axis=0, keepdims=True)
        q = qc_ref[rows, :]
        k = kc_ref[rows, :]
        v = jnp.concatenate([v_ref[0, rows, :], v_ref[1, rows, :]], axis=1)
        qb = q.astype(BF16)
        dlog = jnp.where(causal, b_col - b_row + li_row, NEG_BIG)
        dmax = jnp.max(dlog, axis=1, keepdims=True)
        qk = lax.dot_general(qb, k.astype(BF16), (((1,), (1,)), ((), ())),
                             preferred_element_type=F32)
        return dict(li_col=li_col, b_col=b_col, gtot=gtot, q=q, k=k, qb=qb, v=v, dlog=dlog,
                    dmax=dmax, qk=qk)

    def finish(n, p, state):
        c_prev, n_prev, m_prev = state
        rows = slice(n * L, (n + 1) * L)
        inter_log = p["b_col"] + m_prev
        m_pos = jnp.maximum(inter_log, p["dmax"])
        s_intra = p["qk"] * jnp.exp(p["dlog"] - m_pos)
        inter_scale = jnp.exp(inter_log - m_pos)
        num = (jnp.dot(s_intra.astype(BF16), p["v"], preferred_element_type=F32)
               + inter_scale * jnp.dot(p["qb"], c_prev.astype(BF16), preferred_element_type=F32))
        den = (jnp.sum(s_intra, axis=1, keepdims=True)
               + inter_scale * jnp.sum(p["q"] * n_prev, axis=1, keepdims=True))
        inv = 1.0 / jnp.maximum(jnp.abs(den), jnp.exp(-m_pos))
        ms = jnp.mean(num * num, axis=-1, keepdims=True)
        row_scale = inv * lax.rsqrt(inv * inv * ms + EPS)
        og = jnp.concatenate([og_ref[0, rows, :], og_ref[1, rows, :]], axis=1).astype(F32)
        o_ref[rows, :] = (num * row_scale * nw * jax.nn.sigmoid(og)).astype(o_ref.dtype)
        w_loc = p["gtot"] - p["b_col"] + p["li_col"]
        m_loc = jnp.max(w_loc, axis=0, keepdims=True)
        ke = p["k"] * jnp.exp(w_loc - m_loc)
        c_loc = jnp.dot(ke.T.astype(BF16), p["v"], preferred_element_type=F32)
        n_loc = jnp.sum(ke, axis=0, keepdims=True)
        m_new = jnp.maximum(p["gtot"] + m_prev, m_loc)
        a_dec = jnp.exp(p["gtot"] + m_prev - m_new)
        b_inc = jnp.exp(m_loc - m_new)
        return (a_dec * c_prev + b_inc * c_loc, a_dec * n_prev + b_inc * n_loc, m_new)

    state = (jnp.zeros((MLSTM_QK_DIM, MLSTM_V_DIM), F32), jnp.zeros((1, MLSTM_QK_DIM), F32),
             jnp.zeros((1, 1), F32))
    for n in range(n_chunks):
        state = finish(n, state_free(n), state)


def _mlstm(proj, gpre, gate_b_row, conv_w, conv_b, norm_w, B, S):
    NH, dk, dv = MLSTM_HEADS, MLSTM_QK_DIM, MLSTM_V_DIM
    KC = MLSTM_CONV
    nvb = dv // LANES

    def pspec(base):
        return pl.BlockSpec((None, S, LANES), lambda b, h, base=base: (base + h, b, 0))

    def pspec2(base):
        return pl.BlockSpec((nvb, S, LANES), lambda b, h, base=base: (base // nvb + h, b, 0))

    return pl.pallas_call(
        _mlstm_kernel,
        out_shape=jax.ShapeDtypeStruct((B * S, MLSTM_V_W), BF16),
        grid=(B, NH),
        in_specs=[pspec(_BLK_MQ), pspec(_BLK_MK), pspec2(_BLK_MV), pspec2(_BLK_MO),
                  pl.BlockSpec((S, LANES), lambda b, h: (b, 0)),
                  pl.BlockSpec((1, LANES), lambda b, h: (0, 0)),
                  pl.BlockSpec((KC, dk), lambda b, h: (0, h)),
                  pl.BlockSpec((KC, dk), lambda b, h: (0, NH + h)),
                  pl.BlockSpec((1, dk), lambda b, h: (0, h)),
                  pl.BlockSpec((1, dk), lambda b, h: (0, NH + h)),
                  pl.BlockSpec((1, dv), lambda b, h: (0, h))],
        out_specs=pl.BlockSpec((S, dv), lambda b, h: (b, h)),
        scratch_shapes=[pltpu.VMEM((S + SUBLANES, dk), F32),
                        pltpu.VMEM((S, dk), F32),
                        pltpu.VMEM((S, dk), F32),
                        pltpu.VMEM((S, LANES), F32),
                        pltpu.VMEM((S, LANES), F32),
                        pltpu.VMEM((S // MLSTM_CHUNK * SUBLANES, MLSTM_CHUNK), F32),
                        pltpu.VMEM((S // MLSTM_CHUNK * SUBLANES, MLSTM_CHUNK), F32)],
        compiler_params=_cparams(("parallel", "arbitrary")),
        name="mlstm",
    )(proj, proj, proj, proj, gpre, gate_b_row, conv_w, conv_w, conv_b, conv_b, norm_w)


def _moba_kernel(q_ref, k_ref, v_ref, bvec_ref, far_ref, o_ref, kmean_ref, gate_ref, lg_ref,
                 bias_ref):
    BS = MOBA_BLOCK
    S = k_ref.shape[0]
    NB = S // BS
    nt = (((1,), (1,)), ((), ()))

    toep = pltpu.roll(jnp.broadcast_to(bvec_ref[...], (BS, 2 * BS)), 0, 1, stride=1, stride_axis=0)
    bias_ref[0] = toep[:, 0:BS]
    bias_ref[1] = toep[:, BS:2 * BS]

    for j in range(NB):
        kj = k_ref[j * BS:(j + 1) * BS, :].astype(F32)
        kmean_ref[j:j + 1, :] = jnp.mean(kj, axis=0, keepdims=True)
    ones_rows = jnp.ones((2 * SUBLANES, BS), BF16)
    vts = [jnp.concatenate([v_ref[j * BS:(j + 1) * BS, :].astype(F32).T.astype(BF16), ones_rows],
                           axis=0) for j in range(NB)]
    dh = MOBA_HEAD_DIM
    far = far_ref[...]
    kl = lax.broadcasted_iota(jnp.int32, (BS, BS), 0)
    ql = lax.broadcasted_iota(jnp.int32, (BS, BS), 1)
    causal = kl <= ql

    def select_rows(i, q):
        if i <= MOBA_TOPK:
            return [None] * i
        gate_ref[i % 2] = lax.dot_general(kmean_ref[...], q.astype(F32), nt,
                                          preferred_element_type=F32,
                                          precision=lax.Precision.HIGHEST)
        rows = [gate_ref[i % 2, n:n + 1, :] for n in range(i)]
        add_rows = []
        for n in range(i):
            rank = jnp.zeros_like(rows[n])
            for mth in range(i):
                if mth == n:
                    continue
                ahead = rows[mth] > rows[n]
                if mth < n:
                    ahead = ahead | (rows[mth] == rows[n])
                rank = rank + jnp.where(ahead, 1.0, 0.0)
            add_rows.append(jnp.where(rank < MOBA_TOPK, 0.0, NEG_BIG))
        return add_rows

    def logits_tile(i, j, q, add_rows):
        lg = lax.dot_general(k_ref[j * BS:(j + 1) * BS, :], q, nt, preferred_element_type=F32)
        if j == i:
            lg = jnp.where(causal, lg + bias_ref[0], NEG_BIG)
        elif j == i - 1:
            lg = lg + bias_ref[1]
            if add_rows[j] is not None:
                lg = lg + add_rows[j]
        else:
            lg = lg + (far if add_rows[j] is None else far + add_rows[j])
        lg_ref[i % 2, j] = lg
        return jnp.max(lg, axis=0, keepdims=True)

    def value_tile(i, j, m):
        p = jnp.exp2(lg_ref[i % 2, j] - m)
        return jnp.dot(vts[j], p.astype(BF16), preferred_element_type=F32)

    def merge(a, b, op):
        return b if a is None else op(a, b)

    m_cur = logits_tile(0, 0, q_ref[0:BS, :], [])
    for i in range(NB):
        m_next = None
        if i + 1 < NB:
            q_next = q_ref[(i + 1) * BS:(i + 2) * BS, :]
            rows_next = select_rows(i + 1, q_next)
        acc = None
        for j in range(i + 2):
            if i + 1 < NB:
                m_next = merge(m_next, logits_tile(i + 1, j, q_next, rows_next), jnp.maximum)
            if j <= i:
                acc = merge(acc, value_tile(i, j, m_cur), jnp.add)
        l = jnp.max(acc[dh:dh + SUBLANES, :], axis=0, keepdims=True)
        out_t = acc[0:dh, :] * (1.0 / l)
        o_ref[i * BS:(i + 1) * BS, :] = out_t.T.astype(o_ref.dtype)
        m_cur = m_next


def _moba(proj, bias_tiles, far_rows, B, S):
    H, dh, BS = MOBA_HEADS, MOBA_HEAD_DIM, MOBA_BLOCK
    NB = S // BS

    def pspec(base):
        return pl.BlockSpec((None, S, LANES), lambda b, h, base=base: (base + h, b, 0))

    return pl.pallas_call(
        _moba_kernel,
        out_shape=jax.ShapeDtypeStruct((B * S, MOBA_W), BF16),
        grid=(B, H),
        in_specs=[pspec(_BLK_BQ), pspec(_BLK_BK), pspec(_BLK_BV),
                  pl.BlockSpec((None, 1, 2 * BS), lambda b, h: (h, 0, 0)),
                  pl.BlockSpec((None, 1, BS), lambda b, h: (h, 0, 0))],
        out_specs=pl.BlockSpec((S, dh), lambda b, h: (b, h)),
        scratch_shapes=[pltpu.VMEM((NB, dh), F32),
                        pltpu.VMEM((2, NB, BS), F32),
                        pltpu.VMEM((2, NB, BS, BS), F32),
                        pltpu.VMEM((2, BS, BS), F32)],
        compiler_params=_cparams(("parallel", "parallel")),
        name="moba",
    )(proj, proj, proj, bias_tiles, far_rows)


def _merge_kernel(yr_ref, ym_ref, yb_ref, gr_ref, gm_ref, gb_ref, x_ref,
                  wr_ref, wm_ref, wb_ref, wo_ref, o_ref):
    mixed = gr_ref[...].astype(F32) * jnp.dot(yr_ref[...], wr_ref[...], preferred_element_type=F32)
    mixed = mixed + gm_ref[...].astype(F32) * jnp.dot(ym_ref[...], wm_ref[...],
                                                      preferred_element_type=F32)
    mixed = mixed + gb_ref[...].astype(F32) * jnp.dot(yb_ref[...], wb_ref[...],
                                                      preferred_element_type=F32)
    o_ref[...] = x_ref[...] + jnp.dot(mixed.astype(BF16), wo_ref[...], preferred_element_type=F32)


def _merge(y_ret, y_ml, y_mb, sgates, x2, w_r, w_m, w_b, w_o, layer, *, tm):
    T, D = x2.shape

    def resident(shape):
        return pl.BlockSpec((None,) + shape, lambda i: (layer, 0, 0), pipeline_mode=pl.Buffered(1))

    return pl.pallas_call(
        _merge_kernel,
        out_shape=jax.ShapeDtypeStruct((T, D), F32),
        grid=(T // tm,),
        in_specs=[pl.BlockSpec((tm, RET_W), lambda i: (i, 0)),
                  pl.BlockSpec((tm, MLSTM_V_W), lambda i: (i, 0)),
                  pl.BlockSpec((tm, MOBA_W), lambda i: (i, 0)),
                  pl.BlockSpec((tm, D), lambda i: (i, 0)),
                  pl.BlockSpec((tm, D), lambda i: (i, 1)),
                  pl.BlockSpec((tm, D), lambda i: (i, 2)),
                  pl.BlockSpec((tm, D), lambda i: (i, 0)),
                  resident((RET_W, D)), resident((MLSTM_V_W, D)), resident((MOBA_W, D)),
                  resident((D, D))],
        out_specs=pl.BlockSpec((tm, D), lambda i: (i, 0)),
        compiler_params=_cparams(("parallel",)),
        name="merge",
    )(y_ret, y_ml, y_mb, sgates, sgates, sgates, x2, w_r, w_m, w_b, w_o)


def _ffn_kernel(x_ref, nw_ref, fw_ref, w1_ref, w2_ref, o_ref, hn_ref, *, row_chunk, final_norm):
    j = pl.program_id(1)

    @pl.when(j == 0)
    def _():
        def body(c, carry):
            r = pl.multiple_of(c * row_chunk, row_chunk)
            xf = x_ref[pl.ds(r, row_chunk), :]
            hn_ref[pl.ds(r, row_chunk), :] = _rms_rows(xf, nw_ref[...]).astype(BF16)
            o_ref[pl.ds(r, row_chunk), :] = xf
            return carry
        lax.fori_loop(0, x_ref.shape[0] // row_chunk, body, 0)

    u = jnp.dot(hn_ref[...], w1_ref[...], preferred_element_type=F32)
    u = jnp.maximum(u, 0.0)
    o_ref[...] += jnp.dot((u * u).astype(BF16), w2_ref[...], preferred_element_type=F32)

    if final_norm:
        @pl.when(j == pl.num_programs(1) - 1)
        def _():
            def body(c, carry):
                r = pl.multiple_of(c * row_chunk, row_chunk)
                o_ref[pl.ds(r, row_chunk), :] = _rms_rows(o_ref[pl.ds(r, row_chunk), :], fw_ref[...])
                return carry
            lax.fori_loop(0, x_ref.shape[0] // row_chunk, body, 0)


def _ffn(x2, nw, fw, w1, w2, layer, *, tm, tf, final_norm):
    T, D = x2.shape
    F = w1.shape[2]
    return pl.pallas_call(
        functools.partial(_ffn_kernel, row_chunk=128, final_norm=final_norm),
        out_shape=jax.ShapeDtypeStruct((T, D), F32),
        grid=(T // tm, F // tf),
        in_specs=[pl.BlockSpec((tm, D), lambda i, j: (i, 0)),
                  pl.BlockSpec((1, D), lambda i, j: (0, 0)),
                  pl.BlockSpec((1, D), lambda i, j: (0, 0)),
                  pl.BlockSpec((None, D, tf), lambda i, j: (layer, 0, j)),
                  pl.BlockSpec((None, tf, D), lambda i, j: (layer, j, 0))],
        out_specs=pl.BlockSpec((tm, D), lambda i, j: (i, 0)),
        scratch_shapes=[pltpu.VMEM((tm, D), BF16)],
        compiler_params=_cparams(("parallel", "arbitrary")),
        name="ffn",
    )(x2, nw, fw, w1, w2)


def _rope_tables(S):
    half = RET_HEAD_DIM // 2
    inv = ROPE_BASE ** (-jnp.arange(half, dtype=F32) / half)
    ang = jnp.arange(S).astype(F32)[:, None] * inv[None, :]
    cos, sin = jnp.cos(ang), jnp.sin(ang)
    return jnp.concatenate([cos, cos], axis=-1), jnp.concatenate([-sin, sin], axis=-1)


def _retention_tables():
    H, L, dh = RET_HEADS, RET_CHUNK, RET_HEAD_DIM
    log_gamma = jnp.log1p(-jnp.exp2(-5.0 - jnp.arange(H, dtype=F32)))
    idx = jnp.arange(L, dtype=F32)
    diff = idx[:, None] - idx[None, :]
    decay = jnp.where(diff >= 0, jnp.exp(jnp.maximum(diff, 0.0) * log_gamma[:, None, None]), 0.0)
    zeta = jnp.exp((L - 1 - idx)[None, :] * log_gamma[:, None])
    xi = jnp.exp((idx + 1.0)[None, :] * log_gamma[:, None])
    return (decay, jnp.broadcast_to(xi[:, :, None], (H, L, dh)),
            jnp.broadcast_to(zeta[:, :, None], (H, L, dh)))


def _t5_bucket(dist):
    n = jnp.maximum(dist, 0)
    exact = REL_BUCKETS // 2
    nf = jnp.maximum(n, 1).astype(F32)
    large = exact + (jnp.log(nf / exact) / math.log(REL_MAX_DIST / exact)
                     * (REL_BUCKETS - exact)).astype(jnp.int32)
    large = jnp.minimum(large, REL_BUCKETS - 1)
    return jnp.where(n < exact, n, large)


def _moba_bias_tables(rel_bias, S):
    BS = MOBA_BLOCK
    assert BS + 1 >= REL_MAX_DIST and S % BS == 0
    table_t = rel_bias.T.astype(F32) * LOG2E
    onehot = (_t5_bucket(jnp.arange(2 * BS))[:, None] == jnp.arange(REL_BUCKETS)).astype(F32)
    by_dist = jnp.einsum("db,hb->hd", onehot, table_t, precision=lax.Precision.HIGHEST)
    far = table_t[:, REL_BUCKETS - 1]
    far_rows = jnp.broadcast_to(far[:, None, None], (far.shape[0], 1, BS))
    return by_dist[:, None, :], far_rows


def kernel(x, w_in, mlstm_gate_b, mlstm_conv_w, mlstm_conv_b, mlstm_norm_w, w_branch_ret,
           w_branch_mlstm, w_branch_moba, w_out, norm_mix_w, norm_mlp_w, w_ff1, w_ff2, rel_bias,
           final_norm_w):
    B, S, D = x.shape
    T = B * S
    depth = w_in.shape[0]
    x2 = x.reshape(T, D)

    cos_t, sin_t = _rope_tables(S)
    decay, xi_b, zeta_b = _retention_tables()
    bias_by_dist, far_rows = _moba_bias_tables(rel_bias, S)
    final_w = final_norm_w.reshape(1, D)

    scale_row = np.ones((1, w_in.shape[2] - _PACK_SHIFT), np.float32)
    scale_row[:, _OFF_RQ:_OFF_RQ + RET_W] = RET_HEAD_DIM ** -0.5
    scale_row[:, _OFF_MI:_OFF_MI + MOBA_W] = MOBA_HEAD_DIM ** -0.5 * LOG2E
    w_cat = _pack_w_in(w_in, jnp.asarray(scale_row), **_TILES["pack"])
    n_gate_cols = w_cat.shape[2] - _N_HEADMAJOR
    w_r, w_m, w_b, w_o, w_1, w_2 = [_cast_bf16(w, **_TILES["cast"]) for w in
                                    (w_branch_ret, w_branch_mlstm, w_branch_moba, w_out, w_ff1, w_ff2)]

    for l in range(depth):
        w_gpre = jnp.pad(w_in[l, :, _OFF_MI:_OFF_BQ],
                         ((0, 0), (0, LANES - 2 * MLSTM_HEADS))).astype(BF16)

        proj, hn, gpre = _in_proj(x2, norm_mix_w[l].reshape(1, D), w_cat, l, w_gpre,
                                  **_TILES["in_proj"])
        sgates = _matmul(hn, w_cat, l, _N_HEADMAJOR, n_gate_cols, act="sigmoid", out_dtype=BF16,
                         name="branch_gates", **_TILES["gates"])

        y_ret = _retention(proj, cos_t, sin_t, decay, xi_b, zeta_b, B, S)
        gate_b_row = jnp.pad(mlstm_gate_b[l].reshape(1, 2 * MLSTM_HEADS),
                             ((0, 0), (0, LANES - 2 * MLSTM_HEADS)))
        y_ml = _mlstm(proj, gpre, gate_b_row, mlstm_conv_w[l], mlstm_conv_b[l].reshape(1, -1),
                      mlstm_norm_w[l].reshape(1, -1), B, S)
        y_mb = _moba(proj, bias_by_dist, far_rows, B, S)

        x2 = _merge(y_ret, y_ml, y_mb, sgates, x2, w_r, w_m, w_b, w_o, l, **_TILES["merge"])
        x2 = _ffn(x2, norm_mlp_w[l].reshape(1, D), final_w, w_1, w_2, l,
                  final_norm=(l == depth - 1), **_TILES["ffn"])

    return x2.reshape(B, S, D)
```

```python
import functools
import math

import numpy as np
import jax
import jax.numpy as jnp
from jax import lax
from jax.experimental import pallas as pl
from jax.experimental.pallas import tpu as pltpu

D_MODEL = 2048
RET_HEADS = 8
RET_HEAD_DIM = 128
RET_CHUNK = 128
MLSTM_HEADS = 4
MLSTM_QK_DIM = 128
MLSTM_V_DIM = 256
MLSTM_CHUNK = 128
MLSTM_CONV = 4
MOBA_HEADS = 8
MOBA_HEAD_DIM = 128
MOBA_BLOCK = 256
MOBA_TOPK = 3
REL_BUCKETS = 32
REL_MAX_DIST = 128
D_FF = 4 * D_MODEL
ROPE_BASE = 10000.0
EPS = 1e-6

RET_W = RET_HEADS * RET_HEAD_DIM
MLSTM_QK_W = MLSTM_HEADS * MLSTM_QK_DIM
MLSTM_V_W = MLSTM_HEADS * MLSTM_V_DIM
MOBA_W = MOBA_HEADS * MOBA_HEAD_DIM

LANES = 128
SUBLANES = 8
VMEM_LIMIT_BYTES = 56 * 1024 * 1024

_OFF_RQ = 0
_OFF_MI = 4 * RET_W + 2 * MLSTM_QK_W + 2 * MLSTM_V_W
_OFF_BQ = _OFF_MI + 2 * MLSTM_HEADS
_OFF_GATES = _OFF_BQ + 3 * MOBA_W
_N_HEADMAJOR = _OFF_MI + 3 * MOBA_W
_BLK_RQ, _BLK_RK, _BLK_RV, _BLK_RG = 0, 8, 16, 24
_BLK_MQ, _BLK_MK, _BLK_MV, _BLK_MO = 32, 36, 40, 48
_BLK_BQ, _BLK_BK, _BLK_BV = 56, 64, 72

RET_SKEW = 1
NEG_BIG = -1e30
LOG2E = 1.0 / math.log(2.0)

_TILES = {
    "in_proj": dict(tm=1024, tn=1024),
    "gates": dict(tm=1024, tn=2048),
    "merge": dict(tm=256),
    "ffn": dict(tm=512, tf=1024),
    "pack": dict(tr=1024),
    "cast": dict(tr=1024, tc=2048),
}

BF16 = jnp.bfloat16
F32 = jnp.float32


def _cparams(sem):
    return pltpu.CompilerParams(dimension_semantics=sem, vmem_limit_bytes=VMEM_LIMIT_BYTES)


def _rms_rows(xf, w_row):
    ms = jnp.mean(xf * xf, axis=-1, keepdims=True)
    return xf * lax.rsqrt(ms + EPS) * w_row


def _norm_rows_to(x_ref, nw_ref, hn_ref, row_chunk):
    def body(c, carry):
        r = pl.multiple_of(c * row_chunk, row_chunk)
        hn_ref[pl.ds(r, row_chunk), :] = _rms_rows(x_ref[pl.ds(r, row_chunk), :],
                                                   nw_ref[...]).astype(BF16)
        return carry
    lax.fori_loop(0, x_ref.shape[0] // row_chunk, body, 0)


def _in_proj_kernel(x_ref, nw_ref, w_ref, ws_ref, o_ref, hn_ref, os_ref, *, row_chunk):
    @pl.when(pl.program_id(1) == 0)
    def _():
        _norm_rows_to(x_ref, nw_ref, hn_ref, row_chunk)
        os_ref[...] = jnp.dot(hn_ref[...], ws_ref[...], preferred_element_type=F32)

    acc = jnp.dot(hn_ref[...], w_ref[...], preferred_element_type=F32)
    for c in range(o_ref.shape[0]):
        o_ref[c] = acc[:, c * LANES:(c + 1) * LANES].astype(o_ref.dtype)


def _in_proj(x2, nw, w_cat, layer, w_side, *, tm, tn):
    T, D = x2.shape
    N = _N_HEADMAJOR
    NS = w_side.shape[2]
    return pl.pallas_call(
        functools.partial(_in_proj_kernel, row_chunk=128),
        out_shape=(jax.ShapeDtypeStruct((N // LANES, T, LANES), BF16),
                   jax.ShapeDtypeStruct((T, D), BF16),
                   jax.ShapeDtypeStruct((T, NS), F32)),
        grid=(T // tm, N // tn),
        in_specs=[pl.BlockSpec((tm, D), lambda i, j: (i, 0)),
                  pl.BlockSpec((1, D), lambda i, j: (0, 0)),
                  pl.BlockSpec((None, D, tn), lambda i, j: (layer, 0, j)),
                  pl.BlockSpec((None, D, NS), lambda i, j: (layer, 0, 0))],
        out_specs=(pl.BlockSpec((tn // LANES, tm, LANES), lambda i, j: (j, i, 0)),
                   pl.BlockSpec((tm, D), lambda i, j: (i, 0)),
                   pl.BlockSpec((tm, NS), lambda i, j: (i, 0))),
        compiler_params=_cparams(("parallel", "arbitrary")),
        name="in_proj",
    )(x2, nw, w_cat, w_side)


def _matmul_kernel(h_ref, w_ref, o_ref, *, act):
    acc = jnp.dot(h_ref[...], w_ref[...], preferred_element_type=F32)
    if act == "sigmoid":
        acc = jax.nn.sigmoid(acc)
    o_ref[...] = acc.astype(o_ref.dtype)


def _matmul(hn, w_cat, layer, col0, n_cols, *, tm, tn, act, out_dtype, name):
    T, D = hn.shape
    assert col0 % tn == 0 and n_cols % tn == 0
    return pl.pallas_call(
        functools.partial(_matmul_kernel, act=act),
        out_shape=jax.ShapeDtypeStruct((T, n_cols), out_dtype),
        grid=(T // tm, n_cols // tn),
        in_specs=[pl.BlockSpec((tm, D), lambda i, j: (i, 0)),
                  pl.BlockSpec((None, D, tn), lambda i, j: (layer, 0, col0 // tn + j))],
        out_specs=pl.BlockSpec((tm, tn), lambda i, j: (i, j)),
        compiler_params=_cparams(("parallel", "parallel")),
        name=name,
    )(hn, w_cat)


def _cast_kernel(w_ref, o_ref, *, row_chunk):
    def body(c, carry):
        r = pl.multiple_of(c * row_chunk, row_chunk)
        o_ref[pl.ds(r, row_chunk), :] = w_ref[pl.ds(r, row_chunk), :].astype(BF16)
        return carry
    lax.fori_loop(0, w_ref.shape[0] // row_chunk, body, 0)


def _cast_bf16(w3, *, tr, tc):
    Lw, R, C = w3.shape
    return pl.pallas_call(
        functools.partial(_cast_kernel, row_chunk=128),
        out_shape=jax.ShapeDtypeStruct((Lw, R, C), BF16),
        grid=(Lw, R // tr, C // tc),
        in_specs=[pl.BlockSpec((None, tr, tc), lambda l, i, j: (l, i, j))],
        out_specs=pl.BlockSpec((None, tr, tc), lambda l, i, j: (l, i, j)),
        compiler_params=_cparams(("parallel", "parallel", "parallel")),
        name="cast_bf16",
    )(w3)


_PACK_TC = 1024
_PACK_SHIFT = 2 * MLSTM_HEADS
_PACK_ALIGNED_TILES = _OFF_MI // _PACK_TC


def _pack_w_in_kernel(a_ref, b_ref, s_ref, o_ref, g_ref):
    j = pl.program_id(2)
    tc, tr = a_ref.shape
    blk = LANES

    def emit(shift):
        for rb in range(tc // blk):
            lo, hi = rb * blk + shift, (rb + 1) * blk + shift
            for cb in range(tr // blk):
                cols = slice(cb * blk, (cb + 1) * blk)
                if hi <= tc:
                    x = a_ref[lo:hi, cols]
                else:
                    x = jnp.concatenate([a_ref[lo:tc, cols], b_ref[0:hi - tc, cols]], axis=0)
                y = x.T * s_ref[:, rb * blk:(rb + 1) * blk]
                o_ref[cols, rb * blk:(rb + 1) * blk] = y.astype(BF16)

    @pl.when(j < _PACK_ALIGNED_TILES)
    def _():
        emit(0)

    @pl.when(j >= _PACK_ALIGNED_TILES)
    def _():
        emit(_PACK_SHIFT)

    @pl.when(j == _PACK_ALIGNED_TILES)
    def _():
        zeros = jnp.zeros((blk - _PACK_SHIFT, blk), F32)
        for cb in range(tr // blk):
            cols = slice(cb * blk, (cb + 1) * blk)
            x = jnp.concatenate([a_ref[0:_PACK_SHIFT, cols], zeros], axis=0)
            g_ref[cols, :] = x.T.astype(BF16)


def _pack_w_in(w_in_t, scale_row, *, tr):
    depth, d_in, D = w_in_t.shape
    tc = _PACK_TC
    n_out = d_in - _PACK_SHIFT
    assert _OFF_MI % tc == 0 and n_out % tc == 0 and tc % LANES == 0 and _PACK_SHIFT == SUBLANES
    return pl.pallas_call(
        _pack_w_in_kernel,
        out_shape=(jax.ShapeDtypeStruct((depth, D, n_out), BF16),
                   jax.ShapeDtypeStruct((depth, D, LANES), BF16)),
        grid=(depth, D // tr, n_out // tc),
        in_specs=[pl.BlockSpec((None, tc, tr), lambda l, i, j: (l, j, i)),
                  pl.BlockSpec((None, SUBLANES, tr),
                               lambda l, i, j: (l, (j + 1) * (tc // SUBLANES), i)),
                  pl.BlockSpec((1, tc), lambda l, i, j: (0, j))],
        out_specs=(pl.BlockSpec((None, tr, tc), lambda l, i, j: (l, i, j)),
                   pl.BlockSpec((None, tr, LANES), lambda l, i, j: (l, i, 0))),
        compiler_params=_cparams(("parallel", "parallel", "arbitrary")),
        name="pack_w_in",
    )(w_in_t, w_in_t, scale_row)


def _retention_kernel(q_ref, k_ref, v_ref, g_ref, cos_ref, sin_ref, decay_ref, xi_ref, zeta_ref,
                      o_ref):
    L = RET_CHUNK
    S = q_ref.shape[0]
    decay = decay_ref[...]
    xi = xi_ref[...]
    zeta = zeta_ref[...]
    gamma_l = xi[L - 1:L, :]

    def rope(x, cos, sin_signed):
        return x * cos + pltpu.roll(x, RET_HEAD_DIM // 2, 1) * sin_signed

    def state_free(n):
        rows = slice(n * L, (n + 1) * L)
        cos = cos_ref[rows, :]
        sin = sin_ref[rows, :]
        q = rope(q_ref[rows, :].astype(F32), cos, sin)
        k = rope(k_ref[rows, :].astype(F32), cos, sin)
        v = v_ref[rows, :]
        qb = q.astype(BF16)
        scores = lax.dot_general(qb, k.astype(BF16), (((1,), (1,)), ((), ())),
                                 preferred_element_type=F32) * decay
        kv = jnp.dot((k * zeta).T.astype(BF16), v, preferred_element_type=F32)
        return qb, scores.astype(BF16), v, kv

    def finish(n, parts, r_prev):
        qb, scores, v, kv = parts
        rows = slice(n * L, (n + 1) * L)
        intra = jnp.dot(scores, v, preferred_element_type=F32)
        inter = jnp.dot(qb, r_prev.astype(BF16), preferred_element_type=F32) * xi
        o = intra + inter
        o = o * lax.rsqrt(jnp.mean(o * o, axis=-1, keepdims=True) + EPS)
        g = g_ref[rows, :].astype(F32)
        o_ref[rows, :] = (o * (g * jax.nn.sigmoid(g))).astype(o_ref.dtype)
        return r_prev * gamma_l + kv

    n_chunks = S // L
    r_prev = jnp.zeros((RET_HEAD_DIM, RET_HEAD_DIM), F32)
    pending = [state_free(n) for n in range(min(RET_SKEW, n_chunks))]
    for n in range(n_chunks):
        if n + RET_SKEW < n_chunks:
            pending.append(state_free(n + RET_SKEW))
        r_prev = finish(n, pending.pop(0), r_prev)


def _retention(proj, cos_t, sin_t, decay, xi_b, zeta_b, B, S):
    H, dh, L = RET_HEADS, RET_HEAD_DIM, RET_CHUNK

    def pspec(base):
        return pl.BlockSpec((None, S, LANES), lambda b, h, base=base: (base + h, b, 0))

    return pl.pallas_call(
        _retention_kernel,
        out_shape=jax.ShapeDtypeStruct((B * S, RET_W), BF16),
        grid=(B, H),
        in_specs=[pspec(_BLK_RQ), pspec(_BLK_RK), pspec(_BLK_RV), pspec(_BLK_RG),
                  pl.BlockSpec((S, dh), lambda b, h: (0, 0)),
                  pl.BlockSpec((S, dh), lambda b, h: (0, 0)),
                  pl.BlockSpec((None, L, L), lambda b, h: (h, 0, 0)),
                  pl.BlockSpec((None, L, dh), lambda b, h: (h, 0, 0)),
                  pl.BlockSpec((None, L, dh), lambda b, h: (h, 0, 0))],
        out_specs=pl.BlockSpec((S, dh), lambda b, h: (b, h)),
        compiler_params=_cparams(("parallel", "parallel")),
        name="retention",
    )(proj, proj, proj, proj, cos_t, sin_t, decay, xi_b, zeta_b)


def _mlstm_kernel(q_ref, k_ref, v_ref, og_ref, gp_ref, gb_ref, cwq_ref, cwk_ref, cbq_ref, cbk_ref,
                  nw_ref, o_ref, pad_ref, qc_ref, kc_ref, a_ref, bc_ref, at_ref, bt_ref):
    L = MLSTM_CHUNK
    S = q_ref.shape[0]
    NH = MLSTM_HEADS
    KC = MLSTM_CONV
    h = pl.program_id(1)
    n_chunks = S // L

    def conv_silu(src_ref, cw_ref, cb_ref, dst_ref, scale):
        pad_ref[0:SUBLANES, :] = jnp.zeros((SUBLANES, LANES), F32)
        pad_ref[SUBLANES:SUBLANES + S, :] = src_ref[...].astype(F32)
        cw = cw_ref[...]
        cb = cb_ref[...]
        for c in range(n_chunks):
            acc = cb
            for t in range(KC):
                r0 = SUBLANES + c * L - (KC - 1) + t
                acc = acc + pad_ref[r0:r0 + L, :] * cw[t:t + 1, :]
            y = acc * jax.nn.sigmoid(acc)
            dst_ref[c * L:(c + 1) * L, :] = y * scale

    conv_silu(q_ref, cwq_ref, cbq_ref, qc_ref, MLSTM_QK_DIM ** -0.5)
    conv_silu(k_ref, cwk_ref, cbk_ref, kc_ref, 1.0)

    lane = lax.broadcasted_iota(jnp.int32, (L, LANES), 1)
    sub = lax.broadcasted_iota(jnp.int32, (L, LANES), 0)
    sub_col = lax.broadcasted_iota(jnp.int32, (L, 1), 0)
    causal = sub >= lane
    nw = nw_ref[...]

    @pl.when(h == 0)
    def _():
        tri = (sub >= lane).astype(F32)
        is_f = (lane >= NH) & (lane < 2 * NH)
        for n in range(n_chunks):
            rows = slice(n * L, (n + 1) * L)
            gp = gp_ref[rows, :] + gb_ref[...]
            logsig = jnp.minimum(gp, 0.0) - jnp.log1p(jnp.exp(-jnp.abs(gp)))
            a = jnp.where(is_f, logsig, gp)
            bc = jnp.dot(tri, a, preferred_element_type=F32, precision=lax.Precision.HIGHEST)
            a_ref[rows, :] = a
            bc_ref[rows, :] = bc
            at_ref[n * SUBLANES:(n + 1) * SUBLANES, :] = a.T[0:SUBLANES, :]
            bt_ref[n * SUBLANES:(n + 1) * SUBLANES, :] = bc.T[0:SUBLANES, :]

    def state_free(n):
        rows = slice(n * L, (n + 1) * L)
        li_col = jnp.sum(jnp.where(lane == h, a_ref[rows, :], 0.0), axis=1, keepdims=True)
        b_col = jnp.sum(jnp.where(lane == h + NH, bc_ref[rows, :], 0.0), axis=1, keepdims=True)
        li_row = at_ref[pl.ds(n * SUBLANES + h, 1), :]
        b_row = bt_ref[pl.ds(n * SUBLANES + NH + h, 1), :]
        gtot = jnp.max(jnp.where(sub_col == L - 1, b_col, NEG_BIG), axis=0, keepdims=True)
        q = qc_ref[rows, :]
        k = kc_ref[rows, :]
        v = jnp.concatenate([v_ref[0, rows, :], v_ref[1, rows, :]], axis=1)
        qb = q.astype(BF16)
        dlog = jnp.where(causal, b_col - b_row + li_row, NEG_BIG)
        dmax = jnp.max(dlog, axis=1, keepdims=True)
        qk = lax.dot_general(qb, k.astype(BF16), (((1,), (1,)), ((), ())),
                             preferred_element_type=F32)
        return dict(li_col=li_col, b_col=b_col, gtot=gtot, q=q, k=k, qb=qb, v=v, dlog=dlog,
                    dmax=dmax, qk=qk)

    def finish(n, p, state):
        c_prev, n_prev, m_prev = state
        rows = slice(n * L, (n + 1) * L)
        inter_log = p["b_col"] + m_prev
        m_pos = jnp.maximum(inter_log, p["dmax"])
        s_intra = p["qk"] * jnp.exp(p["dlog"] - m_pos)
        inter_scale = jnp.exp(inter_log - m_pos)
        num = (jnp.dot(s_intra.astype(BF16), p["v"], preferred_element_type=F32)
               + inter_scale * jnp.dot(p["qb"], c_prev.astype(BF16), preferred_element_type=F32))
        den = (jnp.sum(s_intra, axis=1, keepdims=True)
               + inter_scale * jnp.sum(p["q"] * n_prev, axis=1, keepdims=True))
        inv = 1.0 / jnp.maximum(jnp.abs(den), jnp.exp(-m_pos))
        ms = jnp.mean(num * num, axis=-1, keepdims=True)
        row_scale = inv * lax.rsqrt(inv * inv * ms + EPS)
        og = jnp.concatenate([og_ref[0, rows, :], og_ref[1, rows, :]], axis=1).astype(F32)
        o_ref[rows, :] = (num * row_scale * nw * jax.nn.sigmoid(og)).astype(o_ref.dtype)
        w_loc = p["gtot"] - p["b_col"] + p["li_col"]
        m_loc = jnp.max(w_loc, axis=0, keepdims=True)
        ke = p["k"] * jnp.exp(w_loc - m_loc)
        c_loc = jnp.dot(ke.T.astype(BF16), p["v"], preferred_element_type=F32)
        n_loc = jnp.sum(ke, axis=0, keepdims=True)
        m_new = jnp.maximum(p["gtot"] + m_prev, m_loc)
        a_dec = jnp.exp(p["gtot"] + m_prev - m_new)
        b_inc = jnp.exp(m_loc - m_new)
        return (a_dec * c_prev + b_inc * c_loc, a_dec * n_prev + b_inc * n_loc, m_new)

    state = (jnp.zeros((MLSTM_QK_DIM, MLSTM_V_DIM), F32), jnp.zeros((1, MLSTM_QK_DIM), F32),
             jnp.zeros((1, 1), F32))
    for n in range(n_chunks):
        state = finish(n, state_free(n), state)


def _mlstm(proj, gpre, gate_b_row, conv_w, conv_b, norm_w, B, S):
    NH, dk, dv = MLSTM_HEADS, MLSTM_QK_DIM, MLSTM_V_DIM
    KC = MLSTM_CONV
    nvb = dv // LANES

    def pspec(base):
        return pl.BlockSpec((None, S, LANES), lambda b, h, base=base: (base + h, b, 0))

    def pspec2(base):
        return pl.BlockSpec((nvb, S, LANES), lambda b, h, base=base: (base // nvb + h, b, 0))

    return pl.pallas_call(
        _mlstm_kernel,
        out_shape=jax.ShapeDtypeStruct((B * S, MLSTM_V_W), BF16),
        grid=(B, NH),
        in_specs=[pspec(_BLK_MQ), pspec(_BLK_MK), pspec2(_BLK_MV), pspec2(_BLK_MO),
                  pl.BlockSpec((S, LANES), lambda b, h: (b, 0)),
                  pl.BlockSpec((1, LANES), lambda b, h: (0, 0)),
                  pl.BlockSpec((KC, dk), lambda b, h: (0, h)),
                  pl.BlockSpec((KC, dk), lambda b, h: (0, NH + h)),
                  pl.BlockSpec((1, dk), lambda b, h: (0, h)),
                  pl.BlockSpec((1, dk), lambda b, h: (0, NH + h)),
                  pl.BlockSpec((1, dv), lambda b, h: (0, h))],
        out_specs=pl.BlockSpec((S, dv), lambda b, h: (b, h)),
        scratch_shapes=[pltpu.VMEM((S + SUBLANES, dk), F32),
                        pltpu.VMEM((S, dk), F32),
                        pltpu.VMEM((S, dk), F32),
                        pltpu.VMEM((S, LANES), F32),
                        pltpu.VMEM((S, LANES), F32),
                        pltpu.VMEM((S // MLSTM_CHUNK * SUBLANES, MLSTM_CHUNK), F32),
                        pltpu.VMEM((S // MLSTM_CHUNK * SUBLANES, MLSTM_CHUNK), F32)],
        compiler_params=_cparams(("parallel", "arbitrary")),
        name="mlstm",
    )(proj, proj, proj, proj, gpre, gate_b_row, conv_w, conv_w, conv_b, conv_b, norm_w)


def _moba_kernel(q_ref, k_ref, v_ref, bvec_ref, far_ref, o_ref, kmean_ref, gate_ref, lg_ref,
                 bias_ref):
    BS = MOBA_BLOCK
    S = k_ref.shape[0]
    NB = S // BS
    nt = (((1,), (1,)), ((), ()))

    toep = pltpu.roll(jnp.broadcast_to(bvec_ref[...], (BS, 2 * BS)), 0, 1, stride=1, stride_axis=0)
    bias_ref[0] = toep[:, 0:BS]
    bias_ref[1] = toep[:, BS:2 * BS]

    for j in range(NB):
        kj = k_ref[j * BS:(j + 1) * BS, :].astype(F32)
        kmean_ref[j:j + 1, :] = jnp.mean(kj, axis=0, keepdims=True)
    ones_rows = jnp.ones((2 * SUBLANES, BS), BF16)
    vts = [jnp.concatenate([v_ref[j * BS:(j + 1) * BS, :].astype(F32).T.astype(BF16), ones_rows],
                           axis=0) for j in range(NB)]
    dh = MOBA_HEAD_DIM
    far = far_ref[...]
    kl = lax.broadcasted_iota(jnp.int32, (BS, BS), 0)
    ql = lax.broadcasted_iota(jnp.int32, (BS, BS), 1)
    causal = kl <= ql

    def select_rows(i, q):
        if i <= MOBA_TOPK:
            return [None] * i
        gate_ref[i % 2] = lax.dot_general(kmean_ref[...], q.astype(F32), nt,
                                          preferred_element_type=F32,
                                          precision=lax.Precision.HIGHEST)
        rows = [gate_ref[i % 2, n:n + 1, :] for n in range(i)]
        add_rows = []
        for n in range(i):
            rank = jnp.zeros_like(rows[n])
            for mth in range(i):
                if mth == n:
                    continue
                ahead = rows[mth] > rows[n]
                if mth < n:
                    ahead = ahead | (rows[mth] == rows[n])
                rank = rank + jnp.where(ahead, 1.0, 0.0)
            add_rows.append(jnp.where(rank < MOBA_TOPK, 0.0, NEG_BIG))
        return add_rows

    def logits_tile(i, j, q, add_rows):
        lg = lax.dot_general(k_ref[j * BS:(j + 1) * BS, :], q, nt, preferred_element_type=F32)
        if j == i:
            lg = jnp.where(causal, lg + bias_ref[0], NEG_BIG)
        elif j == i - 1:
            lg = lg + bias_ref[1]
            if add_rows[j] is not None:
                lg = lg + add_rows[j]
        else:
            lg = lg + (far if add_rows[j] is None else far + add_rows[j])
        lg_ref[i % 2, j] = lg
        return jnp.max(lg, axis=0, keepdims=True)

    def value_tile(i, j, m):
        p = jnp.exp2(lg_ref[i % 2, j] - m)
        return jnp.dot(vts[j], p.astype(BF16), preferred_element_type=F32)

    def merge(a, b, op):
        return b if a is None else op(a, b)

    m_cur = logits_tile(0, 0, q_ref[0:BS, :], [])
    for i in range(NB):
        m_next = None
        if i + 1 < NB:
            q_next = q_ref[(i + 1) * BS:(i + 2) * BS, :]
            rows_next = select_rows(i + 1, q_next)
        acc = None
        for j in range(i + 2):
            if i + 1 < NB:
                m_next = merge(m_next, logits_tile(i + 1, j, q_next, rows_next), jnp.maximum)
            if j <= i:
                acc = merge(acc, value_tile(i, j, m_cur), jnp.add)
        l = jnp.max(acc[dh:dh + SUBLANES, :], axis=0, keepdims=True)
        out_t = acc[0:dh, :] * (1.0 / l)
        o_ref[i * BS:(i + 1) * BS, :] = out_t.T.astype(o_ref.dtype)
        m_cur = m_next


def _moba(proj, bias_tiles, far_rows, B, S):
    H, dh, BS = MOBA_HEADS, MOBA_HEAD_DIM, MOBA_BLOCK
    NB = S // BS

    def pspec(base):
        return pl.BlockSpec((None, S, LANES), lambda b, h, base=base: (base + h, b, 0))

    return pl.pallas_call(
        _moba_kernel,
        out_shape=jax.ShapeDtypeStruct((B * S, MOBA_W), BF16),
        grid=(B, H),
        in_specs=[pspec(_BLK_BQ), pspec(_BLK_BK), pspec(_BLK_BV),
                  pl.BlockSpec((None, 1, 2 * BS), lambda b, h: (h, 0, 0)),
                  pl.BlockSpec((None, 1, BS), lambda b, h: (h, 0, 0))],
        out_specs=pl.BlockSpec((S, dh), lambda b, h: (b, h)),
        scratch_shapes=[pltpu.VMEM((NB, dh), F32),
                        pltpu.VMEM((2, NB, BS), F32),
                        pltpu.VMEM((2, NB, BS, BS), F32),
                        pltpu.VMEM((2, BS, BS), F32)],
        compiler_params=_cparams(("parallel", "parallel")),
        name="moba",
    )(proj, proj, proj, bias_tiles, far_rows)


def _merge_kernel(yr_ref, ym_ref, yb_ref, gr_ref, gm_ref, gb_ref, x_ref,
                  wr_ref, wm_ref, wb_ref, wo_ref, o_ref):
    mixed = gr_ref[...].astype(F32) * jnp.dot(yr_ref[...], wr_ref[...], preferred_element_type=F32)
    mixed = mixed + gm_ref[...].astype(F32) * jnp.dot(ym_ref[...], wm_ref[...],
                                                      preferred_element_type=F32)
    mixed = mixed + gb_ref[...].astype(F32) * jnp.dot(yb_ref[...], wb_ref[...],
                                                      preferred_element_type=F32)
    o_ref[...] = x_ref[...] + jnp.dot(mixed.astype(BF16), wo_ref[...], preferred_element_type=F32)


def _merge(y_ret, y_ml, y_mb, sgates, x2, w_r, w_m, w_b, w_o, layer, *, tm):
    T, D = x2.shape

    def resident(shape):
        return pl.BlockSpec((None,) + shape, lambda i: (layer, 0, 0), pipeline_mode=pl.Buffered(1))

    return pl.pallas_call(
        _merge_kernel,
        out_shape=jax.ShapeDtypeStruct((T, D), F32),
        grid=(T // tm,),
        in_specs=[pl.BlockSpec((tm, RET_W), lambda i: (i, 0)),
                  pl.BlockSpec((tm, MLSTM_V_W), lambda i: (i, 0)),
                  pl.BlockSpec((tm, MOBA_W), lambda i: (i, 0)),
                  pl.BlockSpec((tm, D), lambda i: (i, 0)),
                  pl.BlockSpec((tm, D), lambda i: (i, 1)),
                  pl.BlockSpec((tm, D), lambda i: (i, 2)),
                  pl.BlockSpec((tm, D), lambda i: (i, 0)),
                  resident((RET_W, D)), resident((MLSTM_V_W, D)), resident((MOBA_W, D)),
                  resident((D, D))],
        out_specs=pl.BlockSpec((tm, D), lambda i: (i, 0)),
        compiler_params=_cparams(("parallel",)),
        name="merge",
    )(y_ret, y_ml, y_mb, sgates, sgates, sgates, x2, w_r, w_m, w_b, w_o)


def _ffn_kernel(x_ref, nw_ref, fw_ref, w1_ref, w2_ref, o_ref, hn_ref, *, row_chunk, final_norm):
    j = pl.program_id(1)

    @pl.when(j == 0)
    def _():
        def body(c, carry):
            r = pl.multiple_of(c * row_chunk, row_chunk)
            xf = x_ref[pl.ds(r, row_chunk), :]
            hn_ref[pl.ds(r, row_chunk), :] = _rms_rows(xf, nw_ref[...]).astype(BF16)
            o_ref[pl.ds(r, row_chunk), :] = xf
            return carry
        lax.fori_loop(0, x_ref.shape[0] // row_chunk, body, 0)

    u = jnp.dot(hn_ref[...], w1_ref[...], preferred_element_type=F32)
    u = jnp.maximum(u, 0.0)
    o_ref[...] += jnp.dot((u * u).astype(BF16), w2_ref[...], preferred_element_type=F32)

    if final_norm:
        @pl.when(j == pl.num_programs(1) - 1)
        def _():
            def body(c, carry):
                r = pl.multiple_of(c * row_chunk, row_chunk)
                o_ref[pl.ds(r, row_chunk), :] = _rms_rows(o_ref[pl.ds(r, row_chunk), :], fw_ref[...])
                return carry
            lax.fori_loop(0, x_ref.shape[0] // row_chunk, body, 0)


def _ffn(x2, nw, fw, w1, w2, layer, *, tm, tf, final_norm):
    T, D = x2.shape
    F = w1.shape[2]
    return pl.pallas_call(
        functools.partial(_ffn_kernel, row_chunk=128, final_norm=final_norm),
        out_shape=jax.ShapeDtypeStruct((T, D), F32),
        grid=(T // tm, F // tf),
        in_specs=[pl.BlockSpec((tm, D), lambda i, j: (i, 0)),
                  pl.BlockSpec((1, D), lambda i, j: (0, 0)),
                  pl.BlockSpec((1, D), lambda i, j: (0, 0)),
                  pl.BlockSpec((None, D, tf), lambda i, j: (layer, 0, j)),
                  pl.BlockSpec((None, tf, D), lambda i, j: (layer, j, 0))],
        out_specs=pl.BlockSpec((tm, D), lambda i, j: (i, 0)),
        scratch_shapes=[pltpu.VMEM((tm, D), BF16)],
        compiler_params=_cparams(("parallel", "arbitrary")),
        name="ffn",
    )(x2, nw, fw, w1, w2)


def _rope_tables(S):
    half = RET_HEAD_DIM // 2
    inv = ROPE_BASE ** (-jnp.arange(half, dtype=F32) / half)
    ang = jnp.arange(S).astype(F32)[:, None] * inv[None, :]
    cos, sin = jnp.cos(ang), jnp.sin(ang)
    return jnp.concatenate([cos, cos], axis=-1), jnp.concatenate([-sin, sin], axis=-1)


def _retention_tables():
    H, L, dh = RET_HEADS, RET_CHUNK, RET_HEAD_DIM
    log_gamma = jnp.log1p(-jnp.exp2(-5.0 - jnp.arange(H, dtype=F32)))
    idx = jnp.arange(L, dtype=F32)
    diff = idx[:, None] - idx[None, :]
    decay = jnp.where(diff >= 0, jnp.exp(jnp.maximum(diff, 0.0) * log_gamma[:, None, None]), 0.0)
    zeta = jnp.exp((L - 1 - idx)[None, :] * log_gamma[:, None])
    xi = jnp.exp((idx + 1.0)[None, :] * log_gamma[:, None])
    return (decay, jnp.broadcast_to(xi[:, :, None], (H, L, dh)),
            jnp.broadcast_to(zeta[:, :, None], (H, L, dh)))


def _t5_bucket(dist):
    n = jnp.maximum(dist, 0)
    exact = REL_BUCKETS // 2
    nf = jnp.maximum(n, 1).astype(F32)
    large = exact + (jnp.log(nf / exact) / math.log(REL_MAX_DIST / exact)
                     * (REL_BUCKETS - exact)).astype(jnp.int32)
    large = jnp.minimum(large, REL_BUCKETS - 1)
    return jnp.where(n < exact, n, large)


def _moba_bias_tables(rel_bias, S):
    BS = MOBA_BLOCK
    assert BS + 1 >= REL_MAX_DIST and S % BS == 0
    table_t = rel_bias.T.astype(F32) * LOG2E
    onehot = (_t5_bucket(jnp.arange(2 * BS))[:, None] == jnp.arange(REL_BUCKETS)).astype(F32)
    by_dist = jnp.einsum("db,hb->hd", onehot, table_t, precision=lax.Precision.HIGHEST)
    far = table_t[:, REL_BUCKETS - 1]
    far_rows = jnp.broadcast_to(far[:, None, None], (far.shape[0], 1, BS))
    return by_dist[:, None, :], far_rows


def kernel(x, w_in, mlstm_gate_b, mlstm_conv_w, mlstm_conv_b, mlstm_norm_w, w_branch_ret,
           w_branch_mlstm, w_branch_moba, w_out, norm_mix_w, norm_mlp_w, w_ff1, w_ff2, rel_bias,
           final_norm_w):
    B, S, D = x.shape
    T = B * S
    depth = w_in.shape[0]
    x2 = x.reshape(T, D)

    cos_t, sin_t = _rope_tables(S)
    decay, xi_b, zeta_b = _retention_tables()
    bias_by_dist, far_rows = _moba_bias_tables(rel_bias, S)
    final_w = final_norm_w.reshape(1, D)

    scale_row = np.ones((1, w_in.shape[2] - _PACK_SHIFT), np.float32)
    scale_row[:, _OFF_RQ:_OFF_RQ + RET_W] = RET_HEAD_DIM ** -0.5
    scale_row[:, _OFF_MI:_OFF_MI + MOBA_W] = MOBA_HEAD_DIM ** -0.5 * LOG2E
    w_cat, w_gpre = _pack_w_in(jnp.swapaxes(w_in, 1, 2), jnp.asarray(scale_row), **_TILES["pack"])
    n_gate_cols = w_cat.shape[2] - _N_HEADMAJOR
    w_r, w_m, w_b, w_o, w_1, w_2 = [_cast_bf16(w, **_TILES["cast"]) for w in
                                    (w_branch_ret, w_branch_mlstm, w_branch_moba, w_out, w_ff1, w_ff2)]

    for l in range(depth):
        proj, hn, gpre = _in_proj(x2, norm_mix_w[l].reshape(1, D), w_cat, l, w_gpre,
                                  **_TILES["in_proj"])
        sgates = _matmul(hn, w_cat, l, _N_HEADMAJOR, n_gate_cols, act="sigmoid", out_dtype=BF16,
                         name="branch_gates", **_TILES["gates"])

        y_ret = _retention(proj, cos_t, sin_t, decay, xi_b, zeta_b, B, S)
        gate_b_row = jnp.pad(mlstm_gate_b[l].reshape(1, 2 * MLSTM_HEADS),
                             ((0, 0), (0, LANES - 2 * MLSTM_HEADS)))
        y_ml = _mlstm(proj, gpre, gate_b_row, mlstm_conv_w[l], mlstm_conv_b[l].reshape(1, -1),
                      mlstm_norm_w[l].reshape(1, -1), B, S)
        y_mb = _moba(proj, bias_by_dist, far_rows, B, S)

        x2 = _merge(y_ret, y_ml, y_mb, sgates, x2, w_r, w_m, w_b, w_o, l, **_TILES["merge"])
        x2 = _ffn(x2, norm_mlp_w[l].reshape(1, D), final_w, w_1, w_2, l,
                  final_norm=(l == depth - 1), **_TILES["ffn"])

    return x2.reshape(B, S, D)
```

```python
import functools
import math

import numpy as np
import jax
import jax.numpy as jnp
from jax import lax
from jax.experimental import pallas as pl
from jax.experimental.pallas import tpu as pltpu

D_MODEL = 2048
RET_HEADS = 8
RET_HEAD_DIM = 128
RET_CHUNK = 128
MLSTM_HEADS = 4
MLSTM_QK_DIM = 128
MLSTM_V_DIM = 256
MLSTM_CHUNK = 128
MLSTM_CONV = 4
MOBA_HEADS = 8
MOBA_HEAD_DIM = 128
MOBA_BLOCK = 256
MOBA_TOPK = 3
REL_BUCKETS = 32
REL_MAX_DIST = 128
D_FF = 4 * D_MODEL
ROPE_BASE = 10000.0
EPS = 1e-6

RET_W = RET_HEADS * RET_HEAD_DIM
MLSTM_QK_W = MLSTM_HEADS * MLSTM_QK_DIM
MLSTM_V_W = MLSTM_HEADS * MLSTM_V_DIM
MOBA_W = MOBA_HEADS * MOBA_HEAD_DIM

LANES = 128
SUBLANES = 8
VMEM_LIMIT_BYTES = 56 * 1024 * 1024

_OFF_RQ = 0
_OFF_MI = 4 * RET_W + 2 * MLSTM_QK_W + 2 * MLSTM_V_W
_OFF_BQ = _OFF_MI + 2 * MLSTM_HEADS
_OFF_GATES = _OFF_BQ + 3 * MOBA_W
_N_HEADMAJOR = _OFF_MI + 3 * MOBA_W
_BLK_RQ, _BLK_RK, _BLK_RV, _BLK_RG = 0, 8, 16, 24
_BLK_MQ, _BLK_MK, _BLK_MV, _BLK_MO = 32, 36, 40, 48
_BLK_BQ, _BLK_BK, _BLK_BV = 56, 64, 72

RET_HEADS_PER_STEP = 1
RET_SKEW = 1
NEG_BIG = -1e30
LOG2E = 1.0 / math.log(2.0)

_TILES = {
    "in_proj": dict(tm=1024, tn=1024),
    "gates": dict(tm=1024, tn=2048),
    "merge": dict(tm=256),
    "ffn": dict(tm=512, tf=1024),
    "pack": dict(tr=1024),
    "cast": dict(tr=1024, tc=2048),
}

BF16 = jnp.bfloat16
F32 = jnp.float32


def _cparams(sem, flags=None):
    return pltpu.CompilerParams(dimension_semantics=sem, vmem_limit_bytes=VMEM_LIMIT_BYTES,
                                flags=flags)


def _rms_rows(xf, w_row):
    ms = jnp.mean(xf * xf, axis=-1, keepdims=True)
    return xf * lax.rsqrt(ms + EPS) * w_row


def _norm_rows_to(x_ref, nw_ref, hn_ref, row_chunk):
    def body(c, carry):
        r = pl.multiple_of(c * row_chunk, row_chunk)
        hn_ref[pl.ds(r, row_chunk), :] = _rms_rows(x_ref[pl.ds(r, row_chunk), :],
                                                   nw_ref[...]).astype(BF16)
        return carry
    lax.fori_loop(0, x_ref.shape[0] // row_chunk, body, 0)


def _in_proj_kernel(x_ref, nw_ref, w_ref, ws_ref, o_ref, hn_ref, os_ref, *, row_chunk, n_steps):
    j = pl.program_id(1)
    tm = x_ref.shape[0]
    pack = 2 * SUBLANES
    side_rows = -(-tm // (n_steps * pack)) * pack
    side_steps = -(-tm // side_rows)

    @pl.when(j == 0)
    def _():
        _norm_rows_to(x_ref, nw_ref, hn_ref, row_chunk)

    @pl.when(j < side_steps)
    def _():
        r = pl.multiple_of(jnp.minimum(j * side_rows, tm - side_rows), pack)
        os_ref[pl.ds(r, side_rows), :] = jnp.dot(hn_ref[pl.ds(r, side_rows), :], ws_ref[...],
                                                 preferred_element_type=F32)

    acc = jnp.dot(hn_ref[...], w_ref[...], preferred_element_type=F32)
    for c in range(o_ref.shape[0]):
        o_ref[c] = acc[:, c * LANES:(c + 1) * LANES].astype(o_ref.dtype)


def _in_proj(x2, nw, w_cat, layer, w_side, *, tm, tn):
    T, D = x2.shape
    N = _N_HEADMAJOR
    NS = w_side.shape[2]
    return pl.pallas_call(
        functools.partial(_in_proj_kernel, row_chunk=128, n_steps=N // tn),
        out_shape=(jax.ShapeDtypeStruct((N // LANES, T, LANES), BF16),
                   jax.ShapeDtypeStruct((T, D), BF16),
                   jax.ShapeDtypeStruct((T, NS), F32)),
        grid=(T // tm, N // tn),
        in_specs=[pl.BlockSpec((tm, D), lambda i, j: (i, 0)),
                  pl.BlockSpec((1, D), lambda i, j: (0, 0)),
                  pl.BlockSpec((None, D, tn), lambda i, j: (layer, 0, j)),
                  pl.BlockSpec((None, D, NS), lambda i, j: (layer, 0, 0))],
        out_specs=(pl.BlockSpec((tn // LANES, tm, LANES), lambda i, j: (j, i, 0)),
                   pl.BlockSpec((tm, D), lambda i, j: (i, 0)),
                   pl.BlockSpec((tm, NS), lambda i, j: (i, 0))),
        compiler_params=_cparams(("parallel", "arbitrary")),
        name="in_proj",
    )(x2, nw, w_cat, w_side)


def _matmul_kernel(h_ref, w_ref, o_ref):
    o_ref[...] = jnp.dot(h_ref[...], w_ref[...], preferred_element_type=F32).astype(o_ref.dtype)


def _matmul(hn, w_cat, layer, col0, n_cols, *, tm, tn, out_dtype, name):
    T, D = hn.shape
    assert col0 % tn == 0 and n_cols % tn == 0
    return pl.pallas_call(
        _matmul_kernel,
        out_shape=jax.ShapeDtypeStruct((T, n_cols), out_dtype),
        grid=(T // tm, n_cols // tn),
        in_specs=[pl.BlockSpec((tm, D), lambda i, j: (i, 0)),
                  pl.BlockSpec((None, D, tn), lambda i, j: (layer, 0, col0 // tn + j))],
        out_specs=pl.BlockSpec((tm, tn), lambda i, j: (i, j)),
        compiler_params=_cparams(("parallel", "parallel")),
        name=name,
    )(hn, w_cat)


def _cast_kernel(w_ref, o_ref, *, row_chunk):
    def body(c, carry):
        r = pl.multiple_of(c * row_chunk, row_chunk)
        o_ref[pl.ds(r, row_chunk), :] = w_ref[pl.ds(r, row_chunk), :].astype(BF16)
        return carry
    lax.fori_loop(0, w_ref.shape[0] // row_chunk, body, 0)


def _cast_bf16(w3, *, tr, tc):
    Lw, R, C = w3.shape
    return pl.pallas_call(
        functools.partial(_cast_kernel, row_chunk=128),
        out_shape=jax.ShapeDtypeStruct((Lw, R, C), BF16),
        grid=(Lw, R // tr, C // tc),
        in_specs=[pl.BlockSpec((None, tr, tc), lambda l, i, j: (l, i, j))],
        out_specs=pl.BlockSpec((None, tr, tc), lambda l, i, j: (l, i, j)),
        compiler_params=_cparams(("parallel", "parallel", "parallel")),
        name="cast_bf16",
    )(w3)


_PACK_TC = 1024
_PACK_SHIFT = 2 * MLSTM_HEADS
_PACK_ALIGNED_TILES = _OFF_MI // _PACK_TC


def _pack_w_in_kernel(a_ref, b_ref, s_ref, o_ref, g_ref):
    j = pl.program_id(2)
    tc, tr = a_ref.shape
    blk = LANES

    def emit(shift):
        for rb in range(tc // blk):
            lo, hi = rb * blk + shift, (rb + 1) * blk + shift
            for cb in range(tr // blk):
                cols = slice(cb * blk, (cb + 1) * blk)
                if hi <= tc:
                    x = a_ref[lo:hi, cols]
                else:
                    x = jnp.concatenate([a_ref[lo:tc, cols], b_ref[0:hi - tc, cols]], axis=0)
                y = x.T * s_ref[:, rb * blk:(rb + 1) * blk]
                o_ref[cols, rb * blk:(rb + 1) * blk] = y.astype(BF16)

    @pl.when(j < _PACK_ALIGNED_TILES)
    def _():
        emit(0)

    @pl.when(j >= _PACK_ALIGNED_TILES)
    def _():
        emit(_PACK_SHIFT)

    @pl.when(j == _PACK_ALIGNED_TILES)
    def _():
        zeros = jnp.zeros((blk - _PACK_SHIFT, blk), F32)
        for cb in range(tr // blk):
            cols = slice(cb * blk, (cb + 1) * blk)
            x = jnp.concatenate([a_ref[0:_PACK_SHIFT, cols], zeros], axis=0)
            g_ref[cols, :] = x.T.astype(BF16)


def _pack_w_in(w_in_t, scale_row, *, tr):
    depth, d_in, D = w_in_t.shape
    tc = _PACK_TC
    n_out = d_in - _PACK_SHIFT
    assert _OFF_MI % tc == 0 and n_out % tc == 0 and tc % LANES == 0 and _PACK_SHIFT == SUBLANES
    return pl.pallas_call(
        _pack_w_in_kernel,
        out_shape=(jax.ShapeDtypeStruct((depth, D, n_out), BF16),
                   jax.ShapeDtypeStruct((depth, D, LANES), BF16)),
        grid=(depth, D // tr, n_out // tc),
        in_specs=[pl.BlockSpec((None, tc, tr), lambda l, i, j: (l, j, i)),
                  pl.BlockSpec((None, SUBLANES, tr),
                               lambda l, i, j: (l, (j + 1) * (tc // SUBLANES), i)),
                  pl.BlockSpec((1, tc), lambda l, i, j: (0, j))],
        out_specs=(pl.BlockSpec((None, tr, tc), lambda l, i, j: (l, i, j)),
                   pl.BlockSpec((None, tr, LANES), lambda l, i, j: (l, i, 0))),
        compiler_params=_cparams(("parallel", "parallel", "arbitrary")),
        name="pack_w_in",
    )(w_in_t, w_in_t, scale_row)


def _retention_kernel(q_ref, k_ref, v_ref, g_ref, cos_ref, sin_ref, decay_ref, xi_ref, zeta_ref,
                      o_ref):
    L = RET_CHUNK
    dh = RET_HEAD_DIM
    HP, S = q_ref.shape[0], q_ref.shape[1]

    def rope(x, cos, sin_signed):
        return x * cos + pltpu.roll(x, dh // 2, 1) * sin_signed

    def state_free(hh, n):
        rows = slice(n * L, (n + 1) * L)
        cos = cos_ref[rows, :]
        sin = sin_ref[rows, :]
        q = rope(q_ref[hh, rows, :].astype(F32), cos, sin)
        k = rope(k_ref[hh, rows, :].astype(F32), cos, sin)
        v = v_ref[hh, rows, :]
        qb = q.astype(BF16)
        scores = lax.dot_general(qb, k.astype(BF16), (((1,), (1,)), ((), ())),
                                 preferred_element_type=F32) * decay_ref[hh]
        kv = jnp.dot((k * zeta_ref[hh]).T.astype(BF16), v, preferred_element_type=F32)
        return qb, scores.astype(BF16), v, kv

    def finish(hh, n, parts, r_prev):
        qb, scores, v, kv = parts
        rows = slice(n * L, (n + 1) * L)
        xi = xi_ref[hh]
        intra = jnp.dot(scores, v, preferred_element_type=F32)
        inter = jnp.dot(qb, r_prev.astype(BF16), preferred_element_type=F32) * xi
        o = intra + inter
        o = o * lax.rsqrt(jnp.mean(o * o, axis=-1, keepdims=True) + EPS)
        g = g_ref[hh, rows, :].astype(F32)
        o_ref[rows, hh * dh:(hh + 1) * dh] = (o * (g * jax.nn.sigmoid(g))).astype(o_ref.dtype)
        return r_prev * xi[L - 1:L, :] + kv

    n_chunks = S // L
    for hh in range(HP):
        r_prev = jnp.zeros((dh, dh), F32)
        pending = [state_free(hh, n) for n in range(min(RET_SKEW, n_chunks))]
        for n in range(n_chunks):
            if n + RET_SKEW < n_chunks:
                pending.append(state_free(hh, n + RET_SKEW))
            r_prev = finish(hh, n, pending.pop(0), r_prev)


def _retention(proj, cos_t, sin_t, decay, xi_b, zeta_b, B, S):
    H, dh, L, HP = RET_HEADS, RET_HEAD_DIM, RET_CHUNK, RET_HEADS_PER_STEP

    def pspec(base):
        return pl.BlockSpec((HP, S, LANES), lambda b, h, base=base: (base // HP + h, b, 0))

    def tspec(width):
        return pl.BlockSpec((HP, L, width), lambda b, h: (h, 0, 0))

    assert H % HP == 0 and all(blk % HP == 0 for blk in (_BLK_RQ, _BLK_RK, _BLK_RV, _BLK_RG))
    return pl.pallas_call(
        _retention_kernel,
        out_shape=jax.ShapeDtypeStruct((B * S, RET_W), BF16),
        grid=(B, H // HP),
        in_specs=[pspec(_BLK_RQ), pspec(_BLK_RK), pspec(_BLK_RV), pspec(_BLK_RG),
                  pl.BlockSpec((S, dh), lambda b, h: (0, 0)),
                  pl.BlockSpec((S, dh), lambda b, h: (0, 0)),
                  tspec(L), tspec(dh), tspec(dh)],
        out_specs=pl.BlockSpec((S, HP * dh), lambda b, h: (b, h)),
        compiler_params=_cparams(("parallel", "parallel")),
        name="retention",
    )(proj, proj, proj, proj, cos_t, sin_t, decay, xi_b, zeta_b)


def _mlstm_kernel(q_ref, k_ref, v_ref, og_ref, gp_ref, gb_ref, cwq_ref, cwk_ref, cbq_ref, cbk_ref,
                  nw_ref, o_ref, pad_ref, qc_ref, kc_ref, a_ref, bc_ref, at_ref, bt_ref):
    L = MLSTM_CHUNK
    S = q_ref.shape[0]
    NH = MLSTM_HEADS
    KC = MLSTM_CONV
    h = pl.program_id(1)
    n_chunks = S // L

    def conv_silu(src_ref, cw_ref, cb_ref, dst_ref, scale):
        pad_ref[0:SUBLANES, :] = jnp.zeros((SUBLANES, LANES), F32)
        pad_ref[SUBLANES:SUBLANES + S, :] = src_ref[...].astype(F32)
        cw = cw_ref[...]
        cb = cb_ref[...]
        for c in range(n_chunks):
            acc = cb
            for t in range(KC):
                r0 = SUBLANES + c * L - (KC - 1) + t
                acc = acc + pad_ref[r0:r0 + L, :] * cw[t:t + 1, :]
            y = acc * jax.nn.sigmoid(acc)
            dst_ref[c * L:(c + 1) * L, :] = y * scale

    conv_silu(q_ref, cwq_ref, cbq_ref, qc_ref, MLSTM_QK_DIM ** -0.5)
    conv_silu(k_ref, cwk_ref, cbk_ref, kc_ref, 1.0)

    lane = lax.broadcasted_iota(jnp.int32, (L, LANES), 1)
    sub = lax.broadcasted_iota(jnp.int32, (L, LANES), 0)
    sub_col = lax.broadcasted_iota(jnp.int32, (L, 1), 0)
    causal = sub >= lane
    nw = nw_ref[...]

    @pl.when(h == 0)
    def _():
        tri = (sub >= lane).astype(F32)
        is_f = (lane >= NH) & (lane < 2 * NH)
        for n in range(n_chunks):
            rows = slice(n * L, (n + 1) * L)
            gp = gp_ref[rows, :] + gb_ref[...]
            logsig = jnp.minimum(gp, 0.0) - jnp.log1p(jnp.exp(-jnp.abs(gp)))
            a = jnp.where(is_f, logsig, gp)
            bc = jnp.dot(tri, a, preferred_element_type=F32, precision=lax.Precision.HIGHEST)
            a_ref[rows, :] = a
            bc_ref[rows, :] = bc
            at_ref[n * SUBLANES:(n + 1) * SUBLANES, :] = a.T[0:SUBLANES, :]
            bt_ref[n * SUBLANES:(n + 1) * SUBLANES, :] = bc.T[0:SUBLANES, :]

    def state_free(n):
        rows = slice(n * L, (n + 1) * L)
        li_col = jnp.sum(jnp.where(lane == h, a_ref[rows, :], 0.0), axis=1, keepdims=True)
        b_col = jnp.sum(jnp.where(lane == h + NH, bc_ref[rows, :], 0.0), axis=1, keepdims=True)
        li_row = at_ref[pl.ds(n * SUBLANES + h, 1), :]
        b_row = bt_ref[pl.ds(n * SUBLANES + NH + h, 1), :]
        gtot = jnp.max(jnp.where(sub_col == L - 1, b_col, NEG_BIG), axis=0, keepdims=True)
        q = qc_ref[rows, :]
        k = kc_ref[rows, :]
        v = jnp.concatenate([v_ref[0, rows, :], v_ref[1, rows, :]], axis=1)
        qb = q.astype(BF16)
        dlog = jnp.where(causal, b_col - b_row + li_row, NEG_BIG)
        dmax = jnp.max(dlog, axis=1, keepdims=True)
        qk = lax.dot_general(qb, k.astype(BF16), (((1,), (1,)), ((), ())),
                             preferred_element_type=F32)
        return dict(li_col=li_col, b_col=b_col, gtot=gtot, q=q, k=k, qb=qb, v=v, dlog=dlog,
                    dmax=dmax, qk=qk)

    def finish(n, p, state):
        c_prev, n_prev, m_prev = state
        rows = slice(n * L, (n + 1) * L)
        inter_log = p["b_col"] + m_prev
        m_pos = jnp.maximum(inter_log, p["dmax"])
        s_intra = p["qk"] * jnp.exp(p["dlog"] - m_pos)
        inter_scale = jnp.exp(inter_log - m_pos)
        num = (jnp.dot(s_intra.astype(BF16), p["v"], preferred_element_type=F32)
               + inter_scale * jnp.dot(p["qb"], c_prev.astype(BF16), preferred_element_type=F32))
        den = (jnp.sum(s_intra, axis=1, keepdims=True)
               + inter_scale * jnp.sum(p["q"] * n_prev, axis=1, keepdims=True))
        inv = 1.0 / jnp.maximum(jnp.abs(den), jnp.exp(-m_pos))
        ms = jnp.mean(num * num, axis=-1, keepdims=True)
        row_scale = inv * lax.rsqrt(inv * inv * ms + EPS)
        og = jnp.concatenate([og_ref[0, rows, :], og_ref[1, rows, :]], axis=1).astype(F32)
        o_ref[rows, :] = (num * row_scale * nw * jax.nn.sigmoid(og)).astype(o_ref.dtype)
        w_loc = p["gtot"] - p["b_col"] + p["li_col"]
        m_loc = jnp.max(w_loc, axis=0, keepdims=True)
        ke = p["k"] * jnp.exp(w_loc - m_loc)
        c_loc = jnp.dot(ke.T.astype(BF16), p["v"], preferred_element_type=F32)
        n_loc = jnp.sum(ke, axis=0, keepdims=True)
        m_new = jnp.maximum(p["gtot"] + m_prev, m_loc)
        a_dec = jnp.exp(p["gtot"] + m_prev - m_new)
        b_inc = jnp.exp(m_loc - m_new)
        return (a_dec * c_prev + b_inc * c_loc, a_dec * n_prev + b_inc * n_loc, m_new)

    state = (jnp.zeros((MLSTM_QK_DIM, MLSTM_V_DIM), F32), jnp.zeros((1, MLSTM_QK_DIM), F32),
             jnp.zeros((1, 1), F32))
    for n in range(n_chunks):
        state = finish(n, state_free(n), state)


def _mlstm(proj, gpre, gate_b_row, conv_w, conv_b, norm_w, B, S):
    NH, dk, dv = MLSTM_HEADS, MLSTM_QK_DIM, MLSTM_V_DIM
    KC = MLSTM_CONV
    nvb = dv // LANES

    def pspec(base):
        return pl.BlockSpec((None, S, LANES), lambda b, h, base=base: (base + h, b, 0))

    def pspec2(base):
        return pl.BlockSpec((nvb, S, LANES), lambda b, h, base=base: (base // nvb + h, b, 0))

    return pl.pallas_call(
        _mlstm_kernel,
        out_shape=jax.ShapeDtypeStruct((B * S, MLSTM_V_W), BF16),
        grid=(B, NH),
        in_specs=[pspec(_BLK_MQ), pspec(_BLK_MK), pspec2(_BLK_MV), pspec2(_BLK_MO),
                  pl.BlockSpec((S, LANES), lambda b, h: (b, 0)),
                  pl.BlockSpec((1, LANES), lambda b, h: (0, 0)),
                  pl.BlockSpec((KC, dk), lambda b, h: (0, h)),
                  pl.BlockSpec((KC, dk), lambda b, h: (0, NH + h)),
                  pl.BlockSpec((1, dk), lambda b, h: (0, h)),
                  pl.BlockSpec((1, dk), lambda b, h: (0, NH + h)),
                  pl.BlockSpec((1, dv), lambda b, h: (0, h))],
        out_specs=pl.BlockSpec((S, dv), lambda b, h: (b, h)),
        scratch_shapes=[pltpu.VMEM((S + SUBLANES, dk), F32),
                        pltpu.VMEM((S, dk), F32),
                        pltpu.VMEM((S, dk), F32),
                        pltpu.VMEM((S, LANES), F32),
                        pltpu.VMEM((S, LANES), F32),
                        pltpu.VMEM((S // MLSTM_CHUNK * SUBLANES, MLSTM_CHUNK), F32),
                        pltpu.VMEM((S // MLSTM_CHUNK * SUBLANES, MLSTM_CHUNK), F32)],
        compiler_params=_cparams(("parallel", "arbitrary")),
        name="mlstm",
    )(proj, proj, proj, proj, gpre, gate_b_row, conv_w, conv_w, conv_b, conv_b, norm_w)


def _moba_kernel(q_ref, k_ref, v_ref, bvec_ref, far_ref, o_ref, kmean_ref, gate_ref, lg_ref,
                 bias_ref):
    BS = MOBA_BLOCK
    S = k_ref.shape[0]
    NB = S // BS
    nt = (((1,), (1,)), ((), ()))

    toep = pltpu.roll(jnp.broadcast_to(bvec_ref[...], (BS, 2 * BS)), 0, 1, stride=1, stride_axis=0)
    bias_ref[0] = toep[:, 0:BS]
    bias_ref[1] = toep[:, BS:2 * BS]

    for j in range(NB):
        kj = k_ref[j * BS:(j + 1) * BS, :].astype(F32)
        kmean_ref[j:j + 1, :] = jnp.mean(kj, axis=0, keepdims=True)
    ones_rows = jnp.ones((2 * SUBLANES, BS), BF16)
    vts = [jnp.concatenate([v_ref[j * BS:(j + 1) * BS, :].astype(F32).T.astype(BF16), ones_rows],
                           axis=0) for j in range(NB)]
    dh = MOBA_HEAD_DIM
    far = far_ref[...]
    kl = lax.broadcasted_iota(jnp.int32, (BS, BS), 0)
    ql = lax.broadcasted_iota(jnp.int32, (BS, BS), 1)
    causal = kl <= ql

    def select_rows(i, q):
        if i <= MOBA_TOPK:
            return [None] * i
        gate_ref[i % 2] = lax.dot_general(kmean_ref[...], q.astype(F32), nt,
                                          preferred_element_type=F32,
                                          precision=lax.Precision.HIGHEST)
        rows = [gate_ref[i % 2, n:n + 1, :] for n in range(i)]
        add_rows = []
        for n in range(i):
            rank = jnp.zeros_like(rows[n])
            for mth in range(i):
                if mth == n:
                    continue
                ahead = rows[mth] > rows[n]
                if mth < n:
                    ahead = ahead | (rows[mth] == rows[n])
                rank = rank + jnp.where(ahead, 1.0, 0.0)
            add_rows.append(jnp.where(rank < MOBA_TOPK, 0.0, NEG_BIG))
        return add_rows

    def logits_tile(i, j, q, add_rows):
        lg = lax.dot_general(k_ref[j * BS:(j + 1) * BS, :], q, nt, preferred_element_type=F32)
        if j == i:
            lg = jnp.where(causal, lg + bias_ref[0], NEG_BIG)
        elif j == i - 1:
            lg = lg + bias_ref[1]
            if add_rows[j] is not None:
                lg = lg + add_rows[j]
        else:
            lg = lg + (far if add_rows[j] is None else far + add_rows[j])
        lg_ref[i % 2, j] = lg
        return jnp.max(lg, axis=0, keepdims=True)

    def value_tile(i, j, m):
        p = jnp.exp2(lg_ref[i % 2, j] - m)
        return jnp.dot(vts[j], p.astype(BF16), preferred_element_type=F32)

    def merge(a, b, op):
        return b if a is None else op(a, b)

    m_cur = logits_tile(0, 0, q_ref[0:BS, :], [])
    for i in range(NB):
        m_next = None
        if i + 1 < NB:
            q_next = q_ref[(i + 1) * BS:(i + 2) * BS, :]
            rows_next = select_rows(i + 1, q_next)
        acc = None
        for j in range(i + 2):
            if i + 1 < NB:
                m_next = merge(m_next, logits_tile(i + 1, j, q_next, rows_next), jnp.maximum)
            if j <= i:
                acc = merge(acc, value_tile(i, j, m_cur), jnp.add)
        l = jnp.max(acc[dh:dh + SUBLANES, :], axis=0, keepdims=True)
        out_t = acc[0:dh, :] * (1.0 / l)
        o_ref[i * BS:(i + 1) * BS, :] = out_t.T.astype(o_ref.dtype)
        m_cur = m_next


def _moba(proj, bias_tiles, far_rows, B, S):
    H, dh, BS = MOBA_HEADS, MOBA_HEAD_DIM, MOBA_BLOCK
    NB = S // BS

    def pspec(base):
        return pl.BlockSpec((None, S, LANES), lambda b, h, base=base: (base + h, b, 0))

    return pl.pallas_call(
        _moba_kernel,
        out_shape=jax.ShapeDtypeStruct((B * S, MOBA_W), BF16),
        grid=(B, H),
        in_specs=[pspec(_BLK_BQ), pspec(_BLK_BK), pspec(_BLK_BV),
                  pl.BlockSpec((None, 1, 2 * BS), lambda b, h: (h, 0, 0)),
                  pl.BlockSpec((None, 1, BS), lambda b, h: (h, 0, 0))],
        out_specs=pl.BlockSpec((S, dh), lambda b, h: (b, h)),
        scratch_shapes=[pltpu.VMEM((NB, dh), F32),
                        pltpu.VMEM((2, NB, BS), F32),
                        pltpu.VMEM((2, NB, BS, BS), F32),
                        pltpu.VMEM((2, BS, BS), F32)],
        compiler_params=_cparams(("parallel", "parallel")),
        name="moba",
    )(proj, proj, proj, bias_tiles, far_rows)


def _merge_kernel(yr_ref, ym_ref, yb_ref, gr_ref, gm_ref, gb_ref, x_ref,
                  wr_ref, wm_ref, wb_ref, wo_ref, o_ref):
    def gate(g_ref):
        return jax.nn.sigmoid(g_ref[...].astype(F32))

    mixed = gate(gr_ref) * jnp.dot(yr_ref[...], wr_ref[...], preferred_element_type=F32)
    mixed = mixed + gate(gm_ref) * jnp.dot(ym_ref[...], wm_ref[...], preferred_element_type=F32)
    mixed = mixed + gate(gb_ref) * jnp.dot(yb_ref[...], wb_ref[...], preferred_element_type=F32)
    o_ref[...] = x_ref[...] + jnp.dot(mixed.astype(BF16), wo_ref[...], preferred_element_type=F32)


def _merge(y_ret, y_ml, y_mb, gate_pre, x2, w_r, w_m, w_b, w_o, layer, *, tm):
    T, D = x2.shape

    def resident(shape):
        return pl.BlockSpec((None,) + shape, lambda i: (layer, 0, 0), pipeline_mode=pl.Buffered(1))

    return pl.pallas_call(
        _merge_kernel,
        out_shape=jax.ShapeDtypeStruct((T, D), F32),
        grid=(T // tm,),
        in_specs=[pl.BlockSpec((tm, RET_W), lambda i: (i, 0)),
                  pl.BlockSpec((tm, MLSTM_V_W), lambda i: (i, 0)),
                  pl.BlockSpec((tm, MOBA_W), lambda i: (i, 0)),
                  pl.BlockSpec((tm, D), lambda i: (i, 0)),
                  pl.BlockSpec((tm, D), lambda i: (i, 1)),
                  pl.BlockSpec((tm, D), lambda i: (i, 2)),
                  pl.BlockSpec((tm, D), lambda i: (i, 0)),
                  resident((RET_W, D)), resident((MLSTM_V_W, D)), resident((MOBA_W, D)),
                  resident((D, D))],
        out_specs=pl.BlockSpec((tm, D), lambda i: (i, 0)),
        compiler_params=_cparams(("parallel",)),
        name="merge",
    )(y_ret, y_ml, y_mb, gate_pre, gate_pre, gate_pre, x2, w_r, w_m, w_b, w_o)


def _ffn_kernel(x_ref, xn_ref, nw_ref, fw_ref, w1_ref, w2_ref, o_ref, hn_ref, *, row_chunk,
                final_norm):
    i = pl.program_id(0)
    j = pl.program_id(1)
    rows_next = xn_ref.shape[0]
    slot = i % 2

    @pl.when((i == 0) & (j == 0))
    def _():
        def body(c, carry):
            r = pl.multiple_of(c * row_chunk, row_chunk)
            hn_ref[0, pl.ds(r, row_chunk), :] = _rms_rows(x_ref[pl.ds(r, row_chunk), :],
                                                          nw_ref[...]).astype(BF16)
            return carry
        lax.fori_loop(0, x_ref.shape[0] // row_chunk, body, 0)

    def step():
        u = jnp.dot(hn_ref[slot], w1_ref[...], preferred_element_type=F32)
        u = jnp.maximum(u, 0.0)
        y = jnp.dot((u * u).astype(BF16), w2_ref[...], preferred_element_type=F32)
        r = pl.multiple_of(j * rows_next, rows_next)
        hn_ref[1 - slot, pl.ds(r, rows_next), :] = _rms_rows(xn_ref[...], nw_ref[...]).astype(BF16)
        return y

    @pl.when(j == 0)
    def _():
        o_ref[...] = x_ref[...] + step()

    @pl.when(j > 0)
    def _():
        o_ref[...] += step()

    if final_norm:
        @pl.when(j == pl.num_programs(1) - 1)
        def _():
            def body(c, carry):
                r = pl.multiple_of(c * row_chunk, row_chunk)
                o_ref[pl.ds(r, row_chunk), :] = _rms_rows(o_ref[pl.ds(r, row_chunk), :], fw_ref[...])
                return carry
            lax.fori_loop(0, x_ref.shape[0] // row_chunk, body, 0)


def _ffn(x2, nw, fw, w1, w2, layer, *, tm, tf, final_norm):
    T, D = x2.shape
    F = w1.shape[2]
    n_i, n_j = T // tm, F // tf
    rows_next = tm // n_j
    assert rows_next * n_j == tm and rows_next % (2 * SUBLANES) == 0
    return pl.pallas_call(
        functools.partial(_ffn_kernel, row_chunk=128, final_norm=final_norm),
        out_shape=jax.ShapeDtypeStruct((T, D), F32),
        grid=(n_i, n_j),
        in_specs=[pl.BlockSpec((tm, D), lambda i, j: (i, 0)),
                  pl.BlockSpec((rows_next, D),
                               lambda i, j: (jnp.minimum(i + 1, n_i - 1) * n_j + j, 0)),
                  pl.BlockSpec((1, D), lambda i, j: (0, 0)),
                  pl.BlockSpec((1, D), lambda i, j: (0, 0)),
                  pl.BlockSpec((None, D, tf), lambda i, j: (layer, 0, j)),
                  pl.BlockSpec((None, tf, D), lambda i, j: (layer, j, 0))],
        out_specs=pl.BlockSpec((tm, D), lambda i, j: (i, 0)),
        scratch_shapes=[pltpu.VMEM((2, tm, D), BF16)],
        compiler_params=_cparams(("arbitrary", "arbitrary")),
        name="ffn",
    )(x2, x2, nw, fw, w1, w2)


def _rope_tables(S):
    half = RET_HEAD_DIM // 2
    inv = ROPE_BASE ** (-jnp.arange(half, dtype=F32) / half)
    ang = jnp.arange(S).astype(F32)[:, None] * inv[None, :]
    cos, sin = jnp.cos(ang), jnp.sin(ang)
    return jnp.concatenate([cos, cos], axis=-1), jnp.concatenate([-sin, sin], axis=-1)


def _retention_tables():
    H, L, dh = RET_HEADS, RET_CHUNK, RET_HEAD_DIM
    log_gamma = jnp.log1p(-jnp.exp2(-5.0 - jnp.arange(H, dtype=F32)))
    idx = jnp.arange(L, dtype=F32)
    diff = idx[:, None] - idx[None, :]
    decay = jnp.where(diff >= 0, jnp.exp(jnp.maximum(diff, 0.0) * log_gamma[:, None, None]), 0.0)
    zeta = jnp.exp((L - 1 - idx)[None, :] * log_gamma[:, None])
    xi = jnp.exp((idx + 1.0)[None, :] * log_gamma[:, None])
    return (decay, jnp.broadcast_to(xi[:, :, None], (H, L, dh)),
            jnp.broadcast_to(zeta[:, :, None], (H, L, dh)))


def _t5_bucket(dist):
    n = jnp.maximum(dist, 0)
    exact = REL_BUCKETS // 2
    nf = jnp.maximum(n, 1).astype(F32)
    large = exact + (jnp.log(nf / exact) / math.log(REL_MAX_DIST / exact)
                     * (REL_BUCKETS - exact)).astype(jnp.int32)
    large = jnp.minimum(large, REL_BUCKETS - 1)
    return jnp.where(n < exact, n, large)


def _moba_bias_tables(rel_bias, S):
    BS = MOBA_BLOCK
    assert BS + 1 >= REL_MAX_DIST and S % BS == 0
    table_t = rel_bias.T.astype(F32) * LOG2E
    onehot = (_t5_bucket(jnp.arange(2 * BS))[:, None] == jnp.arange(REL_BUCKETS)).astype(F32)
    by_dist = jnp.einsum("db,hb->hd", onehot, table_t, precision=lax.Precision.HIGHEST)
    far = table_t[:, REL_BUCKETS - 1]
    far_rows = jnp.broadcast_to(far[:, None, None], (far.shape[0], 1, BS))
    return by_dist[:, None, :], far_rows


def kernel(x, w_in, mlstm_gate_b, mlstm_conv_w, mlstm_conv_b, mlstm_norm_w, w_branch_ret,
           w_branch_mlstm, w_branch_moba, w_out, norm_mix_w, norm_mlp_w, w_ff1, w_ff2, rel_bias,
           final_norm_w):
    B, S, D = x.shape
    T = B * S
    depth = w_in.shape[0]
    x2 = x.reshape(T, D)

    cos_t, sin_t = _rope_tables(S)
    decay, xi_b, zeta_b = _retention_tables()
    bias_by_dist, far_rows = _moba_bias_tables(rel_bias, S)
    final_w = final_norm_w.reshape(1, D)

    scale_row = np.ones((1, w_in.shape[2] - _PACK_SHIFT), np.float32)
    scale_row[:, _OFF_RQ:_OFF_RQ + RET_W] = RET_HEAD_DIM ** -0.5
    scale_row[:, _OFF_MI:_OFF_MI + MOBA_W] = MOBA_HEAD_DIM ** -0.5 * LOG2E
    w_cat, w_gpre = _pack_w_in(jnp.swapaxes(w_in, 1, 2), jnp.asarray(scale_row), **_TILES["pack"])
    n_gate_cols = w_cat.shape[2] - _N_HEADMAJOR
    w_r, w_m, w_b, w_o, w_1, w_2 = [_cast_bf16(w, **_TILES["cast"]) for w in
                                    (w_branch_ret, w_branch_mlstm, w_branch_moba, w_out, w_ff1, w_ff2)]

    for l in range(depth):
        proj, hn, gpre = _in_proj(x2, norm_mix_w[l].reshape(1, D), w_cat, l, w_gpre,
                                  **_TILES["in_proj"])
        gate_pre = _matmul(hn, w_cat, l, _N_HEADMAJOR, n_gate_cols, out_dtype=BF16,
                           name="branch_gates", **_TILES["gates"])

        y_ret = _retention(proj, cos_t, sin_t, decay, xi_b, zeta_b, B, S)
        gate_b_row = jnp.pad(mlstm_gate_b[l].reshape(1, 2 * MLSTM_HEADS),
                             ((0, 0), (0, LANES - 2 * MLSTM_HEADS)))
        y_ml = _mlstm(proj, gpre, gate_b_row, mlstm_conv_w[l], mlstm_conv_b[l].reshape(1, -1),
                      mlstm_norm_w[l].reshape(1, -1), B, S)
        y_mb = _moba(proj, bias_by_dist, far_rows, B, S)

        x2 = _merge(y_ret, y_ml, y_mb, gate_pre, x2, w_r, w_m, w_b, w_o, l, **_TILES["merge"])
        x2 = _ffn(x2, norm_mlp_w[l].reshape(1, D), final_w, w_1, w_2, l,
                  final_norm=(l == depth - 1), **_TILES["ffn"])

    return x2.reshape(B, S, D)
```

```python
import functools
import math

import numpy as np
import jax
import jax.numpy as jnp
from jax import lax
from jax.experimental import pallas as pl
from jax.experimental.pallas import tpu as pltpu

D_MODEL = 2048
RET_HEADS = 8
RET_HEAD_DIM = 128
RET_CHUNK = 128
MLSTM_HEADS = 4
MLSTM_QK_DIM = 128
MLSTM_V_DIM = 256
MLSTM_CHUNK = 128
MLSTM_CONV = 4
MOBA_HEADS = 8
MOBA_HEAD_DIM = 128
MOBA_BLOCK = 256
MOBA_TOPK = 3
REL_BUCKETS = 32
REL_MAX_DIST = 128
D_FF = 4 * D_MODEL
ROPE_BASE = 10000.0
EPS = 1e-6

RET_W = RET_HEADS * RET_HEAD_DIM
MLSTM_QK_W = MLSTM_HEADS * MLSTM_QK_DIM
MLSTM_V_W = MLSTM_HEADS * MLSTM_V_DIM
MOBA_W = MOBA_HEADS * MOBA_HEAD_DIM

LANES = 128
SUBLANES = 8
VMEM_LIMIT_BYTES = 56 * 1024 * 1024

_OFF_RQ = 0
_OFF_MI = 4 * RET_W + 2 * MLSTM_QK_W + 2 * MLSTM_V_W
_OFF_BQ = _OFF_MI + 2 * MLSTM_HEADS
_OFF_GATES = _OFF_BQ + 3 * MOBA_W
_N_HEADMAJOR = _OFF_MI + 3 * MOBA_W
_BLK_RQ, _BLK_RK, _BLK_RV, _BLK_RG = 0, 8, 16, 24
_BLK_MQ, _BLK_MK, _BLK_MV, _BLK_MO = 32, 36, 40, 48
_BLK_BQ, _BLK_BK, _BLK_BV = 56, 64, 72

RET_HEADS_PER_STEP = 1
RET_SKEW = 1
NEG_BIG = -1e30
LOG2E = 1.0 / math.log(2.0)

_TILES = {
    "in_proj": dict(tm=1024, tn=1024),
    "gates": dict(tm=1024, tn=2048),
    "merge": dict(tm=256),
    "ffn": dict(tm=1024, tf=512),
    "pack": dict(tr=1024),
    "cast": dict(tr=1024, tc=2048),
}

BF16 = jnp.bfloat16
F32 = jnp.float32


def _cparams(sem, flags=None):
    return pltpu.CompilerParams(dimension_semantics=sem, vmem_limit_bytes=VMEM_LIMIT_BYTES,
                                flags=flags)


def _rms_rows(xf, w_row):
    ms = jnp.mean(xf * xf, axis=-1, keepdims=True)
    return xf * lax.rsqrt(ms + EPS) * w_row


def _norm_rows_to(x_ref, nw_ref, hn_ref, row_chunk):
    def body(c, carry):
        r = pl.multiple_of(c * row_chunk, row_chunk)
        hn_ref[pl.ds(r, row_chunk), :] = _rms_rows(x_ref[pl.ds(r, row_chunk), :],
                                                   nw_ref[...]).astype(BF16)
        return carry
    lax.fori_loop(0, x_ref.shape[0] // row_chunk, body, 0)


def _in_proj_kernel(x_ref, nw_ref, w_ref, ws_ref, o_ref, hn_ref, os_ref, *, row_chunk):
    @pl.when(pl.program_id(1) == 0)
    def _():
        _norm_rows_to(x_ref, nw_ref, hn_ref, row_chunk)
        os_ref[...] = jnp.dot(hn_ref[...], ws_ref[...], preferred_element_type=F32)

    acc = jnp.dot(hn_ref[...], w_ref[...], preferred_element_type=F32)
    for c in range(o_ref.shape[0]):
        o_ref[c] = acc[:, c * LANES:(c + 1) * LANES].astype(o_ref.dtype)


def _in_proj(x2, nw, w_cat, layer, w_side, *, tm, tn):
    T, D = x2.shape
    N = _N_HEADMAJOR
    NS = w_side.shape[2]
    return pl.pallas_call(
        functools.partial(_in_proj_kernel, row_chunk=128),
        out_shape=(jax.ShapeDtypeStruct((N // LANES, T, LANES), BF16),
                   jax.ShapeDtypeStruct((T, D), BF16),
                   jax.ShapeDtypeStruct((T, NS), F32)),
        grid=(T // tm, N // tn),
        in_specs=[pl.BlockSpec((tm, D), lambda i, j: (i, 0)),
                  pl.BlockSpec((1, D), lambda i, j: (0, 0)),
                  pl.BlockSpec((None, D, tn), lambda i, j: (layer, 0, j)),
                  pl.BlockSpec((None, D, NS), lambda i, j: (layer, 0, 0))],
        out_specs=(pl.BlockSpec((tn // LANES, tm, LANES), lambda i, j: (j, i, 0)),
                   pl.BlockSpec((tm, D), lambda i, j: (i, 0)),
                   pl.BlockSpec((tm, NS), lambda i, j: (i, 0))),
        compiler_params=_cparams(("parallel", "arbitrary")),
        name="in_proj",
    )(x2, nw, w_cat, w_side)


def _matmul_kernel(h_ref, w_ref, o_ref):
    o_ref[...] = jnp.dot(h_ref[...], w_ref[...], preferred_element_type=F32).astype(o_ref.dtype)


def _matmul(hn, w_cat, layer, col0, n_cols, *, tm, tn, out_dtype, name):
    T, D = hn.shape
    assert col0 % tn == 0 and n_cols % tn == 0
    return pl.pallas_call(
        _matmul_kernel,
        out_shape=jax.ShapeDtypeStruct((T, n_cols), out_dtype),
        grid=(T // tm, n_cols // tn),
        in_specs=[pl.BlockSpec((tm, D), lambda i, j: (i, 0)),
                  pl.BlockSpec((None, D, tn), lambda i, j: (layer, 0, col0 // tn + j))],
        out_specs=pl.BlockSpec((tm, tn), lambda i, j: (i, j)),
        compiler_params=_cparams(("parallel", "parallel")),
        name=name,
    )(hn, w_cat)


def _cast_kernel(w_ref, o_ref, *, row_chunk):
    def body(c, carry):
        r = pl.multiple_of(c * row_chunk, row_chunk)
        o_ref[pl.ds(r, row_chunk), :] = w_ref[pl.ds(r, row_chunk), :].astype(BF16)
        return carry
    lax.fori_loop(0, w_ref.shape[0] // row_chunk, body, 0)


def _cast_bf16(w3, *, tr, tc):
    Lw, R, C = w3.shape
    return pl.pallas_call(
        functools.partial(_cast_kernel, row_chunk=128),
        out_shape=jax.ShapeDtypeStruct((Lw, R, C), BF16),
        grid=(Lw, R // tr, C // tc),
        in_specs=[pl.BlockSpec((None, tr, tc), lambda l, i, j: (l, i, j))],
        out_specs=pl.BlockSpec((None, tr, tc), lambda l, i, j: (l, i, j)),
        compiler_params=_cparams(("parallel", "parallel", "parallel")),
        name="cast_bf16",
    )(w3)


_PACK_TC = 1024
_PACK_SHIFT = 2 * MLSTM_HEADS
_PACK_ALIGNED_TILES = _OFF_MI // _PACK_TC


def _pack_w_in_kernel(a_ref, b_ref, s_ref, o_ref, g_ref):
    j = pl.program_id(2)
    tc, tr = a_ref.shape
    blk = LANES

    def emit(shift):
        for rb in range(tc // blk):
            lo, hi = rb * blk + shift, (rb + 1) * blk + shift
            for cb in range(tr // blk):
                cols = slice(cb * blk, (cb + 1) * blk)
                if hi <= tc:
                    x = a_ref[lo:hi, cols]
                else:
                    x = jnp.concatenate([a_ref[lo:tc, cols], b_ref[0:hi - tc, cols]], axis=0)
                y = x.T * s_ref[:, rb * blk:(rb + 1) * blk]
                o_ref[cols, rb * blk:(rb + 1) * blk] = y.astype(BF16)

    @pl.when(j < _PACK_ALIGNED_TILES)
    def _():
        emit(0)

    @pl.when(j >= _PACK_ALIGNED_TILES)
    def _():
        emit(_PACK_SHIFT)

    @pl.when(j == _PACK_ALIGNED_TILES)
    def _():
        zeros = jnp.zeros((blk - _PACK_SHIFT, blk), F32)
        for cb in range(tr // blk):
            cols = slice(cb * blk, (cb + 1) * blk)
            x = jnp.concatenate([a_ref[0:_PACK_SHIFT, cols], zeros], axis=0)
            g_ref[cols, :] = x.T.astype(BF16)


def _pack_w_in(w_in_t, scale_row, *, tr):
    depth, d_in, D = w_in_t.shape
    tc = _PACK_TC
    n_out = d_in - _PACK_SHIFT
    assert _OFF_MI % tc == 0 and n_out % tc == 0 and tc % LANES == 0 and _PACK_SHIFT == SUBLANES
    return pl.pallas_call(
        _pack_w_in_kernel,
        out_shape=(jax.ShapeDtypeStruct((depth, D, n_out), BF16),
                   jax.ShapeDtypeStruct((depth, D, LANES), BF16)),
        grid=(depth, D // tr, n_out // tc),
        in_specs=[pl.BlockSpec((None, tc, tr), lambda l, i, j: (l, j, i)),
                  pl.BlockSpec((None, SUBLANES, tr),
                               lambda l, i, j: (l, (j + 1) * (tc // SUBLANES), i)),
                  pl.BlockSpec((1, tc), lambda l, i, j: (0, j))],
        out_specs=(pl.BlockSpec((None, tr, tc), lambda l, i, j: (l, i, j)),
                   pl.BlockSpec((None, tr, LANES), lambda l, i, j: (l, i, 0))),
        compiler_params=_cparams(("parallel", "parallel", "arbitrary")),
        name="pack_w_in",
    )(w_in_t, w_in_t, scale_row)


def _retention_kernel(q_ref, k_ref, v_ref, g_ref, cos_ref, sin_ref, decay_ref, xi_ref, zeta_ref,
                      o_ref):
    L = RET_CHUNK
    dh = RET_HEAD_DIM
    HP, S = q_ref.shape[0], q_ref.shape[1]

    def rope(x, cos, sin_signed):
        return x * cos + pltpu.roll(x, dh // 2, 1) * sin_signed

    def state_free(hh, n):
        rows = slice(n * L, (n + 1) * L)
        cos = cos_ref[rows, :]
        sin = sin_ref[rows, :]
        q = rope(q_ref[hh, rows, :].astype(F32), cos, sin)
        k = rope(k_ref[hh, rows, :].astype(F32), cos, sin)
        v = v_ref[hh, rows, :]
        qb = q.astype(BF16)
        scores = lax.dot_general(qb, k.astype(BF16), (((1,), (1,)), ((), ())),
                                 preferred_element_type=F32) * decay_ref[hh]
        kv = jnp.dot((k * zeta_ref[hh]).T.astype(BF16), v, preferred_element_type=F32)
        return qb, scores.astype(BF16), v, kv

    def finish(hh, n, parts, r_prev):
        qb, scores, v, kv = parts
        rows = slice(n * L, (n + 1) * L)
        xi = xi_ref[hh]
        intra = jnp.dot(scores, v, preferred_element_type=F32)
        inter = jnp.dot(qb, r_prev.astype(BF16), preferred_element_type=F32) * xi
        o = intra + inter
        o = o * lax.rsqrt(jnp.mean(o * o, axis=-1, keepdims=True) + EPS)
        g = g_ref[hh, rows, :].astype(F32)
        o_ref[rows, hh * dh:(hh + 1) * dh] = (o * (g * jax.nn.sigmoid(g))).astype(o_ref.dtype)
        return r_prev * xi[L - 1:L, :] + kv

    n_chunks = S // L
    for hh in range(HP):
        r_prev = jnp.zeros((dh, dh), F32)
        pending = [state_free(hh, n) for n in range(min(RET_SKEW, n_chunks))]
        for n in range(n_chunks):
            if n + RET_SKEW < n_chunks:
                pending.append(state_free(hh, n + RET_SKEW))
            r_prev = finish(hh, n, pending.pop(0), r_prev)


def _retention(proj, cos_t, sin_t, decay, xi_b, zeta_b, B, S):
    H, dh, L, HP = RET_HEADS, RET_HEAD_DIM, RET_CHUNK, RET_HEADS_PER_STEP

    def pspec(base):
        return pl.BlockSpec((HP, S, LANES), lambda b, h, base=base: (base // HP + h, b, 0))

    def tspec(width):
        return pl.BlockSpec((HP, L, width), lambda b, h: (h, 0, 0))

    assert H % HP == 0 and all(blk % HP == 0 for blk in (_BLK_RQ, _BLK_RK, _BLK_RV, _BLK_RG))
    return pl.pallas_call(
        _retention_kernel,
        out_shape=jax.ShapeDtypeStruct((B * S, RET_W), BF16),
        grid=(B, H // HP),
        in_specs=[pspec(_BLK_RQ), pspec(_BLK_RK), pspec(_BLK_RV), pspec(_BLK_RG),
                  pl.BlockSpec((S, dh), lambda b, h: (0, 0)),
                  pl.BlockSpec((S, dh), lambda b, h: (0, 0)),
                  tspec(L), tspec(dh), tspec(dh)],
        out_specs=pl.BlockSpec((S, HP * dh), lambda b, h: (b, h)),
        compiler_params=_cparams(("parallel", "parallel")),
        name="retention",
    )(proj, proj, proj, proj, cos_t, sin_t, decay, xi_b, zeta_b)


def _mlstm_kernel(q_ref, k_ref, v_ref, og_ref, gp_ref, gb_ref, cwq_ref, cwk_ref, cbq_ref, cbk_ref,
                  nw_ref, o_ref, pad_ref, qc_ref, kc_ref, a_ref, bc_ref, at_ref, bt_ref):
    L = MLSTM_CHUNK
    S = q_ref.shape[0]
    NH = MLSTM_HEADS
    KC = MLSTM_CONV
    h = pl.program_id(1)
    n_chunks = S // L

    def conv_silu(src_ref, cw_ref, cb_ref, dst_ref, scale):
        pad_ref[0:SUBLANES, :] = jnp.zeros((SUBLANES, LANES), F32)
        pad_ref[SUBLANES:SUBLANES + S, :] = src_ref[...].astype(F32)
        cw = cw_ref[...]
        cb = cb_ref[...]
        for c in range(n_chunks):
            acc = cb
            for t in range(KC):
                r0 = SUBLANES + c * L - (KC - 1) + t
                acc = acc + pad_ref[r0:r0 + L, :] * cw[t:t + 1, :]
            y = acc * jax.nn.sigmoid(acc)
            dst_ref[c * L:(c + 1) * L, :] = y * scale

    conv_silu(q_ref, cwq_ref, cbq_ref, qc_ref, MLSTM_QK_DIM ** -0.5)
    conv_silu(k_ref, cwk_ref, cbk_ref, kc_ref, 1.0)

    lane = lax.broadcasted_iota(jnp.int32, (L, LANES), 1)
    sub = lax.broadcasted_iota(jnp.int32, (L, LANES), 0)
    sub_col = lax.broadcasted_iota(jnp.int32, (L, 1), 0)
    causal = sub >= lane
    nw = nw_ref[...]

    @pl.when(h == 0)
    def _():
        tri = (sub >= lane).astype(F32)
        is_f = (lane >= NH) & (lane < 2 * NH)
        for n in range(n_chunks):
            rows = slice(n * L, (n + 1) * L)
            gp = gp_ref[rows, :] + gb_ref[...]
            logsig = jnp.minimum(gp, 0.0) - jnp.log1p(jnp.exp(-jnp.abs(gp)))
            a = jnp.where(is_f, logsig, gp)
            bc = jnp.dot(tri, a, preferred_element_type=F32, precision=lax.Precision.HIGHEST)
            a_ref[rows, :] = a
            bc_ref[rows, :] = bc
            at_ref[n * SUBLANES:(n + 1) * SUBLANES, :] = a.T[0:SUBLANES, :]
            bt_ref[n * SUBLANES:(n + 1) * SUBLANES, :] = bc.T[0:SUBLANES, :]

    def state_free(n):
        rows = slice(n * L, (n + 1) * L)
        li_col = jnp.sum(jnp.where(lane == h, a_ref[rows, :], 0.0), axis=1, keepdims=True)
        b_col = jnp.sum(jnp.where(lane == h + NH, bc_ref[rows, :], 0.0), axis=1, keepdims=True)
        li_row = at_ref[pl.ds(n * SUBLANES + h, 1), :]
        b_row = bt_ref[pl.ds(n * SUBLANES + NH + h, 1), :]
        gtot = jnp.max(jnp.where(sub_col == L - 1, b_col, NEG_BIG), axis=0, keepdims=True)
        q = qc_ref[rows, :]
        k = kc_ref[rows, :]
        v = jnp.concatenate([v_ref[0, rows, :], v_ref[1, rows, :]], axis=1)
        qb = q.astype(BF16)
        dlog = jnp.where(causal, b_col - b_row + li_row, NEG_BIG)
        dmax = jnp.max(dlog, axis=1, keepdims=True)
        qk = lax.dot_general(qb, k.astype(BF16), (((1,), (1,)), ((), ())),
                             preferred_element_type=F32)
        return dict(li_col=li_col, b_col=b_col, gtot=gtot, q=q, k=k, qb=qb, v=v, dlog=dlog,
                    dmax=dmax, qk=qk)

    def finish(n, p, state):
        c_prev, n_prev, m_prev = state
        rows = slice(n * L, (n + 1) * L)
        inter_log = p["b_col"] + m_prev
        m_pos = jnp.maximum(inter_log, p["dmax"])
        s_intra = p["qk"] * jnp.exp(p["dlog"] - m_pos)
        inter_scale = jnp.exp(inter_log - m_pos)
        num = (jnp.dot(s_intra.astype(BF16), p["v"], preferred_element_type=F32)
               + inter_scale * jnp.dot(p["qb"], c_prev.astype(BF16), preferred_element_type=F32))
        den = (jnp.sum(s_intra, axis=1, keepdims=True)
               + inter_scale * jnp.sum(p["q"] * n_prev, axis=1, keepdims=True))
        inv = 1.0 / jnp.maximum(jnp.abs(den), jnp.exp(-m_pos))
        ms = jnp.mean(num * num, axis=-1, keepdims=True)
        row_scale = inv * lax.rsqrt(inv * inv * ms + EPS)
        og = jnp.concatenate([og_ref[0, rows, :], og_ref[1, rows, :]], axis=1).astype(F32)
        o_ref[rows, :] = (num * row_scale * nw * jax.nn.sigmoid(og)).astype(o_ref.dtype)
        w_loc = p["gtot"] - p["b_col"] + p["li_col"]
        m_loc = jnp.max(w_loc, axis=0, keepdims=True)
        ke = p["k"] * jnp.exp(w_loc - m_loc)
        c_loc = jnp.dot(ke.T.astype(BF16), p["v"], preferred_element_type=F32)
        n_loc = jnp.sum(ke, axis=0, keepdims=True)
        m_new = jnp.maximum(p["gtot"] + m_prev, m_loc)
        a_dec = jnp.exp(p["gtot"] + m_prev - m_new)
        b_inc = jnp.exp(m_loc - m_new)
        return (a_dec * c_prev + b_inc * c_loc, a_dec * n_prev + b_inc * n_loc, m_new)

    state = (jnp.zeros((MLSTM_QK_DIM, MLSTM_V_DIM), F32), jnp.zeros((1, MLSTM_QK_DIM), F32),
             jnp.zeros((1, 1), F32))
    for n in range(n_chunks):
        state = finish(n, state_free(n), state)


def _mlstm(proj, gpre, gate_b_row, conv_w, conv_b, norm_w, B, S):
    NH, dk, dv = MLSTM_HEADS, MLSTM_QK_DIM, MLSTM_V_DIM
    KC = MLSTM_CONV
    nvb = dv // LANES

    def pspec(base):
        return pl.BlockSpec((None, S, LANES), lambda b, h, base=base: (base + h, b, 0))

    def pspec2(base):
        return pl.BlockSpec((nvb, S, LANES), lambda b, h, base=base: (base // nvb + h, b, 0))

    return pl.pallas_call(
        _mlstm_kernel,
        out_shape=jax.ShapeDtypeStruct((B * S, MLSTM_V_W), BF16),
        grid=(B, NH),
        in_specs=[pspec(_BLK_MQ), pspec(_BLK_MK), pspec2(_BLK_MV), pspec2(_BLK_MO),
                  pl.BlockSpec((S, LANES), lambda b, h: (b, 0)),
                  pl.BlockSpec((1, LANES), lambda b, h: (0, 0)),
                  pl.BlockSpec((KC, dk), lambda b, h: (0, h)),
                  pl.BlockSpec((KC, dk), lambda b, h: (0, NH + h)),
                  pl.BlockSpec((1, dk), lambda b, h: (0, h)),
                  pl.BlockSpec((1, dk), lambda b, h: (0, NH + h)),
                  pl.BlockSpec((1, dv), lambda b, h: (0, h))],
        out_specs=pl.BlockSpec((S, dv), lambda b, h: (b, h)),
        scratch_shapes=[pltpu.VMEM((S + SUBLANES, dk), F32),
                        pltpu.VMEM((S, dk), F32),
                        pltpu.VMEM((S, dk), F32),
                        pltpu.VMEM((S, LANES), F32),
                        pltpu.VMEM((S, LANES), F32),
                        pltpu.VMEM((S // MLSTM_CHUNK * SUBLANES, MLSTM_CHUNK), F32),
                        pltpu.VMEM((S // MLSTM_CHUNK * SUBLANES, MLSTM_CHUNK), F32)],
        compiler_params=_cparams(("parallel", "arbitrary")),
        name="mlstm",
    )(proj, proj, proj, proj, gpre, gate_b_row, conv_w, conv_w, conv_b, conv_b, norm_w)


def _moba_kernel(q_ref, k_ref, v_ref, bvec_ref, far_ref, o_ref, kmean_ref, gate_ref, lg_ref,
                 bias_ref):
    BS = MOBA_BLOCK
    S = k_ref.shape[0]
    NB = S // BS
    nt = (((1,), (1,)), ((), ()))

    toep = pltpu.roll(jnp.broadcast_to(bvec_ref[...], (BS, 2 * BS)), 0, 1, stride=1, stride_axis=0)
    bias_ref[0] = toep[:, 0:BS]
    bias_ref[1] = toep[:, BS:2 * BS]

    for j in range(NB):
        kj = k_ref[j * BS:(j + 1) * BS, :].astype(F32)
        kmean_ref[j:j + 1, :] = jnp.mean(kj, axis=0, keepdims=True)
    ones_rows = jnp.ones((2 * SUBLANES, BS), BF16)
    vts = [jnp.concatenate([v_ref[j * BS:(j + 1) * BS, :].astype(F32).T.astype(BF16), ones_rows],
                           axis=0) for j in range(NB)]
    dh = MOBA_HEAD_DIM
    far = far_ref[...]
    kl = lax.broadcasted_iota(jnp.int32, (BS, BS), 0)
    ql = lax.broadcasted_iota(jnp.int32, (BS, BS), 1)
    causal = kl <= ql

    def select_rows(i, q):
        if i <= MOBA_TOPK:
            return [None] * i
        gate_ref[i % 2] = lax.dot_general(kmean_ref[...], q.astype(F32), nt,
                                          preferred_element_type=F32,
                                          precision=lax.Precision.HIGHEST)
        rows = [gate_ref[i % 2, n:n + 1, :] for n in range(i)]
        add_rows = []
        for n in range(i):
            rank = jnp.zeros_like(rows[n])
            for mth in range(i):
                if mth == n:
                    continue
                ahead = rows[mth] > rows[n]
                if mth < n:
                    ahead = ahead | (rows[mth] == rows[n])
                rank = rank + jnp.where(ahead, 1.0, 0.0)
            add_rows.append(jnp.where(rank < MOBA_TOPK, 0.0, NEG_BIG))
        return add_rows

    def logits_tile(i, j, q, add_rows):
        lg = lax.dot_general(k_ref[j * BS:(j + 1) * BS, :], q, nt, preferred_element_type=F32)
        if j == i:
            lg = jnp.where(causal, lg + bias_ref[0], NEG_BIG)
        elif j == i - 1:
            lg = lg + bias_ref[1]
            if add_rows[j] is not None:
                lg = lg + add_rows[j]
        else:
            lg = lg + (far if add_rows[j] is None else far + add_rows[j])
        lg_ref[i % 2, j] = lg
        return jnp.max(lg, axis=0, keepdims=True)

    def value_tile(i, j, m):
        p = jnp.exp2(lg_ref[i % 2, j] - m)
        return jnp.dot(vts[j], p.astype(BF16), preferred_element_type=F32)

    def merge(a, b, op):
        return b if a is None else op(a, b)

    m_cur = logits_tile(0, 0, q_ref[0:BS, :], [])
    for i in range(NB):
        m_next = None
        if i + 1 < NB:
            q_next = q_ref[(i + 1) * BS:(i + 2) * BS, :]
            rows_next = select_rows(i + 1, q_next)
        acc = None
        for j in range(i + 2):
            if i + 1 < NB:
                m_next = merge(m_next, logits_tile(i + 1, j, q_next, rows_next), jnp.maximum)
            if j <= i:
                acc = merge(acc, value_tile(i, j, m_cur), jnp.add)
        l = jnp.max(acc[dh:dh + SUBLANES, :], axis=0, keepdims=True)
        out_t = acc[0:dh, :] * (1.0 / l)
        o_ref[i * BS:(i + 1) * BS, :] = out_t.T.astype(o_ref.dtype)
        m_cur = m_next


def _moba(proj, bias_tiles, far_rows, B, S):
    H, dh, BS = MOBA_HEADS, MOBA_HEAD_DIM, MOBA_BLOCK
    NB = S // BS

    def pspec(base):
        return pl.BlockSpec((None, S, LANES), lambda b, h, base=base: (base + h, b, 0))

    return pl.pallas_call(
        _moba_kernel,
        out_shape=jax.ShapeDtypeStruct((B * S, MOBA_W), BF16),
        grid=(B, H),
        in_specs=[pspec(_BLK_BQ), pspec(_BLK_BK), pspec(_BLK_BV),
                  pl.BlockSpec((None, 1, 2 * BS), lambda b, h: (h, 0, 0)),
                  pl.BlockSpec((None, 1, BS), lambda b, h: (h, 0, 0))],
        out_specs=pl.BlockSpec((S, dh), lambda b, h: (b, h)),
        scratch_shapes=[pltpu.VMEM((NB, dh), F32),
                        pltpu.VMEM((2, NB, BS), F32),
                        pltpu.VMEM((2, NB, BS, BS), F32),
                        pltpu.VMEM((2, BS, BS), F32)],
        compiler_params=_cparams(("parallel", "parallel")),
        name="moba",
    )(proj, proj, proj, bias_tiles, far_rows)


def _merge_kernel(yr_ref, ym_ref, yb_ref, gr_ref, gm_ref, gb_ref, x_ref,
                  wr_ref, wm_ref, wb_ref, wo_ref, o_ref):
    def gate(g_ref):
        return jax.nn.sigmoid(g_ref[...].astype(F32))

    mixed = gate(gr_ref) * jnp.dot(yr_ref[...], wr_ref[...], preferred_element_type=F32)
    mixed = mixed + gate(gm_ref) * jnp.dot(ym_ref[...], wm_ref[...], preferred_element_type=F32)
    mixed = mixed + gate(gb_ref) * jnp.dot(yb_ref[...], wb_ref[...], preferred_element_type=F32)
    o_ref[...] = x_ref[...] + jnp.dot(mixed.astype(BF16), wo_ref[...], preferred_element_type=F32)


def _merge(y_ret, y_ml, y_mb, gate_pre, x2, w_r, w_m, w_b, w_o, layer, *, tm):
    T, D = x2.shape

    def resident(shape):
        return pl.BlockSpec((None,) + shape, lambda i: (layer, 0, 0), pipeline_mode=pl.Buffered(1))

    return pl.pallas_call(
        _merge_kernel,
        out_shape=jax.ShapeDtypeStruct((T, D), F32),
        grid=(T // tm,),
        in_specs=[pl.BlockSpec((tm, RET_W), lambda i: (i, 0)),
                  pl.BlockSpec((tm, MLSTM_V_W), lambda i: (i, 0)),
                  pl.BlockSpec((tm, MOBA_W), lambda i: (i, 0)),
                  pl.BlockSpec((tm, D), lambda i: (i, 0)),
                  pl.BlockSpec((tm, D), lambda i: (i, 1)),
                  pl.BlockSpec((tm, D), lambda i: (i, 2)),
                  pl.BlockSpec((tm, D), lambda i: (i, 0)),
                  resident((RET_W, D)), resident((MLSTM_V_W, D)), resident((MOBA_W, D)),
                  resident((D, D))],
        out_specs=pl.BlockSpec((tm, D), lambda i: (i, 0)),
        compiler_params=_cparams(("parallel",)),
        name="merge",
    )(y_ret, y_ml, y_mb, gate_pre, gate_pre, gate_pre, x2, w_r, w_m, w_b, w_o)


def _ffn_kernel(x_ref, xn_ref, nw_ref, fw_ref, w1_ref, w2_ref, o_ref, hn_ref, *, row_chunk,
                final_norm):
    i = pl.program_id(0)
    j = pl.program_id(1)
    rows_next = xn_ref.shape[0]
    slot = i % 2

    @pl.when((i == 0) & (j == 0))
    def _():
        def body(c, carry):
            r = pl.multiple_of(c * row_chunk, row_chunk)
            hn_ref[0, pl.ds(r, row_chunk), :] = _rms_rows(x_ref[pl.ds(r, row_chunk), :],
                                                          nw_ref[...]).astype(BF16)
            return carry
        lax.fori_loop(0, x_ref.shape[0] // row_chunk, body, 0)

    def step():
        u = jnp.dot(hn_ref[slot], w1_ref[...], preferred_element_type=F32)
        u = jnp.maximum(u, 0.0)
        y = jnp.dot((u * u).astype(BF16), w2_ref[...], preferred_element_type=F32)
        r = pl.multiple_of(j * rows_next, rows_next)
        hn_ref[1 - slot, pl.ds(r, rows_next), :] = _rms_rows(xn_ref[...], nw_ref[...]).astype(BF16)
        return y

    @pl.when(j == 0)
    def _():
        o_ref[...] = x_ref[...] + step()

    @pl.when(j > 0)
    def _():
        o_ref[...] += step()

    if final_norm:
        @pl.when(j == pl.num_programs(1) - 1)
        def _():
            def body(c, carry):
                r = pl.multiple_of(c * row_chunk, row_chunk)
                o_ref[pl.ds(r, row_chunk), :] = _rms_rows(o_ref[pl.ds(r, row_chunk), :], fw_ref[...])
                return carry
            lax.fori_loop(0, x_ref.shape[0] // row_chunk, body, 0)


def _ffn(x2, nw, fw, w1, w2, layer, *, tm, tf, final_norm):
    T, D = x2.shape
    F = w1.shape[2]
    n_i, n_j = T // tm, F // tf
    rows_next = tm // n_j
    assert rows_next * n_j == tm and rows_next % (2 * SUBLANES) == 0
    return pl.pallas_call(
        functools.partial(_ffn_kernel, row_chunk=128, final_norm=final_norm),
        out_shape=jax.ShapeDtypeStruct((T, D), F32),
        grid=(n_i, n_j),
        in_specs=[pl.BlockSpec((tm, D), lambda i, j: (i, 0)),
                  pl.BlockSpec((rows_next, D),
                               lambda i, j: (jnp.minimum(i + 1, n_i - 1) * n_j + j, 0)),
                  pl.BlockSpec((1, D), lambda i, j: (0, 0)),
                  pl.BlockSpec((1, D), lambda i, j: (0, 0)),
                  pl.BlockSpec((None, D, tf), lambda i, j: (layer, 0, j)),
                  pl.BlockSpec((None, tf, D), lambda i, j: (layer, j, 0))],
        out_specs=pl.BlockSpec((tm, D), lambda i, j: (i, 0)),
        scratch_shapes=[pltpu.VMEM((2, tm, D), BF16)],
        compiler_params=_cparams(("arbitrary", "arbitrary")),
        name="ffn",
    )(x2, x2, nw, fw, w1, w2)


def _rope_tables(S):
    half = RET_HEAD_DIM // 2
    inv = ROPE_BASE ** (-jnp.arange(half, dtype=F32) / half)
    ang = jnp.arange(S).astype(F32)[:, None] * inv[None, :]
    cos, sin = jnp.cos(ang), jnp.sin(ang)
    return jnp.concatenate([cos, cos], axis=-1), jnp.concatenate([-sin, sin], axis=-1)


def _retention_tables():
    H, L, dh = RET_HEADS, RET_CHUNK, RET_HEAD_DIM
    log_gamma = jnp.log1p(-jnp.exp2(-5.0 - jnp.arange(H, dtype=F32)))
    idx = jnp.arange(L, dtype=F32)
    diff = idx[:, None] - idx[None, :]
    decay = jnp.where(diff >= 0, jnp.exp(jnp.maximum(diff, 0.0) * log_gamma[:, None, None]), 0.0)
    zeta = jnp.exp((L - 1 - idx)[None, :] * log_gamma[:, None])
    xi = jnp.exp((idx + 1.0)[None, :] * log_gamma[:, None])
    return (decay, jnp.broadcast_to(xi[:, :, None], (H, L, dh)),
            jnp.broadcast_to(zeta[:, :, None], (H, L, dh)))


def _t5_bucket(dist):
    n = jnp.maximum(dist, 0)
    exact = REL_BUCKETS // 2
    nf = jnp.maximum(n, 1).astype(F32)
    large = exact + (jnp.log(nf / exact) / math.log(REL_MAX_DIST / exact)
                     * (REL_BUCKETS - exact)).astype(jnp.int32)
    large = jnp.minimum(large, REL_BUCKETS - 1)
    return jnp.where(n < exact, n, large)


def _moba_bias_tables(rel_bias, S):
    BS = MOBA_BLOCK
    assert BS + 1 >= REL_MAX_DIST and S % BS == 0
    table_t = rel_bias.T.astype(F32) * LOG2E
    onehot = (_t5_bucket(jnp.arange(2 * BS))[:, None] == jnp.arange(REL_BUCKETS)).astype(F32)
    by_dist = jnp.einsum("db,hb->hd", onehot, table_t, precision=lax.Precision.HIGHEST)
    far = table_t[:, REL_BUCKETS - 1]
    far_rows = jnp.broadcast_to(far[:, None, None], (far.shape[0], 1, BS))
    return by_dist[:, None, :], far_rows


def kernel(x, w_in, mlstm_gate_b, mlstm_conv_w, mlstm_conv_b, mlstm_norm_w, w_branch_ret,
           w_branch_mlstm, w_branch_moba, w_out, norm_mix_w, norm_mlp_w, w_ff1, w_ff2, rel_bias,
           final_norm_w):
    B, S, D = x.shape
    T = B * S
    depth = w_in.shape[0]
    x2 = x.reshape(T, D)

    cos_t, sin_t = _rope_tables(S)
    decay, xi_b, zeta_b = _retention_tables()
    bias_by_dist, far_rows = _moba_bias_tables(rel_bias, S)
    final_w = final_norm_w.reshape(1, D)

    scale_row = np.ones((1, w_in.shape[2] - _PACK_SHIFT), np.float32)
    scale_row[:, _OFF_RQ:_OFF_RQ + RET_W] = RET_HEAD_DIM ** -0.5
    scale_row[:, _OFF_MI:_OFF_MI + MOBA_W] = MOBA_HEAD_DIM ** -0.5 * LOG2E
    w_cat, w_gpre = _pack_w_in(jnp.swapaxes(w_in, 1, 2), jnp.asarray(scale_row), **_TILES["pack"])
    n_gate_cols = w_cat.shape[2] - _N_HEADMAJOR
    w_r, w_m, w_b, w_o, w_1, w_2 = [_cast_bf16(w, **_TILES["cast"]) for w in
                                    (w_branch_ret, w_branch_mlstm, w_branch_moba, w_out, w_ff1, w_ff2)]

    for l in range(depth):
        proj, hn, gpre = _in_proj(x2, norm_mix_w[l].reshape(1, D), w_cat, l, w_gpre,
                                  **_TILES["in_proj"])
        gate_pre = _matmul(hn, w_cat, l, _N_HEADMAJOR, n_gate_cols, out_dtype=BF16,
                           name="branch_gates", **_TILES["gates"])

        y_ret = _retention(proj, cos_t, sin_t, decay, xi_b, zeta_b, B, S)
        gate_b_row = jnp.pad(mlstm_gate_b[l].reshape(1, 2 * MLSTM_HEADS),
                             ((0, 0), (0, LANES - 2 * MLSTM_HEADS)))
        y_ml = _mlstm(proj, gpre, gate_b_row, mlstm_conv_w[l], mlstm_conv_b[l].reshape(1, -1),
                      mlstm_norm_w[l].reshape(1, -1), B, S)
        y_mb = _moba(proj, bias_by_dist, far_rows, B, S)

        x2 = _merge(y_ret, y_ml, y_mb, gate_pre, x2, w_r, w_m, w_b, w_o, l, **_TILES["merge"])
        x2 = _ffn(x2, norm_mlp_w[l].reshape(1, D), final_w, w_1, w_2, l,
                  final_norm=(l == depth - 1), **_TILES["ffn"])

    return x2.reshape(B, S, D)
```

```python
import functools
import math

import numpy as np
import jax
import jax.numpy as jnp
from jax import lax
from jax.experimental import pallas as pl
from jax.experimental.pallas import tpu as pltpu

D_MODEL = 2048
RET_HEADS = 8
RET_HEAD_DIM = 128
RET_CHUNK = 128
MLSTM_HEADS = 4
MLSTM_QK_DIM = 128
MLSTM_V_DIM = 256
MLSTM_CHUNK = 128
MLSTM_CONV = 4
MOBA_HEADS = 8
MOBA_HEAD_DIM = 128
MOBA_BLOCK = 256
MOBA_TOPK = 3
REL_BUCKETS = 32
REL_MAX_DIST = 128
D_FF = 4 * D_MODEL
ROPE_BASE = 10000.0
EPS = 1e-6

RET_W = RET_HEADS * RET_HEAD_DIM
MLSTM_QK_W = MLSTM_HEADS * MLSTM_QK_DIM
MLSTM_V_W = MLSTM_HEADS * MLSTM_V_DIM
MOBA_W = MOBA_HEADS * MOBA_HEAD_DIM

LANES = 128
SUBLANES = 8
VMEM_LIMIT_BYTES = 56 * 1024 * 1024

_OFF_RQ = 0
_OFF_MI = 4 * RET_W + 2 * MLSTM_QK_W + 2 * MLSTM_V_W
_OFF_BQ = _OFF_MI + 2 * MLSTM_HEADS
_OFF_GATES = _OFF_BQ + 3 * MOBA_W
_N_HEADMAJOR = _OFF_MI + 3 * MOBA_W
_BLK_RQ, _BLK_RK, _BLK_RV, _BLK_RG = 0, 8, 16, 24
_BLK_MQ, _BLK_MK, _BLK_MV, _BLK_MO = 32, 36, 40, 48
_BLK_BQ, _BLK_BK, _BLK_BV = 56, 64, 72

MOBA_LOGITS_LEAD = 2
RET_HEADS_PER_STEP = 1
RET_SKEW = 1
NEG_BIG = -1e30
LOG2E = 1.0 / math.log(2.0)

_TILES = {
    "in_proj": dict(tm=1024, tn=1024),
    "gates": dict(tm=1024, tn=2048),
    "merge": dict(tm=256),
    "ffn": dict(tm=512, tf=1024),
    "pack": dict(tr=1024),
    "cast": dict(tr=1024, tc=2048),
}

BF16 = jnp.bfloat16
F32 = jnp.float32


def _cparams(sem, flags=None):
    return pltpu.CompilerParams(dimension_semantics=sem, vmem_limit_bytes=VMEM_LIMIT_BYTES,
                                flags=flags)


def _rms_rows(xf, w_row):
    ms = jnp.mean(xf * xf, axis=-1, keepdims=True)
    return xf * lax.rsqrt(ms + EPS) * w_row


def _norm_rows_to(x_ref, nw_ref, hn_ref, row_chunk):
    def body(c, carry):
        r = pl.multiple_of(c * row_chunk, row_chunk)
        hn_ref[pl.ds(r, row_chunk), :] = _rms_rows(x_ref[pl.ds(r, row_chunk), :],
                                                   nw_ref[...]).astype(BF16)
        return carry
    lax.fori_loop(0, x_ref.shape[0] // row_chunk, body, 0)


def _in_proj_kernel(x_ref, nw_ref, w_ref, ws_ref, o_ref, hn_ref, os_ref, *, row_chunk):
    @pl.when(pl.program_id(1) == 0)
    def _():
        _norm_rows_to(x_ref, nw_ref, hn_ref, row_chunk)
        os_ref[...] = jnp.dot(hn_ref[...], ws_ref[...], preferred_element_type=F32)

    acc = jnp.dot(hn_ref[...], w_ref[...], preferred_element_type=F32)
    for c in range(o_ref.shape[0]):
        o_ref[c] = acc[:, c * LANES:(c + 1) * LANES].astype(o_ref.dtype)


def _in_proj(x2, nw, w_cat, layer, w_side, *, tm, tn):
    T, D = x2.shape
    N = _N_HEADMAJOR
    NS = w_side.shape[2]
    return pl.pallas_call(
        functools.partial(_in_proj_kernel, row_chunk=128),
        out_shape=(jax.ShapeDtypeStruct((N // LANES, T, LANES), BF16),
                   jax.ShapeDtypeStruct((T, D), BF16),
                   jax.ShapeDtypeStruct((T, NS), F32)),
        grid=(T // tm, N // tn),
        in_specs=[pl.BlockSpec((tm, D), lambda i, j: (i, 0)),
                  pl.BlockSpec((1, D), lambda i, j: (0, 0)),
                  pl.BlockSpec((None, D, tn), lambda i, j: (layer, 0, j)),
                  pl.BlockSpec((None, D, NS), lambda i, j: (layer, 0, 0))],
        out_specs=(pl.BlockSpec((tn // LANES, tm, LANES), lambda i, j: (j, i, 0)),
                   pl.BlockSpec((tm, D), lambda i, j: (i, 0)),
                   pl.BlockSpec((tm, NS), lambda i, j: (i, 0))),
        compiler_params=_cparams(("parallel", "arbitrary")),
        name="in_proj",
    )(x2, nw, w_cat, w_side)


def _matmul_kernel(h_ref, w_ref, o_ref):
    o_ref[...] = jnp.dot(h_ref[...], w_ref[...], preferred_element_type=F32).astype(o_ref.dtype)


def _matmul(hn, w_cat, layer, col0, n_cols, *, tm, tn, out_dtype, name):
    T, D = hn.shape
    assert col0 % tn == 0 and n_cols % tn == 0
    return pl.pallas_call(
        _matmul_kernel,
        out_shape=jax.ShapeDtypeStruct((T, n_cols), out_dtype),
        grid=(T // tm, n_cols // tn),
        in_specs=[pl.BlockSpec((tm, D), lambda i, j: (i, 0)),
                  pl.BlockSpec((None, D, tn), lambda i, j: (layer, 0, col0 // tn + j))],
        out_specs=pl.BlockSpec((tm, tn), lambda i, j: (i, j)),
        compiler_params=_cparams(("parallel", "parallel")),
        name=name,
    )(hn, w_cat)


def _cast_kernel(w_ref, o_ref, *, row_chunk):
    def body(c, carry):
        r = pl.multiple_of(c * row_chunk, row_chunk)
        o_ref[pl.ds(r, row_chunk), :] = w_ref[pl.ds(r, row_chunk), :].astype(BF16)
        return carry
    lax.fori_loop(0, w_ref.shape[0] // row_chunk, body, 0)


def _cast_bf16(w3, *, tr, tc):
    Lw, R, C = w3.shape
    return pl.pallas_call(
        functools.partial(_cast_kernel, row_chunk=128),
        out_shape=jax.ShapeDtypeStruct((Lw, R, C), BF16),
        grid=(Lw, R // tr, C // tc),
        in_specs=[pl.BlockSpec((None, tr, tc), lambda l, i, j: (l, i, j))],
        out_specs=pl.BlockSpec((None, tr, tc), lambda l, i, j: (l, i, j)),
        compiler_params=_cparams(("parallel", "parallel", "parallel")),
        name="cast_bf16",
    )(w3)


_PACK_TC = 1024
_PACK_SHIFT = 2 * MLSTM_HEADS
_PACK_ALIGNED_TILES = _OFF_MI // _PACK_TC


def _pack_w_in_kernel(a_ref, b_ref, s_ref, o_ref, g_ref):
    j = pl.program_id(2)
    tc, tr = a_ref.shape
    blk = LANES

    def emit(shift):
        for rb in range(tc // blk):
            lo, hi = rb * blk + shift, (rb + 1) * blk + shift
            for cb in range(tr // blk):
                cols = slice(cb * blk, (cb + 1) * blk)
                if hi <= tc:
                    x = a_ref[lo:hi, cols]
                else:
                    x = jnp.concatenate([a_ref[lo:tc, cols], b_ref[0:hi - tc, cols]], axis=0)
                y = x.T * s_ref[:, rb * blk:(rb + 1) * blk]
                o_ref[cols, rb * blk:(rb + 1) * blk] = y.astype(BF16)

    @pl.when(j < _PACK_ALIGNED_TILES)
    def _():
        emit(0)

    @pl.when(j >= _PACK_ALIGNED_TILES)
    def _():
        emit(_PACK_SHIFT)

    @pl.when(j == _PACK_ALIGNED_TILES)
    def _():
        zeros = jnp.zeros((blk - _PACK_SHIFT, blk), F32)
        for cb in range(tr // blk):
            cols = slice(cb * blk, (cb + 1) * blk)
            x = jnp.concatenate([a_ref[0:_PACK_SHIFT, cols], zeros], axis=0)
            g_ref[cols, :] = x.T.astype(BF16)


def _pack_w_in(w_in_t, scale_row, *, tr):
    depth, d_in, D = w_in_t.shape
    tc = _PACK_TC
    n_out = d_in - _PACK_SHIFT
    assert _OFF_MI % tc == 0 and n_out % tc == 0 and tc % LANES == 0 and _PACK_SHIFT == SUBLANES
    return pl.pallas_call(
        _pack_w_in_kernel,
        out_shape=(jax.ShapeDtypeStruct((depth, D, n_out), BF16),
                   jax.ShapeDtypeStruct((depth, D, LANES), BF16)),
        grid=(depth, D // tr, n_out // tc),
        in_specs=[pl.BlockSpec((None, tc, tr), lambda l, i, j: (l, j, i)),
                  pl.BlockSpec((None, SUBLANES, tr),
                               lambda l, i, j: (l, (j + 1) * (tc // SUBLANES), i)),
                  pl.BlockSpec((1, tc), lambda l, i, j: (0, j))],
        out_specs=(pl.BlockSpec((None, tr, tc), lambda l, i, j: (l, i, j)),
                   pl.BlockSpec((None, tr, LANES), lambda l, i, j: (l, i, 0))),
        compiler_params=_cparams(("parallel", "parallel", "arbitrary")),
        name="pack_w_in",
    )(w_in_t, w_in_t, scale_row)


def _retention_kernel(q_ref, k_ref, v_ref, g_ref, cos_ref, sin_ref, decay_ref, xi_ref, zeta_ref,
                      o_ref):
    L = RET_CHUNK
    dh = RET_HEAD_DIM
    HP, S = q_ref.shape[0], q_ref.shape[1]

    def rope(x, cos, sin_signed):
        return x * cos + pltpu.roll(x, dh // 2, 1) * sin_signed

    def state_free(hh, n):
        rows = slice(n * L, (n + 1) * L)
        cos = cos_ref[rows, :]
        sin = sin_ref[rows, :]
        q = rope(q_ref[hh, rows, :].astype(F32), cos, sin)
        k = rope(k_ref[hh, rows, :].astype(F32), cos, sin)
        v = v_ref[hh, rows, :]
        qb = q.astype(BF16)
        scores = lax.dot_general(qb, k.astype(BF16), (((1,), (1,)), ((), ())),
                                 preferred_element_type=F32) * decay_ref[hh]
        kv = jnp.dot((k * zeta_ref[hh]).T.astype(BF16), v, preferred_element_type=F32)
        return qb, scores.astype(BF16), v, kv

    def finish(hh, n, parts, r_prev):
        qb, scores, v, kv = parts
        rows = slice(n * L, (n + 1) * L)
        xi = xi_ref[hh]
        intra = jnp.dot(scores, v, preferred_element_type=F32)
        inter = jnp.dot(qb, r_prev.astype(BF16), preferred_element_type=F32) * xi
        o = intra + inter
        o = o * lax.rsqrt(jnp.mean(o * o, axis=-1, keepdims=True) + EPS)
        g = g_ref[hh, rows, :].astype(F32)
        o_ref[rows, hh * dh:(hh + 1) * dh] = (o * (g * jax.nn.sigmoid(g))).astype(o_ref.dtype)
        return r_prev * xi[L - 1:L, :] + kv

    n_chunks = S // L
    for hh in range(HP):
        r_prev = jnp.zeros((dh, dh), F32)
        pending = [state_free(hh, n) for n in range(min(RET_SKEW, n_chunks))]
        for n in range(n_chunks):
            if n + RET_SKEW < n_chunks:
                pending.append(state_free(hh, n + RET_SKEW))
            r_prev = finish(hh, n, pending.pop(0), r_prev)


def _retention(proj, cos_t, sin_t, decay, xi_b, zeta_b, B, S):
    H, dh, L, HP = RET_HEADS, RET_HEAD_DIM, RET_CHUNK, RET_HEADS_PER_STEP

    def pspec(base):
        return pl.BlockSpec((HP, S, LANES), lambda b, h, base=base: (base // HP + h, b, 0))

    def tspec(width):
        return pl.BlockSpec((HP, L, width), lambda b, h: (h, 0, 0))

    assert H % HP == 0 and all(blk % HP == 0 for blk in (_BLK_RQ, _BLK_RK, _BLK_RV, _BLK_RG))
    return pl.pallas_call(
        _retention_kernel,
        out_shape=jax.ShapeDtypeStruct((B * S, RET_W), BF16),
        grid=(B, H // HP),
        in_specs=[pspec(_BLK_RQ), pspec(_BLK_RK), pspec(_BLK_RV), pspec(_BLK_RG),
                  pl.BlockSpec((S, dh), lambda b, h: (0, 0)),
                  pl.BlockSpec((S, dh), lambda b, h: (0, 0)),
                  tspec(L), tspec(dh), tspec(dh)],
        out_specs=pl.BlockSpec((S, HP * dh), lambda b, h: (b, h)),
        compiler_params=_cparams(("parallel", "parallel")),
        name="retention",
    )(proj, proj, proj, proj, cos_t, sin_t, decay, xi_b, zeta_b)


def _mlstm_kernel(q_ref, k_ref, v_ref, og_ref, gp_ref, gb_ref, cwq_ref, cwk_ref, cbq_ref, cbk_ref,
                  nw_ref, o_ref, qpad_ref, kpad_ref, a_ref, bc_ref, at_ref, bt_ref):
    L = MLSTM_CHUNK
    S = q_ref.shape[0]
    NH = MLSTM_HEADS
    KC = MLSTM_CONV
    h = pl.program_id(1)
    n_chunks = S // L

    qpad_ref[0:SUBLANES, :] = jnp.zeros((SUBLANES, LANES), F32)
    kpad_ref[0:SUBLANES, :] = jnp.zeros((SUBLANES, LANES), F32)

    def conv_silu(src_ref, pad_ref, cw_ref, cb_ref, c, scale):
        pad_ref[SUBLANES + c * L:SUBLANES + (c + 1) * L, :] = src_ref[c * L:(c + 1) * L, :].astype(F32)
        cw = cw_ref[...]
        acc = cb_ref[...]
        for t in range(KC):
            r0 = SUBLANES + c * L - (KC - 1) + t
            acc = acc + pad_ref[r0:r0 + L, :] * cw[t:t + 1, :]
        return acc * jax.nn.sigmoid(acc) * scale

    lane = lax.broadcasted_iota(jnp.int32, (L, LANES), 1)
    sub = lax.broadcasted_iota(jnp.int32, (L, LANES), 0)
    sub_col = lax.broadcasted_iota(jnp.int32, (L, 1), 0)
    causal = sub >= lane
    nw = nw_ref[...]

    @pl.when(h == 0)
    def _():
        tri = (sub >= lane).astype(F32)
        is_f = (lane >= NH) & (lane < 2 * NH)
        for n in range(n_chunks):
            rows = slice(n * L, (n + 1) * L)
            gp = gp_ref[rows, :] + gb_ref[...]
            logsig = jnp.minimum(gp, 0.0) - jnp.log1p(jnp.exp(-jnp.abs(gp)))
            a = jnp.where(is_f, logsig, gp)
            bc = jnp.dot(tri, a, preferred_element_type=F32, precision=lax.Precision.HIGHEST)
            a_ref[rows, :] = a
            bc_ref[rows, :] = bc
            at_ref[n * SUBLANES:(n + 1) * SUBLANES, :] = a.T[0:SUBLANES, :]
            bt_ref[n * SUBLANES:(n + 1) * SUBLANES, :] = bc.T[0:SUBLANES, :]

    def state_free(n):
        rows = slice(n * L, (n + 1) * L)
        li_col = jnp.sum(jnp.where(lane == h, a_ref[rows, :], 0.0), axis=1, keepdims=True)
        b_col = jnp.sum(jnp.where(lane == h + NH, bc_ref[rows, :], 0.0), axis=1, keepdims=True)
        li_row = at_ref[pl.ds(n * SUBLANES + h, 1), :]
        b_row = bt_ref[pl.ds(n * SUBLANES + NH + h, 1), :]
        gtot = jnp.max(jnp.where(sub_col == L - 1, b_col, NEG_BIG), axis=0, keepdims=True)
        q = conv_silu(q_ref, qpad_ref, cwq_ref, cbq_ref, n, MLSTM_QK_DIM ** -0.5)
        k = conv_silu(k_ref, kpad_ref, cwk_ref, cbk_ref, n, 1.0)
        v = jnp.concatenate([v_ref[0, rows, :], v_ref[1, rows, :]], axis=1)
        qb = q.astype(BF16)
        dlog = jnp.where(causal, b_col - b_row + li_row, NEG_BIG)
        dmax = jnp.max(dlog, axis=1, keepdims=True)
        qk = lax.dot_general(qb, k.astype(BF16), (((1,), (1,)), ((), ())),
                             preferred_element_type=F32)
        return dict(li_col=li_col, b_col=b_col, gtot=gtot, q=q, k=k, qb=qb, v=v, dlog=dlog,
                    dmax=dmax, qk=qk)

    def finish(n, p, state):
        c_prev, n_prev, m_prev = state
        rows = slice(n * L, (n + 1) * L)
        inter_log = p["b_col"] + m_prev
        m_pos = jnp.maximum(inter_log, p["dmax"])
        s_intra = p["qk"] * jnp.exp(p["dlog"] - m_pos)
        inter_scale = jnp.exp(inter_log - m_pos)
        num = (jnp.dot(s_intra.astype(BF16), p["v"], preferred_element_type=F32)
               + inter_scale * jnp.dot(p["qb"], c_prev.astype(BF16), preferred_element_type=F32))
        den = (jnp.sum(s_intra, axis=1, keepdims=True)
               + inter_scale * jnp.sum(p["q"] * n_prev, axis=1, keepdims=True))
        inv = 1.0 / jnp.maximum(jnp.abs(den), jnp.exp(-m_pos))
        ms = jnp.mean(num * num, axis=-1, keepdims=True)
        row_scale = inv * lax.rsqrt(inv * inv * ms + EPS)
        og = jnp.concatenate([og_ref[0, rows, :], og_ref[1, rows, :]], axis=1).astype(F32)
        o_ref[rows, :] = (num * row_scale * nw * jax.nn.sigmoid(og)).astype(o_ref.dtype)
        w_loc = p["gtot"] - p["b_col"] + p["li_col"]
        m_loc = jnp.max(w_loc, axis=0, keepdims=True)
        ke = p["k"] * jnp.exp(w_loc - m_loc)
        c_loc = jnp.dot(ke.T.astype(BF16), p["v"], preferred_element_type=F32)
        n_loc = jnp.sum(ke, axis=0, keepdims=True)
        m_new = jnp.maximum(p["gtot"] + m_prev, m_loc)
        a_dec = jnp.exp(p["gtot"] + m_prev - m_new)
        b_inc = jnp.exp(m_loc - m_new)
        return (a_dec * c_prev + b_inc * c_loc, a_dec * n_prev + b_inc * n_loc, m_new)

    state = (jnp.zeros((MLSTM_QK_DIM, MLSTM_V_DIM), F32), jnp.zeros((1, MLSTM_QK_DIM), F32),
             jnp.zeros((1, 1), F32))
    for n in range(n_chunks):
        state = finish(n, state_free(n), state)


def _mlstm(proj, gpre, gate_b_row, conv_w, conv_b, norm_w, B, S):
    NH, dk, dv = MLSTM_HEADS, MLSTM_QK_DIM, MLSTM_V_DIM
    KC = MLSTM_CONV
    nvb = dv // LANES

    def pspec(base):
        return pl.BlockSpec((None, S, LANES), lambda b, h, base=base: (base + h, b, 0))

    def pspec2(base):
        return pl.BlockSpec((nvb, S, LANES), lambda b, h, base=base: (base // nvb + h, b, 0))

    return pl.pallas_call(
        _mlstm_kernel,
        out_shape=jax.ShapeDtypeStruct((B * S, MLSTM_V_W), BF16),
        grid=(B, NH),
        in_specs=[pspec(_BLK_MQ), pspec(_BLK_MK), pspec2(_BLK_MV), pspec2(_BLK_MO),
                  pl.BlockSpec((S, LANES), lambda b, h: (b, 0)),
                  pl.BlockSpec((1, LANES), lambda b, h: (0, 0)),
                  pl.BlockSpec((KC, dk), lambda b, h: (0, h)),
                  pl.BlockSpec((KC, dk), lambda b, h: (0, NH + h)),
                  pl.BlockSpec((1, dk), lambda b, h: (0, h)),
                  pl.BlockSpec((1, dk), lambda b, h: (0, NH + h)),
                  pl.BlockSpec((1, dv), lambda b, h: (0, h))],
        out_specs=pl.BlockSpec((S, dv), lambda b, h: (b, h)),
        scratch_shapes=[pltpu.VMEM((S + SUBLANES, dk), F32),
                        pltpu.VMEM((S + SUBLANES, dk), F32),
                        pltpu.VMEM((S, LANES), F32),
                        pltpu.VMEM((S, LANES), F32),
                        pltpu.VMEM((S // MLSTM_CHUNK * SUBLANES, MLSTM_CHUNK), F32),
                        pltpu.VMEM((S // MLSTM_CHUNK * SUBLANES, MLSTM_CHUNK), F32)],
        compiler_params=_cparams(("parallel", "arbitrary")),
        name="mlstm",
    )(proj, proj, proj, proj, gpre, gate_b_row, conv_w, conv_w, conv_b, conv_b, norm_w)


def _moba_kernel(q_ref, k_ref, v_ref, bvec_ref, far_ref, o_ref, kmean_ref, gate_ref, lg_ref,
                 bias_ref):
    BS = MOBA_BLOCK
    S = k_ref.shape[0]
    NB = S // BS
    nt = (((1,), (1,)), ((), ()))

    toep = pltpu.roll(jnp.broadcast_to(bvec_ref[...], (BS, 2 * BS)), 0, 1, stride=1, stride_axis=0)
    bias_ref[0] = toep[:, 0:BS]
    bias_ref[1] = toep[:, BS:2 * BS]

    kmean_ref[...] = jnp.zeros_like(kmean_ref)
    ones_rows = jnp.ones((2 * SUBLANES, BS), BF16)
    vts = []

    def prepare_key_block(j):
        kj = k_ref[j * BS:(j + 1) * BS, :].astype(F32)
        kmean_ref[j:j + 1, :] = jnp.mean(kj, axis=0, keepdims=True)
        vts.append(jnp.concatenate([v_ref[j * BS:(j + 1) * BS, :].astype(F32).T.astype(BF16),
                                    ones_rows], axis=0))

    dh = MOBA_HEAD_DIM
    far = far_ref[...]
    kl = lax.broadcasted_iota(jnp.int32, (BS, BS), 0)
    ql = lax.broadcasted_iota(jnp.int32, (BS, BS), 1)
    causal = kl <= ql

    def select_rows(i, q):
        if i <= MOBA_TOPK:
            return [None] * i
        gate_ref[i % 2] = lax.dot_general(kmean_ref[...], q.astype(F32), nt,
                                          preferred_element_type=F32,
                                          precision=lax.Precision.HIGHEST)
        rows = [gate_ref[i % 2, n:n + 1, :] for n in range(i)]
        add_rows = []
        for n in range(i):
            rank = jnp.zeros_like(rows[n])
            for mth in range(i):
                if mth == n:
                    continue
                ahead = rows[mth] > rows[n]
                if mth < n:
                    ahead = ahead | (rows[mth] == rows[n])
                rank = rank + jnp.where(ahead, 1.0, 0.0)
            add_rows.append(jnp.where(rank < MOBA_TOPK, 0.0, NEG_BIG))
        return add_rows

    def logits_tile(i, j, q, add_rows):
        lg = lax.dot_general(k_ref[j * BS:(j + 1) * BS, :], q, nt, preferred_element_type=F32)
        if j == i:
            lg = jnp.where(causal, lg + bias_ref[0], NEG_BIG)
        elif j == i - 1:
            lg = lg + bias_ref[1]
            if add_rows[j] is not None:
                lg = lg + add_rows[j]
        else:
            lg = lg + (far if add_rows[j] is None else far + add_rows[j])
        lg_ref[i % 2, j] = lg
        return jnp.max(lg, axis=0, keepdims=True)

    def value_tile(i, j, m):
        p = jnp.exp2(lg_ref[i % 2, j] - m)
        return jnp.dot(vts[j], p.astype(BF16), preferred_element_type=F32)

    def merge(a, b, op):
        return b if a is None else op(a, b)

    m_cur = logits_tile(0, 0, q_ref[0:BS, :], [])
    for i in range(NB):
        prepare_key_block(i)
        m_next = None
        if i + 1 < NB:
            q_next = q_ref[(i + 1) * BS:(i + 2) * BS, :]
            rows_next = select_rows(i + 1, q_next)
        acc = None
        lead = MOBA_LOGITS_LEAD
        if i + 1 < NB:
            for j in range(min(lead, i + 2)):
                m_next = merge(m_next, logits_tile(i + 1, j, q_next, rows_next), jnp.maximum)
        for j in range(i + 1):
            if i + 1 < NB and j + lead < i + 2:
                m_next = jnp.maximum(m_next, logits_tile(i + 1, j + lead, q_next, rows_next))
            acc = merge(acc, value_tile(i, j, m_cur), jnp.add)
        l = jnp.max(acc[dh:dh + SUBLANES, :], axis=0, keepdims=True)
        out_t = acc[0:dh, :] * (1.0 / l)
        o_ref[i * BS:(i + 1) * BS, :] = out_t.T.astype(o_ref.dtype)
        m_cur = m_next


def _moba(proj, bias_tiles, far_rows, B, S):
    H, dh, BS = MOBA_HEADS, MOBA_HEAD_DIM, MOBA_BLOCK
    NB = S // BS

    def pspec(base):
        return pl.BlockSpec((None, S, LANES), lambda b, h, base=base: (base + h, b, 0))

    return pl.pallas_call(
        _moba_kernel,
        out_shape=jax.ShapeDtypeStruct((B * S, MOBA_W), BF16),
        grid=(B, H),
        in_specs=[pspec(_BLK_BQ), pspec(_BLK_BK), pspec(_BLK_BV),
                  pl.BlockSpec((None, 1, 2 * BS), lambda b, h: (h, 0, 0)),
                  pl.BlockSpec((None, 1, BS), lambda b, h: (h, 0, 0))],
        out_specs=pl.BlockSpec((S, dh), lambda b, h: (b, h)),
        scratch_shapes=[pltpu.VMEM((NB, dh), F32),
                        pltpu.VMEM((2, NB, BS), F32),
                        pltpu.VMEM((2, NB, BS, BS), F32),
                        pltpu.VMEM((2, BS, BS), F32)],
        compiler_params=_cparams(("parallel", "parallel")),
        name="moba",
    )(proj, proj, proj, bias_tiles, far_rows)


def _merge_kernel(yr_ref, ym_ref, yb_ref, gr_ref, gm_ref, gb_ref, x_ref,
                  wr_ref, wm_ref, wb_ref, wo_ref, o_ref):
    def gate(g_ref):
        return jax.nn.sigmoid(g_ref[...].astype(F32))

    mixed = gate(gr_ref) * jnp.dot(yr_ref[...], wr_ref[...], preferred_element_type=F32)
    mixed = mixed + gate(gm_ref) * jnp.dot(ym_ref[...], wm_ref[...], preferred_element_type=F32)
    mixed = mixed + gate(gb_ref) * jnp.dot(yb_ref[...], wb_ref[...], preferred_element_type=F32)
    o_ref[...] = x_ref[...] + jnp.dot(mixed.astype(BF16), wo_ref[...], preferred_element_type=F32)


def _merge(y_ret, y_ml, y_mb, gate_pre, x2, w_r, w_m, w_b, w_o, layer, *, tm):
    T, D = x2.shape

    def resident(shape):
        return pl.BlockSpec((None,) + shape, lambda i: (layer, 0, 0), pipeline_mode=pl.Buffered(1))

    return pl.pallas_call(
        _merge_kernel,
        out_shape=jax.ShapeDtypeStruct((T, D), F32),
        grid=(T // tm,),
        in_specs=[pl.BlockSpec((tm, RET_W), lambda i: (i, 0)),
                  pl.BlockSpec((tm, MLSTM_V_W), lambda i: (i, 0)),
                  pl.BlockSpec((tm, MOBA_W), lambda i: (i, 0)),
                  pl.BlockSpec((tm, D), lambda i: (i, 0)),
                  pl.BlockSpec((tm, D), lambda i: (i, 1)),
                  pl.BlockSpec((tm, D), lambda i: (i, 2)),
                  pl.BlockSpec((tm, D), lambda i: (i, 0)),
                  resident((RET_W, D)), resident((MLSTM_V_W, D)), resident((MOBA_W, D)),
                  resident((D, D))],
        out_specs=pl.BlockSpec((tm, D), lambda i: (i, 0)),
        compiler_params=_cparams(("parallel",)),
        name="merge",
    )(y_ret, y_ml, y_mb, gate_pre, gate_pre, gate_pre, x2, w_r, w_m, w_b, w_o)


def _ffn_kernel(x_ref, xn_ref, nw_ref, fw_ref, w1_ref, w2_ref, o_ref, hn_ref, *, row_chunk,
                final_norm):
    i = pl.program_id(0)
    j = pl.program_id(1)
    rows_next = xn_ref.shape[0]
    slot = i % 2

    @pl.when((i == 0) & (j == 0))
    def _():
        def body(c, carry):
            r = pl.multiple_of(c * row_chunk, row_chunk)
            hn_ref[0, pl.ds(r, row_chunk), :] = _rms_rows(x_ref[pl.ds(r, row_chunk), :],
                                                          nw_ref[...]).astype(BF16)
            return carry
        lax.fori_loop(0, x_ref.shape[0] // row_chunk, body, 0)

    def step():
        u = jnp.dot(hn_ref[slot], w1_ref[...], preferred_element_type=F32)
        u = jnp.maximum(u, 0.0)
        y = jnp.dot((u * u).astype(BF16), w2_ref[...], preferred_element_type=F32)
        r = pl.multiple_of(j * rows_next, rows_next)
        hn_ref[1 - slot, pl.ds(r, rows_next), :] = _rms_rows(xn_ref[...], nw_ref[...]).astype(BF16)
        return y

    @pl.when(j == 0)
    def _():
        o_ref[...] = x_ref[...] + step()

    @pl.when(j > 0)
    def _():
        o_ref[...] += step()

    if final_norm:
        @pl.when(j == pl.num_programs(1) - 1)
        def _():
            def body(c, carry):
                r = pl.multiple_of(c * row_chunk, row_chunk)
                o_ref[pl.ds(r, row_chunk), :] = _rms_rows(o_ref[pl.ds(r, row_chunk), :], fw_ref[...])
                return carry
            lax.fori_loop(0, x_ref.shape[0] // row_chunk, body, 0)


def _ffn(x2, nw, fw, w1, w2, layer, *, tm, tf, final_norm):
    T, D = x2.shape
    F = w1.shape[2]
    n_i, n_j = T // tm, F // tf
    rows_next = tm // n_j
    assert rows_next * n_j == tm and rows_next % (2 * SUBLANES) == 0
    return pl.pallas_call(
        functools.partial(_ffn_kernel, row_chunk=128, final_norm=final_norm),
        out_shape=jax.ShapeDtypeStruct((T, D), F32),
        grid=(n_i, n_j),
        in_specs=[pl.BlockSpec((tm, D), lambda i, j: (i, 0)),
                  pl.BlockSpec((rows_next, D),
                               lambda i, j: (jnp.minimum(i + 1, n_i - 1) * n_j + j, 0)),
                  pl.BlockSpec((1, D), lambda i, j: (0, 0)),
                  pl.BlockSpec((1, D), lambda i, j: (0, 0)),
                  pl.BlockSpec((None, D, tf), lambda i, j: (layer, 0, j)),
                  pl.BlockSpec((None, tf, D), lambda i, j: (layer, j, 0))],
        out_specs=pl.BlockSpec((tm, D), lambda i, j: (i, 0)),
        scratch_shapes=[pltpu.VMEM((2, tm, D), BF16)],
        compiler_params=_cparams(("arbitrary", "arbitrary")),
        name="ffn",
    )(x2, x2, nw, fw, w1, w2)


def _rope_tables(S):
    half = RET_HEAD_DIM // 2
    inv = ROPE_BASE ** (-jnp.arange(half, dtype=F32) / half)
    ang = jnp.arange(S).astype(F32)[:, None] * inv[None, :]
    cos, sin = jnp.cos(ang), jnp.sin(ang)
    return jnp.concatenate([cos, cos], axis=-1), jnp.concatenate([-sin, sin], axis=-1)


def _retention_tables():
    H, L, dh = RET_HEADS, RET_CHUNK, RET_HEAD_DIM
    log_gamma = jnp.log1p(-jnp.exp2(-5.0 - jnp.arange(H, dtype=F32)))
    idx = jnp.arange(L, dtype=F32)
    diff = idx[:, None] - idx[None, :]
    decay = jnp.where(diff >= 0, jnp.exp(jnp.maximum(diff, 0.0) * log_gamma[:, None, None]), 0.0)
    zeta = jnp.exp((L - 1 - idx)[None, :] * log_gamma[:, None])
    xi = jnp.exp((idx + 1.0)[None, :] * log_gamma[:, None])
    return (decay, jnp.broadcast_to(xi[:, :, None], (H, L, dh)),
            jnp.broadcast_to(zeta[:, :, None], (H, L, dh)))


def _t5_bucket(dist):
    n = jnp.maximum(dist, 0)
    exact = REL_BUCKETS // 2
    nf = jnp.maximum(n, 1).astype(F32)
    large = exact + (jnp.log(nf / exact) / math.log(REL_MAX_DIST / exact)
                     * (REL_BUCKETS - exact)).astype(jnp.int32)
    large = jnp.minimum(large, REL_BUCKETS - 1)
    return jnp.where(n < exact, n, large)


def _moba_bias_tables(rel_bias, S):
    BS = MOBA_BLOCK
    assert BS + 1 >= REL_MAX_DIST and S % BS == 0
    table_t = rel_bias.T.astype(F32) * LOG2E
    onehot = (_t5_bucket(jnp.arange(2 * BS))[:, None] == jnp.arange(REL_BUCKETS)).astype(F32)
    by_dist = jnp.einsum("db,hb->hd", onehot, table_t, precision=lax.Precision.HIGHEST)
    far = table_t[:, REL_BUCKETS - 1]
    far_rows = jnp.broadcast_to(far[:, None, None], (far.shape[0], 1, BS))
    return by_dist[:, None, :], far_rows


def kernel(x, w_in, mlstm_gate_b, mlstm_conv_w, mlstm_conv_b, mlstm_norm_w, w_branch_ret,
           w_branch_mlstm, w_branch_moba, w_out, norm_mix_w, norm_mlp_w, w_ff1, w_ff2, rel_bias,
           final_norm_w):
    B, S, D = x.shape
    T = B * S
    depth = w_in.shape[0]
    x2 = x.reshape(T, D)

    cos_t, sin_t = _rope_tables(S)
    decay, xi_b, zeta_b = _retention_tables()
    bias_by_dist, far_rows = _moba_bias_tables(rel_bias, S)
    final_w = final_norm_w.reshape(1, D)

    scale_row = np.ones((1, w_in.shape[2] - _PACK_SHIFT), np.float32)
    scale_row[:, _OFF_RQ:_OFF_RQ + RET_W] = RET_HEAD_DIM ** -0.5
    scale_row[:, _OFF_MI:_OFF_MI + MOBA_W] = MOBA_HEAD_DIM ** -0.5 * LOG2E
    w_cat, w_gpre = _pack_w_in(jnp.swapaxes(w_in, 1, 2), jnp.asarray(scale_row), **_TILES["pack"])
    n_gate_cols = w_cat.shape[2] - _N_HEADMAJOR
    w_r, w_m, w_b, w_o, w_1, w_2 = [_cast_bf16(w, **_TILES["cast"]) for w in
                                    (w_branch_ret, w_branch_mlstm, w_branch_moba, w_out, w_ff1, w_ff2)]

    for l in range(depth):
        proj, hn, gpre = _in_proj(x2, norm_mix_w[l].reshape(1, D), w_cat, l, w_gpre,
                                  **_TILES["in_proj"])
        gate_pre = _matmul(hn, w_cat, l, _N_HEADMAJOR, n_gate_cols, out_dtype=BF16,
                           name="branch_gates", **_TILES["gates"])

        y_ret = _retention(proj, cos_t, sin_t, decay, xi_b, zeta_b, B, S)
        gate_b_row = jnp.pad(mlstm_gate_b[l].reshape(1, 2 * MLSTM_HEADS),
                             ((0, 0), (0, LANES - 2 * MLSTM_HEADS)))
        y_ml = _mlstm(proj, gpre, gate_b_row, mlstm_conv_w[l], mlstm_conv_b[l].reshape(1, -1),
                      mlstm_norm_w[l].reshape(1, -1), B, S)
        y_mb = _moba(proj, bias_by_dist, far_rows, B, S)

        x2 = _merge(y_ret, y_ml, y_mb, gate_pre, x2, w_r, w_m, w_b, w_o, l, **_TILES["merge"])
        x2 = _ffn(x2, norm_mlp_w[l].reshape(1, D), final_w, w_1, w_2, l,
                  final_norm=(l == depth - 1), **_TILES["ffn"])

    return x2.reshape(B, S, D)
```

```python
import functools
import math

import numpy as np
import jax
import jax.numpy as jnp
from jax import lax
from jax.experimental import pallas as pl
from jax.experimental.pallas import tpu as pltpu

D_MODEL = 2048
RET_HEADS = 8
RET_HEAD_DIM = 128
RET_CHUNK = 128
MLSTM_HEADS = 4
MLSTM_QK_DIM = 128
MLSTM_V_DIM = 256
MLSTM_CHUNK = 128
MLSTM_CONV = 4
MOBA_HEADS = 8
MOBA_HEAD_DIM = 128
MOBA_BLOCK = 256
MOBA_TOPK = 3
REL_BUCKETS = 32
REL_MAX_DIST = 128
D_FF = 4 * D_MODEL
ROPE_BASE = 10000.0
EPS = 1e-6

RET_W = RET_HEADS * RET_HEAD_DIM
MLSTM_QK_W = MLSTM_HEADS * MLSTM_QK_DIM
MLSTM_V_W = MLSTM_HEADS * MLSTM_V_DIM
MOBA_W = MOBA_HEADS * MOBA_HEAD_DIM

LANES = 128
SUBLANES = 8
VMEM_LIMIT_BYTES = 56 * 1024 * 1024

_OFF_RQ = 0
_OFF_MI = 4 * RET_W + 2 * MLSTM_QK_W + 2 * MLSTM_V_W
_OFF_BQ = _OFF_MI + 2 * MLSTM_HEADS
_OFF_GATES = _OFF_BQ + 3 * MOBA_W
_N_HEADMAJOR = _OFF_MI + 3 * MOBA_W
_BLK_RQ, _BLK_RK, _BLK_RV, _BLK_RG = 0, 8, 16, 24
_BLK_MQ, _BLK_MK, _BLK_MV, _BLK_MO = 32, 36, 40, 48
_BLK_BQ, _BLK_BK, _BLK_BV = 56, 64, 72

MOBA_KEY_SPLIT = 1
MOBA_LOGITS_LEAD = 2
RET_HEADS_PER_STEP = 1
RET_SKEW = 1
NEG_BIG = -1e30
LOG2E = 1.0 / math.log(2.0)

_TILES = {
    "in_proj": dict(tm=1024, tn=1024),
    "gates": dict(tm=1024, tn=2048),
    "merge": dict(tm=256),
    "ffn": dict(tm=512, tf=1024),
    "pack": dict(tr=1024),
    "cast": dict(tr=1024, tc=2048),
}

BF16 = jnp.bfloat16
F32 = jnp.float32


def _cparams(sem):
    return pltpu.CompilerParams(dimension_semantics=sem, vmem_limit_bytes=VMEM_LIMIT_BYTES)


def _rms_rows(xf, w_row):
    ms = jnp.mean(xf * xf, axis=-1, keepdims=True)
    return xf * lax.rsqrt(ms + EPS) * w_row


def _norm_rows_to(x_ref, nw_ref, hn_ref, row_chunk):
    def body(c, carry):
        r = pl.multiple_of(c * row_chunk, row_chunk)
        hn_ref[pl.ds(r, row_chunk), :] = _rms_rows(x_ref[pl.ds(r, row_chunk), :],
                                                   nw_ref[...]).astype(BF16)
        return carry
    lax.fori_loop(0, x_ref.shape[0] // row_chunk, body, 0)


def _in_proj_kernel(x_ref, nw_ref, w_ref, ws_ref, o_ref, hn_ref, os_ref, *, row_chunk):
    @pl.when(pl.program_id(1) == 0)
    def _():
        _norm_rows_to(x_ref, nw_ref, hn_ref, row_chunk)
        os_ref[...] = jnp.dot(hn_ref[...], ws_ref[...], preferred_element_type=F32)

    acc = jnp.dot(hn_ref[...], w_ref[...], preferred_element_type=F32)
    for c in range(o_ref.shape[0]):
        o_ref[c] = acc[:, c * LANES:(c + 1) * LANES].astype(o_ref.dtype)


def _in_proj(x2, nw, w_cat, layer, w_side, *, tm, tn):
    T, D = x2.shape
    N = _N_HEADMAJOR
    NS = w_side.shape[2]
    return pl.pallas_call(
        functools.partial(_in_proj_kernel, row_chunk=128),
        out_shape=(jax.ShapeDtypeStruct((N // LANES, T, LANES), BF16),
                   jax.ShapeDtypeStruct((T, D), BF16),
                   jax.ShapeDtypeStruct((T, NS), F32)),
        grid=(T // tm, N // tn),
        in_specs=[pl.BlockSpec((tm, D), lambda i, j: (i, 0)),
                  pl.BlockSpec((1, D), lambda i, j: (0, 0)),
                  pl.BlockSpec((None, D, tn), lambda i, j: (layer, 0, j)),
                  pl.BlockSpec((None, D, NS), lambda i, j: (layer, 0, 0))],
        out_specs=(pl.BlockSpec((tn // LANES, tm, LANES), lambda i, j: (j, i, 0)),
                   pl.BlockSpec((tm, D), lambda i, j: (i, 0)),
                   pl.BlockSpec((tm, NS), lambda i, j: (i, 0))),
        compiler_params=_cparams(("parallel", "arbitrary")),
        name="in_proj",
    )(x2, nw, w_cat, w_side)


def _matmul_kernel(h_ref, w_ref, o_ref):
    o_ref[...] = jnp.dot(h_ref[...], w_ref[...], preferred_element_type=F32).astype(o_ref.dtype)


def _matmul(hn, w_cat, layer, col0, n_cols, *, tm, tn, out_dtype, name):
    T, D = hn.shape
    assert col0 % tn == 0 and n_cols % tn == 0
    return pl.pallas_call(
        _matmul_kernel,
        out_shape=jax.ShapeDtypeStruct((T, n_cols), out_dtype),
        grid=(T // tm, n_cols // tn),
        in_specs=[pl.BlockSpec((tm, D), lambda i, j: (i, 0)),
                  pl.BlockSpec((None, D, tn), lambda i, j: (layer, 0, col0 // tn + j))],
        out_specs=pl.BlockSpec((tm, tn), lambda i, j: (i, j)),
        compiler_params=_cparams(("parallel", "parallel")),
        name=name,
    )(hn, w_cat)


def _cast_kernel(w_ref, o_ref, *, row_chunk):
    def body(c, carry):
        r = pl.multiple_of(c * row_chunk, row_chunk)
        o_ref[pl.ds(r, row_chunk), :] = w_ref[pl.ds(r, row_chunk), :].astype(BF16)
        return carry
    lax.fori_loop(0, w_ref.shape[0] // row_chunk, body, 0)


def _cast_bf16(w3, *, tr, tc):
    Lw, R, C = w3.shape
    return pl.pallas_call(
        functools.partial(_cast_kernel, row_chunk=128),
        out_shape=jax.ShapeDtypeStruct((Lw, R, C), BF16),
        grid=(Lw, R // tr, C // tc),
        in_specs=[pl.BlockSpec((None, tr, tc), lambda l, i, j: (l, i, j))],
        out_specs=pl.BlockSpec((None, tr, tc), lambda l, i, j: (l, i, j)),
        compiler_params=_cparams(("parallel", "parallel", "parallel")),
        name="cast_bf16",
    )(w3)


_PACK_TC = 1024
_PACK_SHIFT = 2 * MLSTM_HEADS
_PACK_ALIGNED_TILES = _OFF_MI // _PACK_TC


def _pack_w_in_kernel(a_ref, b_ref, s_ref, o_ref, g_ref):
    j = pl.program_id(2)
    tc, tr = a_ref.shape
    blk = LANES

    def emit(shift):
        for rb in range(tc // blk):
            lo, hi = rb * blk + shift, (rb + 1) * blk + shift
            for cb in range(tr // blk):
                cols = slice(cb * blk, (cb + 1) * blk)
                if hi <= tc:
                    x = a_ref[lo:hi, cols]
                else:
                    x = jnp.concatenate([a_ref[lo:tc, cols], b_ref[0:hi - tc, cols]], axis=0)
                y = x.T * s_ref[:, rb * blk:(rb + 1) * blk]
                o_ref[cols, rb * blk:(rb + 1) * blk] = y.astype(BF16)

    @pl.when(j < _PACK_ALIGNED_TILES)
    def _():
        emit(0)

    @pl.when(j >= _PACK_ALIGNED_TILES)
    def _():
        emit(_PACK_SHIFT)

    @pl.when(j == _PACK_ALIGNED_TILES)
    def _():
        zeros = jnp.zeros((blk - _PACK_SHIFT, blk), F32)
        for cb in range(tr // blk):
            cols = slice(cb * blk, (cb + 1) * blk)
            x = jnp.concatenate([a_ref[0:_PACK_SHIFT, cols], zeros], axis=0)
            g_ref[cols, :] = x.T.astype(BF16)


def _pack_w_in(w_in_t, scale_row, *, tr):
    depth, d_in, D = w_in_t.shape
    tc = _PACK_TC
    n_out = d_in - _PACK_SHIFT
    assert _OFF_MI % tc == 0 and n_out % tc == 0 and tc % LANES == 0 and _PACK_SHIFT == SUBLANES
    return pl.pallas_call(
        _pack_w_in_kernel,
        out_shape=(jax.ShapeDtypeStruct((depth, D, n_out), BF16),
                   jax.ShapeDtypeStruct((depth, D, LANES), BF16)),
        grid=(depth, D // tr, n_out // tc),
        in_specs=[pl.BlockSpec((None, tc, tr), lambda l, i, j: (l, j, i)),
                  pl.BlockSpec((None, SUBLANES, tr),
                               lambda l, i, j: (l, (j + 1) * (tc // SUBLANES), i)),
                  pl.BlockSpec((1, tc), lambda l, i, j: (0, j))],
        out_specs=(pl.BlockSpec((None, tr, tc), lambda l, i, j: (l, i, j)),
                   pl.BlockSpec((None, tr, LANES), lambda l, i, j: (l, i, 0))),
        compiler_params=_cparams(("parallel", "parallel", "arbitrary")),
        name="pack_w_in",
    )(w_in_t, w_in_t, scale_row)


def _retention_kernel(q_ref, k_ref, v_ref, g_ref, cos_ref, sin_ref, decay_ref, xi_ref, zeta_ref,
                      o_ref):
    L = RET_CHUNK
    dh = RET_HEAD_DIM
    HP, S = q_ref.shape[0], q_ref.shape[1]

    def rope(x, cos, sin_signed):
        return x * cos + pltpu.roll(x, dh // 2, 1) * sin_signed

    def state_free(hh, n):
        rows = slice(n * L, (n + 1) * L)
        cos = cos_ref[rows, :]
        sin = sin_ref[rows, :]
        q = rope(q_ref[hh, rows, :].astype(F32), cos, sin)
        k = rope(k_ref[hh, rows, :].astype(F32), cos, sin)
        v = v_ref[hh, rows, :]
        qb = q.astype(BF16)
        scores = lax.dot_general(qb, k.astype(BF16), (((1,), (1,)), ((), ())),
                                 preferred_element_type=F32) * decay_ref[hh]
        kv = lax.dot_general((k * zeta_ref[hh]).astype(BF16), v, (((0,), (0,)), ((), ())),
                             preferred_element_type=F32)
        return qb, scores.astype(BF16), v, kv

    def finish(hh, n, parts, r_prev):
        qb, scores, v, kv = parts
        rows = slice(n * L, (n + 1) * L)
        xi = xi_ref[hh]
        intra = jnp.dot(scores, v, preferred_element_type=F32)
        inter = jnp.dot(qb, r_prev.astype(BF16), preferred_element_type=F32) * xi
        o = intra + inter
        o = o * lax.rsqrt(jnp.mean(o * o, axis=-1, keepdims=True) + EPS)
        g = g_ref[hh, rows, :].astype(F32)
        o_ref[rows, hh * dh:(hh + 1) * dh] = (o * (g * jax.nn.sigmoid(g))).astype(o_ref.dtype)
        return r_prev * xi[L - 1:L, :] + kv

    n_chunks = S // L
    for hh in range(HP):
        r_prev = jnp.zeros((dh, dh), F32)
        pending = [state_free(hh, n) for n in range(min(RET_SKEW, n_chunks))]
        for n in range(n_chunks):
            if n + RET_SKEW < n_chunks:
                pending.append(state_free(hh, n + RET_SKEW))
            r_prev = finish(hh, n, pending.pop(0), r_prev)


def _retention(proj, cos_t, sin_t, decay, xi_b, zeta_b, B, S):
    H, dh, L, HP = RET_HEADS, RET_HEAD_DIM, RET_CHUNK, RET_HEADS_PER_STEP

    def pspec(base):
        return pl.BlockSpec((HP, S, LANES), lambda b, h, base=base: (base // HP + h, b, 0))

    def tspec(width):
        return pl.BlockSpec((HP, L, width), lambda b, h: (h, 0, 0))

    assert H % HP == 0 and all(blk % HP == 0 for blk in (_BLK_RQ, _BLK_RK, _BLK_RV, _BLK_RG))
    return pl.pallas_call(
        _retention_kernel,
        out_shape=jax.ShapeDtypeStruct((B * S, RET_W), BF16),
        grid=(B, H // HP),
        in_specs=[pspec(_BLK_RQ), pspec(_BLK_RK), pspec(_BLK_RV), pspec(_BLK_RG),
                  pl.BlockSpec((S, dh), lambda b, h: (0, 0)),
                  pl.BlockSpec((S, dh), lambda b, h: (0, 0)),
                  tspec(L), tspec(dh), tspec(dh)],
        out_specs=pl.BlockSpec((S, HP * dh), lambda b, h: (b, h)),
        compiler_params=_cparams(("parallel", "parallel")),
        name="retention",
    )(proj, proj, proj, proj, cos_t, sin_t, decay, xi_b, zeta_b)


def _mlstm_kernel(q_ref, k_ref, v_ref, og_ref, gp_ref, gb_ref, cwq_ref, cwk_ref, cbq_ref, cbk_ref,
                  nw_ref, o_ref, qpad_ref, kpad_ref, a_ref, bc_ref, at_ref, bt_ref):
    L = MLSTM_CHUNK
    S = q_ref.shape[0]
    NH = MLSTM_HEADS
    KC = MLSTM_CONV
    h = pl.program_id(1)
    n_chunks = S // L

    qpad_ref[0:SUBLANES, :] = jnp.zeros((SUBLANES, LANES), F32)
    kpad_ref[0:SUBLANES, :] = jnp.zeros((SUBLANES, LANES), F32)

    def conv_silu(src_ref, pad_ref, cw_ref, cb_ref, c, scale):
        pad_ref[SUBLANES + c * L:SUBLANES + (c + 1) * L, :] = src_ref[c * L:(c + 1) * L, :].astype(F32)
        cw = cw_ref[...]
        acc = cb_ref[...]
        for t in range(KC):
            r0 = SUBLANES + c * L - (KC - 1) + t
            acc = acc + pad_ref[r0:r0 + L, :] * cw[t:t + 1, :]
        return acc * jax.nn.sigmoid(acc) * scale

    lane = lax.broadcasted_iota(jnp.int32, (L, LANES), 1)
    sub = lax.broadcasted_iota(jnp.int32, (L, LANES), 0)
    sub_col = lax.broadcasted_iota(jnp.int32, (L, 1), 0)
    causal = sub >= lane
    nw = nw_ref[...]

    @pl.when(h == 0)
    def _():
        tri = (sub >= lane).astype(F32)
        is_f = (lane >= NH) & (lane < 2 * NH)
        for n in range(n_chunks):
            rows = slice(n * L, (n + 1) * L)
            gp = gp_ref[rows, :] + gb_ref[...]
            logsig = jnp.minimum(gp, 0.0) - jnp.log1p(jnp.exp(-jnp.abs(gp)))
            a = jnp.where(is_f, logsig, gp)
            bc = jnp.dot(tri, a, preferred_element_type=F32, precision=lax.Precision.HIGHEST)
            a_ref[rows, :] = a
            bc_ref[rows, :] = bc
            at_ref[n * SUBLANES:(n + 1) * SUBLANES, :] = a.T[0:SUBLANES, :]
            bt_ref[n * SUBLANES:(n + 1) * SUBLANES, :] = bc.T[0:SUBLANES, :]

    def state_free(n):
        rows = slice(n * L, (n + 1) * L)
        li_col = jnp.sum(jnp.where(lane == h, a_ref[rows, :], 0.0), axis=1, keepdims=True)
        b_col = jnp.sum(jnp.where(lane == h + NH, bc_ref[rows, :], 0.0), axis=1, keepdims=True)
        li_row = at_ref[pl.ds(n * SUBLANES + h, 1), :]
        b_row = bt_ref[pl.ds(n * SUBLANES + NH + h, 1), :]
        gtot = jnp.max(jnp.where(sub_col == L - 1, b_col, NEG_BIG), axis=0, keepdims=True)
        q = conv_silu(q_ref, qpad_ref, cwq_ref, cbq_ref, n, MLSTM_QK_DIM ** -0.5)
        k = conv_silu(k_ref, kpad_ref, cwk_ref, cbk_ref, n, 1.0)
        v = jnp.concatenate([v_ref[0, rows, :], v_ref[1, rows, :]], axis=1)
        qb = q.astype(BF16)
        dlog = jnp.where(causal, b_col - b_row + li_row, NEG_BIG)
        dmax = jnp.max(dlog, axis=1, keepdims=True)
        qk = lax.dot_general(qb, k.astype(BF16), (((1,), (1,)), ((), ())),
                             preferred_element_type=F32)
        return dict(li_col=li_col, b_col=b_col, gtot=gtot, q=q, k=k, qb=qb, v=v, dlog=dlog,
                    dmax=dmax, qk=qk)

    def finish(n, p, state):
        c_prev, n_prev, m_prev = state
        rows = slice(n * L, (n + 1) * L)
        inter_log = p["b_col"] + m_prev
        m_pos = jnp.maximum(inter_log, p["dmax"])
        s_intra = p["qk"] * jnp.exp(p["dlog"] - m_pos)
        inter_scale = jnp.exp(inter_log - m_pos)
        num = (jnp.dot(s_intra.astype(BF16), p["v"], preferred_element_type=F32)
               + inter_scale * jnp.dot(p["qb"], c_prev.astype(BF16), preferred_element_type=F32))
        den = (jnp.sum(s_intra, axis=1, keepdims=True)
               + inter_scale * jnp.sum(p["q"] * n_prev, axis=1, keepdims=True))
        inv = 1.0 / jnp.maximum(jnp.abs(den), jnp.exp(-m_pos))
        ms = jnp.mean(num * num, axis=-1, keepdims=True)
        row_scale = inv * lax.rsqrt(inv * inv * ms + EPS)
        og = jnp.concatenate([og_ref[0, rows, :], og_ref[1, rows, :]], axis=1).astype(F32)
        o_ref[rows, :] = (num * row_scale * nw * jax.nn.sigmoid(og)).astype(o_ref.dtype)
        w_loc = p["gtot"] - p["b_col"] + p["li_col"]
        m_loc = jnp.max(w_loc, axis=0, keepdims=True)
        ke = p["k"] * jnp.exp(w_loc - m_loc)
        c_loc = lax.dot_general(ke.astype(BF16), p["v"], (((0,), (0,)), ((), ())),
                                preferred_element_type=F32)
        n_loc = jnp.sum(ke, axis=0, keepdims=True)
        m_new = jnp.maximum(p["gtot"] + m_prev, m_loc)
        a_dec = jnp.exp(p["gtot"] + m_prev - m_new)
        b_inc = jnp.exp(m_loc - m_new)
        return (a_dec * c_prev + b_inc * c_loc, a_dec * n_prev + b_inc * n_loc, m_new)

    state = (jnp.zeros((MLSTM_QK_DIM, MLSTM_V_DIM), F32), jnp.zeros((1, MLSTM_QK_DIM), F32),
             jnp.zeros((1, 1), F32))
    for n in range(n_chunks):
        state = finish(n, state_free(n), state)


def _mlstm(proj, gpre, gate_b_row, conv_w, conv_b, norm_w, B, S):
    NH, dk, dv = MLSTM_HEADS, MLSTM_QK_DIM, MLSTM_V_DIM
    KC = MLSTM_CONV
    nvb = dv // LANES

    def pspec(base):
        return pl.BlockSpec((None, S, LANES), lambda b, h, base=base: (base + h, b, 0))

    def pspec2(base):
        return pl.BlockSpec((nvb, S, LANES), lambda b, h, base=base: (base // nvb + h, b, 0))

    return pl.pallas_call(
        _mlstm_kernel,
        out_shape=jax.ShapeDtypeStruct((B * S, MLSTM_V_W), BF16),
        grid=(B, NH),
        in_specs=[pspec(_BLK_MQ), pspec(_BLK_MK), pspec2(_BLK_MV), pspec2(_BLK_MO),
                  pl.BlockSpec((S, LANES), lambda b, h: (b, 0)),
                  pl.BlockSpec((1, LANES), lambda b, h: (0, 0)),
                  pl.BlockSpec((KC, dk), lambda b, h: (0, h)),
                  pl.BlockSpec((KC, dk), lambda b, h: (0, NH + h)),
                  pl.BlockSpec((1, dk), lambda b, h: (0, h)),
                  pl.BlockSpec((1, dk), lambda b, h: (0, NH + h)),
                  pl.BlockSpec((1, dv), lambda b, h: (0, h))],
        out_specs=pl.BlockSpec((S, dv), lambda b, h: (b, h)),
        scratch_shapes=[pltpu.VMEM((S + SUBLANES, dk), F32),
                        pltpu.VMEM((S + SUBLANES, dk), F32),
                        pltpu.VMEM((S, LANES), F32),
                        pltpu.VMEM((S, LANES), F32),
                        pltpu.VMEM((S // MLSTM_CHUNK * SUBLANES, MLSTM_CHUNK), F32),
                        pltpu.VMEM((S // MLSTM_CHUNK * SUBLANES, MLSTM_CHUNK), F32)],
        compiler_params=_cparams(("parallel", "arbitrary")),
        name="mlstm",
    )(proj, proj, proj, proj, gpre, gate_b_row, conv_w, conv_w, conv_b, conv_b, norm_w)


def _moba_kernel(q_ref, k_ref, v_ref, bvec_ref, far_ref, o_ref, kmean_ref, gate_ref, lg_ref,
                 bias_ref):
    BS = MOBA_BLOCK
    S = k_ref.shape[0]
    NB = S // BS
    nt = (((1,), (1,)), ((), ()))

    toep = pltpu.roll(jnp.broadcast_to(bvec_ref[...], (BS, 2 * BS)), 0, 1, stride=1, stride_axis=0)
    bias_ref[0] = toep[:, 0:BS]
    bias_ref[1] = toep[:, BS:2 * BS]

    kmean_ref[...] = jnp.zeros_like(kmean_ref)
    ones_rows = jnp.ones((2 * SUBLANES, BS), BF16)
    vts = []

    def prepare_key_block(j):
        kj = k_ref[j * BS:(j + 1) * BS, :].astype(F32)
        kmean_ref[j:j + 1, :] = jnp.mean(kj, axis=0, keepdims=True)
        vts.append(jnp.concatenate([v_ref[j * BS:(j + 1) * BS, :].astype(F32).T.astype(BF16),
                                    ones_rows], axis=0))

    dh = MOBA_HEAD_DIM
    far = far_ref[...]
    kl = lax.broadcasted_iota(jnp.int32, (BS, BS), 0)
    ql = lax.broadcasted_iota(jnp.int32, (BS, BS), 1)
    causal = kl <= ql

    def select_rows(i, q):
        if i <= MOBA_TOPK:
            return [None] * i
        gate_ref[i % 2] = lax.dot_general(kmean_ref[...], q.astype(F32), nt,
                                          preferred_element_type=F32,
                                          precision=lax.Precision.HIGHEST)
        rows = [gate_ref[i % 2, n:n + 1, :] for n in range(i)]
        add_rows = []
        for n in range(i):
            rank = jnp.zeros_like(rows[n])
            for mth in range(i):
                if mth == n:
                    continue
                ahead = rows[mth] > rows[n]
                if mth < n:
                    ahead = ahead | (rows[mth] == rows[n])
                rank = rank + jnp.where(ahead, 1.0, 0.0)
            add_rows.append(jnp.where(rank < MOBA_TOPK, 0.0, NEG_BIG))
        return add_rows

    def merge(a, b, op):
        return b if a is None else op(a, b)

    KT = BS // MOBA_KEY_SPLIT

    def logits_tile(i, j, q, add_rows):
        m = None
        for part in range(MOBA_KEY_SPLIT):
            ks = slice(part * KT, (part + 1) * KT)
            lg = lax.dot_general(k_ref[j * BS + part * KT:j * BS + (part + 1) * KT, :], q, nt,
                                 preferred_element_type=F32)
            if j == i:
                lg = jnp.where(causal[ks, :], lg + bias_ref[0, ks, :], NEG_BIG)
            elif j == i - 1:
                lg = lg + bias_ref[1, ks, :]
                if add_rows[j] is not None:
                    lg = lg + add_rows[j]
            else:
                lg = lg + (far if add_rows[j] is None else far + add_rows[j])
            lg_ref[i % 2, j, ks, :] = lg
            m = merge(m, jnp.max(lg, axis=0, keepdims=True), jnp.maximum)
        return m

    def value_tile(i, j, m):
        pv = None
        for part in range(MOBA_KEY_SPLIT):
            ks = slice(part * KT, (part + 1) * KT)
            p = jnp.exp2(lg_ref[i % 2, j, ks, :] - m)
            pv = merge(pv, jnp.dot(vts[j][:, ks], p.astype(BF16), preferred_element_type=F32),
                       jnp.add)
        return pv

    m_cur = logits_tile(0, 0, q_ref[0:BS, :], [])
    for i in range(NB):
        prepare_key_block(i)
        m_next = None
        if i + 1 < NB:
            q_next = q_ref[(i + 1) * BS:(i + 2) * BS, :]
            rows_next = select_rows(i + 1, q_next)
        acc = None
        lead = MOBA_LOGITS_LEAD
        if i + 1 < NB:
            for j in range(min(lead, i + 2)):
                m_next = merge(m_next, logits_tile(i + 1, j, q_next, rows_next), jnp.maximum)
        for j in range(i + 1):
            if i + 1 < NB and j + lead < i + 2:
                m_next = jnp.maximum(m_next, logits_tile(i + 1, j + lead, q_next, rows_next))
            acc = merge(acc, value_tile(i, j, m_cur), jnp.add)
        l = jnp.max(acc[dh:dh + SUBLANES, :], axis=0, keepdims=True)
        out_t = acc[0:dh, :] * (1.0 / l)
        o_ref[i * BS:(i + 1) * BS, :] = out_t.T.astype(o_ref.dtype)
        m_cur = m_next


def _moba(proj, bias_tiles, far_rows, B, S):
    H, dh, BS = MOBA_HEADS, MOBA_HEAD_DIM, MOBA_BLOCK
    NB = S // BS

    def pspec(base):
        return pl.BlockSpec((None, S, LANES), lambda b, h, base=base: (base + h, b, 0))

    return pl.pallas_call(
        _moba_kernel,
        out_shape=jax.ShapeDtypeStruct((B * S, MOBA_W), BF16),
        grid=(B, H),
        in_specs=[pspec(_BLK_BQ), pspec(_BLK_BK), pspec(_BLK_BV),
                  pl.BlockSpec((None, 1, 2 * BS), lambda b, h: (h, 0, 0)),
                  pl.BlockSpec((None, 1, BS), lambda b, h: (h, 0, 0))],
        out_specs=pl.BlockSpec((S, dh), lambda b, h: (b, h)),
        scratch_shapes=[pltpu.VMEM((NB, dh), F32),
                        pltpu.VMEM((2, NB, BS), F32),
                        pltpu.VMEM((2, NB, BS, BS), F32),
                        pltpu.VMEM((2, BS, BS), F32)],
        compiler_params=_cparams(("parallel", "parallel")),
        name="moba",
    )(proj, proj, proj, bias_tiles, far_rows)


def _merge_kernel(yr_ref, ym_ref, yb_ref, gr_ref, gm_ref, gb_ref, x_ref,
                  wr_ref, wm_ref, wb_ref, wo_ref, o_ref):
    def gate(g_ref):
        return jax.nn.sigmoid(g_ref[...].astype(F32))

    mixed = gate(gr_ref) * jnp.dot(yr_ref[...], wr_ref[...], preferred_element_type=F32)
    mixed = mixed + gate(gm_ref) * jnp.dot(ym_ref[...], wm_ref[...], preferred_element_type=F32)
    mixed = mixed + gate(gb_ref) * jnp.dot(yb_ref[...], wb_ref[...], preferred_element_type=F32)
    o_ref[...] = x_ref[...] + jnp.dot(mixed.astype(BF16), wo_ref[...], preferred_element_type=F32)


def _merge(y_ret, y_ml, y_mb, gate_pre, x2, w_r, w_m, w_b, w_o, layer, *, tm):
    T, D = x2.shape

    def resident(shape):
        return pl.BlockSpec((None,) + shape, lambda i: (layer, 0, 0), pipeline_mode=pl.Buffered(1))

    return pl.pallas_call(
        _merge_kernel,
        out_shape=jax.ShapeDtypeStruct((T, D), F32),
        grid=(T // tm,),
        in_specs=[pl.BlockSpec((tm, RET_W), lambda i: (i, 0)),
                  pl.BlockSpec((tm, MLSTM_V_W), lambda i: (i, 0)),
                  pl.BlockSpec((tm, MOBA_W), lambda i: (i, 0)),
                  pl.BlockSpec((tm, D), lambda i: (i, 0)),
                  pl.BlockSpec((tm, D), lambda i: (i, 1)),
                  pl.BlockSpec((tm, D), lambda i: (i, 2)),
                  pl.BlockSpec((tm, D), lambda i: (i, 0)),
                  resident((RET_W, D)), resident((MLSTM_V_W, D)), resident((MOBA_W, D)),
                  resident((D, D))],
        out_specs=pl.BlockSpec((tm, D), lambda i: (i, 0)),
        compiler_params=_cparams(("parallel",)),
        name="merge",
    )(y_ret, y_ml, y_mb, gate_pre, gate_pre, gate_pre, x2, w_r, w_m, w_b, w_o)


def _ffn_kernel(x_ref, xn_ref, nw_ref, fw_ref, w1_ref, w2_ref, o_ref, hn_ref, *, row_chunk,
                final_norm):
    i = pl.program_id(0)
    j = pl.program_id(1)
    rows_next = xn_ref.shape[0]
    slot = i % 2

    @pl.when((i == 0) & (j == 0))
    def _():
        def body(c, carry):
            r = pl.multiple_of(c * row_chunk, row_chunk)
            hn_ref[0, pl.ds(r, row_chunk), :] = _rms_rows(x_ref[pl.ds(r, row_chunk), :],
                                                          nw_ref[...]).astype(BF16)
            return carry
        lax.fori_loop(0, x_ref.shape[0] // row_chunk, body, 0)

    def step():
        u = jnp.dot(hn_ref[slot], w1_ref[...], preferred_element_type=F32)
        u = jnp.maximum(u, 0.0)
        y = jnp.dot((u * u).astype(BF16), w2_ref[...], preferred_element_type=F32)
        r = pl.multiple_of(j * rows_next, rows_next)
        hn_ref[1 - slot, pl.ds(r, rows_next), :] = _rms_rows(xn_ref[...], nw_ref[...]).astype(BF16)
        return y

    @pl.when(j == 0)
    def _():
        o_ref[...] = x_ref[...] + step()

    @pl.when(j > 0)
    def _():
        o_ref[...] += step()

    if final_norm:
        @pl.when(j == pl.num_programs(1) - 1)
        def _():
            def body(c, carry):
                r = pl.multiple_of(c * row_chunk, row_chunk)
                o_ref[pl.ds(r, row_chunk), :] = _rms_rows(o_ref[pl.ds(r, row_chunk), :], fw_ref[...])
                return carry
            lax.fori_loop(0, x_ref.shape[0] // row_chunk, body, 0)


def _ffn(x2, nw, fw, w1, w2, layer, *, tm, tf, final_norm):
    T, D = x2.shape
    F = w1.shape[2]
    n_i, n_j = T // tm, F // tf
    rows_next = tm // n_j
    assert rows_next * n_j == tm and rows_next % (2 * SUBLANES) == 0
    return pl.pallas_call(
        functools.partial(_ffn_kernel, row_chunk=128, final_norm=final_norm),
        out_shape=jax.ShapeDtypeStruct((T, D), F32),
        grid=(n_i, n_j),
        in_specs=[pl.BlockSpec((tm, D), lambda i, j: (i, 0)),
                  pl.BlockSpec((rows_next, D),
                               lambda i, j: (jnp.minimum(i + 1, n_i - 1) * n_j + j, 0)),
                  pl.BlockSpec((1, D), lambda i, j: (0, 0)),
                  pl.BlockSpec((1, D), lambda i, j: (0, 0)),
                  pl.BlockSpec((None, D, tf), lambda i, j: (layer, 0, j)),
                  pl.BlockSpec((None, tf, D), lambda i, j: (layer, j, 0))],
        out_specs=pl.BlockSpec((tm, D), lambda i, j: (i, 0)),
        scratch_shapes=[pltpu.VMEM((2, tm, D), BF16)],
        compiler_params=_cparams(("arbitrary", "arbitrary")),
        name="ffn",
    )(x2, x2, nw, fw, w1, w2)


def _rope_tables(S):
    half = RET_HEAD_DIM // 2
    inv = ROPE_BASE ** (-jnp.arange(half, dtype=F32) / half)
    ang = jnp.arange(S).astype(F32)[:, None] * inv[None, :]
    cos, sin = jnp.cos(ang), jnp.sin(ang)
    return jnp.concatenate([cos, cos], axis=-1), jnp.concatenate([-sin, sin], axis=-1)


def _retention_tables():
    H, L, dh = RET_HEADS, RET_CHUNK, RET_HEAD_DIM
    log_gamma = jnp.log1p(-jnp.exp2(-5.0 - jnp.arange(H, dtype=F32)))
    idx = jnp.arange(L, dtype=F32)
    diff = idx[:, None] - idx[None, :]
    decay = jnp.where(diff >= 0, jnp.exp(jnp.maximum(diff, 0.0) * log_gamma[:, None, None]), 0.0)
    zeta = jnp.exp((L - 1 - idx)[None, :] * log_gamma[:, None])
    xi = jnp.exp((idx + 1.0)[None, :] * log_gamma[:, None])
    return (decay, jnp.broadcast_to(xi[:, :, None], (H, L, dh)),
            jnp.broadcast_to(zeta[:, :, None], (H, L, dh)))


def _t5_bucket(dist):
    n = jnp.maximum(dist, 0)
    exact = REL_BUCKETS // 2
    nf = jnp.maximum(n, 1).astype(F32)
    large = exact + (jnp.log(nf / exact) / math.log(REL_MAX_DIST / exact)
                     * (REL_BUCKETS - exact)).astype(jnp.int32)
    large = jnp.minimum(large, REL_BUCKETS - 1)
    return jnp.where(n < exact, n, large)


def _moba_bias_tables(rel_bias, S):
    BS = MOBA_BLOCK
    assert BS + 1 >= REL_MAX_DIST and S % BS == 0
    table_t = rel_bias.T.astype(F32) * LOG2E
    onehot = (_t5_bucket(jnp.arange(2 * BS))[:, None] == jnp.arange(REL_BUCKETS)).astype(F32)
    by_dist = jnp.einsum("db,hb->hd", onehot, table_t, precision=lax.Precision.HIGHEST)
    far = table_t[:, REL_BUCKETS - 1]
    far_rows = jnp.broadcast_to(far[:, None, None], (far.shape[0], 1, BS))
    return by_dist[:, None, :], far_rows


def kernel(x, w_in, mlstm_gate_b, mlstm_conv_w, mlstm_conv_b, mlstm_norm_w, w_branch_ret,
           w_branch_mlstm, w_branch_moba, w_out, norm_mix_w, norm_mlp_w, w_ff1, w_ff2, rel_bias,
           final_norm_w):
    B, S, D = x.shape
    T = B * S
    depth = w_in.shape[0]
    x2 = x.reshape(T, D)

    cos_t, sin_t = _rope_tables(S)
    decay, xi_b, zeta_b = _retention_tables()
    bias_by_dist, far_rows = _moba_bias_tables(rel_bias, S)
    final_w = final_norm_w.reshape(1, D)

    scale_row = np.ones((1, w_in.shape[2] - _PACK_SHIFT), np.float32)
    scale_row[:, _OFF_RQ:_OFF_RQ + RET_W] = RET_HEAD_DIM ** -0.5
    scale_row[:, _OFF_MI:_OFF_MI + MOBA_W] = MOBA_HEAD_DIM ** -0.5 * LOG2E
    w_cat, w_gpre = _pack_w_in(jnp.swapaxes(w_in, 1, 2), jnp.asarray(scale_row), **_TILES["pack"])
    n_gate_cols = w_cat.shape[2] - _N_HEADMAJOR
    w_r, w_m, w_b, w_o, w_1, w_2 = [_cast_bf16(w, **_TILES["cast"]) for w in
                                    (w_branch_ret, w_branch_mlstm, w_branch_moba, w_out, w_ff1, w_ff2)]

    for l in range(depth):
        proj, hn, gpre = _in_proj(x2, norm_mix_w[l].reshape(1, D), w_cat, l, w_gpre,
                                  **_TILES["in_proj"])
        gate_pre = _matmul(hn, w_cat, l, _N_HEADMAJOR, n_gate_cols, out_dtype=BF16,
                           name="branch_gates", **_TILES["gates"])

        y_ret = _retention(proj, cos_t, sin_t, decay, xi_b, zeta_b, B, S)
        gate_b_row = jnp.pad(mlstm_gate_b[l].reshape(1, 2 * MLSTM_HEADS),
                             ((0, 0), (0, LANES - 2 * MLSTM_HEADS)))
        y_ml = _mlstm(proj, gpre, gate_b_row, mlstm_conv_w[l], mlstm_conv_b[l].reshape(1, -1),
                      mlstm_norm_w[l].reshape(1, -1), B, S)
        y_mb = _moba(proj, bias_by_dist, far_rows, B, S)

        x2 = _merge(y_ret, y_ml, y_mb, gate_pre, x2, w_r, w_m, w_b, w_o, l, **_TILES["merge"])
        x2 = _ffn(x2, norm_mlp_w[l].reshape(1, D), final_w, w_1, w_2, l,
                  final_norm=(l == depth - 1), **_TILES["ffn"])

    return x2.reshape(B, S, D)
```

```python
import functools
import math

import numpy as np
import jax
import jax.numpy as jnp
from jax import lax
from jax.experimental import pallas as pl
from jax.experimental.pallas import tpu as pltpu

D_MODEL = 2048
RET_HEADS = 8
RET_HEAD_DIM = 128
RET_CHUNK = 128
MLSTM_HEADS = 4
MLSTM_QK_DIM = 128
MLSTM_V_DIM = 256
MLSTM_CHUNK = 128
MLSTM_CONV = 4
MOBA_HEADS = 8
MOBA_HEAD_DIM = 128
MOBA_BLOCK = 256
MOBA_TOPK = 3
REL_BUCKETS = 32
REL_MAX_DIST = 128
D_FF = 4 * D_MODEL
ROPE_BASE = 10000.0
EPS = 1e-6

RET_W = RET_HEADS * RET_HEAD_DIM
MLSTM_QK_W = MLSTM_HEADS * MLSTM_QK_DIM
MLSTM_V_W = MLSTM_HEADS * MLSTM_V_DIM
MOBA_W = MOBA_HEADS * MOBA_HEAD_DIM

LANES = 128
SUBLANES = 8
VMEM_LIMIT_BYTES = 56 * 1024 * 1024

_OFF_RQ = 0
_OFF_MI = 4 * RET_W + 2 * MLSTM_QK_W + 2 * MLSTM_V_W
_OFF_BQ = _OFF_MI + 2 * MLSTM_HEADS
_OFF_GATES = _OFF_BQ + 3 * MOBA_W
_N_HEADMAJOR = _OFF_MI + 3 * MOBA_W
_BLK_RQ, _BLK_RK, _BLK_RV, _BLK_RG = 0, 8, 16, 24
_BLK_MQ, _BLK_MK, _BLK_MV, _BLK_MO = 32, 36, 40, 48
_BLK_BQ, _BLK_BK, _BLK_BV = 56, 64, 72

MOBA_KEY_SPLIT = 1
MOBA_LOGITS_LEAD = 2
RET_HEADS_PER_STEP = 1
RET_SKEW = 1
NEG_BIG = -1e30
LOG2E = 1.0 / math.log(2.0)

_TILES = {
    "in_proj": dict(tm=1024, tn=1280),
    "gates": dict(tm=1024, tn=2048),
    "merge": dict(tm=256),
    "ffn": dict(tm=512, tf=1024),
    "pack": dict(tr=1024),
    "cast": dict(tr=1024, tc=2048),
}

BF16 = jnp.bfloat16
F32 = jnp.float32


def _cparams(sem):
    return pltpu.CompilerParams(dimension_semantics=sem, vmem_limit_bytes=VMEM_LIMIT_BYTES)


def _rms_rows(xf, w_row):
    ms = jnp.mean(xf * xf, axis=-1, keepdims=True)
    return xf * lax.rsqrt(ms + EPS) * w_row


def _norm_rows_to(x_ref, nw_ref, hn_ref, row_chunk):
    def body(c, carry):
        r = pl.multiple_of(c * row_chunk, row_chunk)
        hn_ref[pl.ds(r, row_chunk), :] = _rms_rows(x_ref[pl.ds(r, row_chunk), :],
                                                   nw_ref[...]).astype(BF16)
        return carry
    lax.fori_loop(0, x_ref.shape[0] // row_chunk, body, 0)


def _in_proj_kernel(x_ref, nw_ref, w_ref, ws_ref, o_ref, hn_ref, os_ref, *, row_chunk):
    @pl.when(pl.program_id(1) == 0)
    def _():
        _norm_rows_to(x_ref, nw_ref, hn_ref, row_chunk)
        os_ref[...] = jnp.dot(hn_ref[...], ws_ref[...], preferred_element_type=F32)

    acc = jnp.dot(hn_ref[...], w_ref[...], preferred_element_type=F32)
    for c in range(o_ref.shape[0]):
        o_ref[c] = acc[:, c * LANES:(c + 1) * LANES].astype(o_ref.dtype)


def _in_proj(x2, nw, w_cat, layer, w_side, *, tm, tn):
    T, D = x2.shape
    N = _N_HEADMAJOR
    NS = w_side.shape[2]
    return pl.pallas_call(
        functools.partial(_in_proj_kernel, row_chunk=128),
        out_shape=(jax.ShapeDtypeStruct((N // LANES, T, LANES), BF16),
                   jax.ShapeDtypeStruct((T, D), BF16),
                   jax.ShapeDtypeStruct((T, NS), F32)),
        grid=(T // tm, N // tn),
        in_specs=[pl.BlockSpec((tm, D), lambda i, j: (i, 0)),
                  pl.BlockSpec((1, D), lambda i, j: (0, 0)),
                  pl.BlockSpec((None, D, tn), lambda i, j: (layer, 0, j)),
                  pl.BlockSpec((None, D, NS), lambda i, j: (layer, 0, 0))],
        out_specs=(pl.BlockSpec((tn // LANES, tm, LANES), lambda i, j: (j, i, 0)),
                   pl.BlockSpec((tm, D), lambda i, j: (i, 0)),
                   pl.BlockSpec((tm, NS), lambda i, j: (i, 0))),
        compiler_params=_cparams(("parallel", "arbitrary")),
        name="in_proj",
    )(x2, nw, w_cat, w_side)


def _matmul_kernel(h_ref, w_ref, o_ref):
    o_ref[...] = jnp.dot(h_ref[...], w_ref[...], preferred_element_type=F32).astype(o_ref.dtype)


def _matmul(hn, w_cat, layer, col0, n_cols, *, tm, tn, out_dtype, name):
    T, D = hn.shape
    assert col0 % tn == 0 and n_cols % tn == 0
    return pl.pallas_call(
        _matmul_kernel,
        out_shape=jax.ShapeDtypeStruct((T, n_cols), out_dtype),
        grid=(T // tm, n_cols // tn),
        in_specs=[pl.BlockSpec((tm, D), lambda i, j: (i, 0)),
                  pl.BlockSpec((None, D, tn), lambda i, j: (layer, 0, col0 // tn + j))],
        out_specs=pl.BlockSpec((tm, tn), lambda i, j: (i, j)),
        compiler_params=_cparams(("parallel", "parallel")),
        name=name,
    )(hn, w_cat)


def _cast_kernel(w_ref, o_ref, *, row_chunk):
    def body(c, carry):
        r = pl.multiple_of(c * row_chunk, row_chunk)
        o_ref[pl.ds(r, row_chunk), :] = w_ref[pl.ds(r, row_chunk), :].astype(BF16)
        return carry
    lax.fori_loop(0, w_ref.shape[0] // row_chunk, body, 0)


def _cast_bf16(w3, *, tr, tc):
    Lw, R, C = w3.shape
    return pl.pallas_call(
        functools.partial(_cast_kernel, row_chunk=128),
        out_shape=jax.ShapeDtypeStruct((Lw, R, C), BF16),
        grid=(Lw, R // tr, C // tc),
        in_specs=[pl.BlockSpec((None, tr, tc), lambda l, i, j: (l, i, j))],
        out_specs=pl.BlockSpec((None, tr, tc), lambda l, i, j: (l, i, j)),
        compiler_params=_cparams(("parallel", "parallel", "parallel")),
        name="cast_bf16",
    )(w3)


_PACK_TC = 1024
_PACK_SHIFT = 2 * MLSTM_HEADS
_PACK_ALIGNED_TILES = _OFF_MI // _PACK_TC


def _pack_w_in_kernel(a_ref, b_ref, s_ref, o_ref, g_ref):
    j = pl.program_id(2)
    tc, tr = a_ref.shape
    blk = LANES

    def emit(shift):
        for rb in range(tc // blk):
            lo, hi = rb * blk + shift, (rb + 1) * blk + shift
            for cb in range(tr // blk):
                cols = slice(cb * blk, (cb + 1) * blk)
                if hi <= tc:
                    x = a_ref[lo:hi, cols]
                else:
                    x = jnp.concatenate([a_ref[lo:tc, cols], b_ref[0:hi - tc, cols]], axis=0)
                y = x.T * s_ref[:, rb * blk:(rb + 1) * blk]
                o_ref[cols, rb * blk:(rb + 1) * blk] = y.astype(BF16)

    @pl.when(j < _PACK_ALIGNED_TILES)
    def _():
        emit(0)

    @pl.when(j >= _PACK_ALIGNED_TILES)
    def _():
        emit(_PACK_SHIFT)

    @pl.when(j == _PACK_ALIGNED_TILES)
    def _():
        zeros = jnp.zeros((blk - _PACK_SHIFT, blk), F32)
        for cb in range(tr // blk):
            cols = slice(cb * blk, (cb + 1) * blk)
            x = jnp.concatenate([a_ref[0:_PACK_SHIFT, cols], zeros], axis=0)
            g_ref[cols, :] = x.T.astype(BF16)


def _pack_w_in(w_in_t, scale_row, *, tr):
    depth, d_in, D = w_in_t.shape
    tc = _PACK_TC
    n_out = d_in - _PACK_SHIFT
    assert _OFF_MI % tc == 0 and n_out % tc == 0 and tc % LANES == 0 and _PACK_SHIFT == SUBLANES
    return pl.pallas_call(
        _pack_w_in_kernel,
        out_shape=(jax.ShapeDtypeStruct((depth, D, n_out), BF16),
                   jax.ShapeDtypeStruct((depth, D, LANES), BF16)),
        grid=(depth, D // tr, n_out // tc),
        in_specs=[pl.BlockSpec((None, tc, tr), lambda l, i, j: (l, j, i)),
                  pl.BlockSpec((None, SUBLANES, tr),
                               lambda l, i, j: (l, (j + 1) * (tc // SUBLANES), i)),
                  pl.BlockSpec((1, tc), lambda l, i, j: (0, j))],
        out_specs=(pl.BlockSpec((None, tr, tc), lambda l, i, j: (l, i, j)),
                   pl.BlockSpec((None, tr, LANES), lambda l, i, j: (l, i, 0))),
        compiler_params=_cparams(("parallel", "parallel", "arbitrary")),
        name="pack_w_in",
    )(w_in_t, w_in_t, scale_row)


def _retention_kernel(q_ref, k_ref, v_ref, g_ref, cos_ref, sin_ref, decay_ref, xi_ref, zeta_ref,
                      o_ref):
    L = RET_CHUNK
    dh = RET_HEAD_DIM
    HP, S = q_ref.shape[0], q_ref.shape[1]

    def rope(x, cos, sin_signed):
        return x * cos + pltpu.roll(x, dh // 2, 1) * sin_signed

    def state_free(hh, n):
        rows = slice(n * L, (n + 1) * L)
        cos = cos_ref[rows, :]
        sin = sin_ref[rows, :]
        q = rope(q_ref[hh, rows, :].astype(F32), cos, sin)
        k = rope(k_ref[hh, rows, :].astype(F32), cos, sin)
        v = v_ref[hh, rows, :]
        qb = q.astype(BF16)
        scores = lax.dot_general(qb, k.astype(BF16), (((1,), (1,)), ((), ())),
                                 preferred_element_type=F32) * decay_ref[hh]
        kv = lax.dot_general((k * zeta_ref[hh]).astype(BF16), v, (((0,), (0,)), ((), ())),
                             preferred_element_type=F32)
        return qb, scores.astype(BF16), v, kv

    def finish(hh, n, parts, r_prev):
        qb, scores, v, kv = parts
        rows = slice(n * L, (n + 1) * L)
        xi = xi_ref[hh]
        intra = jnp.dot(scores, v, preferred_element_type=F32)
        inter = jnp.dot(qb, r_prev.astype(BF16), preferred_element_type=F32) * xi
        o = intra + inter
        o = o * lax.rsqrt(jnp.mean(o * o, axis=-1, keepdims=True) + EPS)
        g = g_ref[hh, rows, :].astype(F32)
        o_ref[rows, hh * dh:(hh + 1) * dh] = (o * (g * jax.nn.sigmoid(g))).astype(o_ref.dtype)
        return r_prev * xi[L - 1:L, :] + kv

    n_chunks = S // L
    for hh in range(HP):
        r_prev = jnp.zeros((dh, dh), F32)
        pending = [state_free(hh, n) for n in range(min(RET_SKEW, n_chunks))]
        for n in range(n_chunks):
            if n + RET_SKEW < n_chunks:
                pending.append(state_free(hh, n + RET_SKEW))
            r_prev = finish(hh, n, pending.pop(0), r_prev)


def _retention(proj, cos_t, sin_t, decay, xi_b, zeta_b, B, S):
    H, dh, L, HP = RET_HEADS, RET_HEAD_DIM, RET_CHUNK, RET_HEADS_PER_STEP

    def pspec(base):
        return pl.BlockSpec((HP, S, LANES), lambda b, h, base=base: (base // HP + h, b, 0))

    def tspec(width):
        return pl.BlockSpec((HP, L, width), lambda b, h: (h, 0, 0))

    assert H % HP == 0 and all(blk % HP == 0 for blk in (_BLK_RQ, _BLK_RK, _BLK_RV, _BLK_RG))
    return pl.pallas_call(
        _retention_kernel,
        out_shape=jax.ShapeDtypeStruct((B * S, RET_W), BF16),
        grid=(B, H // HP),
        in_specs=[pspec(_BLK_RQ), pspec(_BLK_RK), pspec(_BLK_RV), pspec(_BLK_RG),
                  pl.BlockSpec((S, dh), lambda b, h: (0, 0)),
                  pl.BlockSpec((S, dh), lambda b, h: (0, 0)),
                  tspec(L), tspec(dh), tspec(dh)],
        out_specs=pl.BlockSpec((S, HP * dh), lambda b, h: (b, h)),
        compiler_params=_cparams(("parallel", "parallel")),
        name="retention",
    )(proj, proj, proj, proj, cos_t, sin_t, decay, xi_b, zeta_b)


def _mlstm_kernel(q_ref, k_ref, v_ref, og_ref, gp_ref, gb_ref, cwq_ref, cwk_ref, cbq_ref, cbk_ref,
                  nw_ref, o_ref, qpad_ref, kpad_ref, a_ref, bc_ref, at_ref, bt_ref):
    L = MLSTM_CHUNK
    S = q_ref.shape[0]
    NH = MLSTM_HEADS
    KC = MLSTM_CONV
    h = pl.program_id(1)
    n_chunks = S // L

    qpad_ref[0:SUBLANES, :] = jnp.zeros((SUBLANES, LANES), F32)
    kpad_ref[0:SUBLANES, :] = jnp.zeros((SUBLANES, LANES), F32)

    def conv_silu(src_ref, pad_ref, cw_ref, cb_ref, c, scale):
        pad_ref[SUBLANES + c * L:SUBLANES + (c + 1) * L, :] = src_ref[c * L:(c + 1) * L, :].astype(F32)
        cw = cw_ref[...]
        acc = cb_ref[...]
        for t in range(KC):
            r0 = SUBLANES + c * L - (KC - 1) + t
            acc = acc + pad_ref[r0:r0 + L, :] * cw[t:t + 1, :]
        return acc * jax.nn.sigmoid(acc) * scale

    lane = lax.broadcasted_iota(jnp.int32, (L, LANES), 1)
    sub = lax.broadcasted_iota(jnp.int32, (L, LANES), 0)
    sub_col = lax.broadcasted_iota(jnp.int32, (L, 1), 0)
    causal = sub >= lane
    nw = nw_ref[...]

    @pl.when(h == 0)
    def _():
        tri = (sub >= lane).astype(F32)
        is_f = (lane >= NH) & (lane < 2 * NH)
        for n in range(n_chunks):
            rows = slice(n * L, (n + 1) * L)
            gp = gp_ref[rows, :] + gb_ref[...]
            logsig = jnp.minimum(gp, 0.0) - jnp.log1p(jnp.exp(-jnp.abs(gp)))
            a = jnp.where(is_f, logsig, gp)
            bc = jnp.dot(tri, a, preferred_element_type=F32, precision=lax.Precision.HIGHEST)
            a_ref[rows, :] = a
            bc_ref[rows, :] = bc
            at_ref[n * SUBLANES:(n + 1) * SUBLANES, :] = a.T[0:SUBLANES, :]
            bt_ref[n * SUBLANES:(n + 1) * SUBLANES, :] = bc.T[0:SUBLANES, :]

    def state_free(n):
        rows = slice(n * L, (n + 1) * L)
        li_col = jnp.sum(jnp.where(lane == h, a_ref[rows, :], 0.0), axis=1, keepdims=True)
        b_col = jnp.sum(jnp.where(lane == h + NH, bc_ref[rows, :], 0.0), axis=1, keepdims=True)
        li_row = at_ref[pl.ds(n * SUBLANES + h, 1), :]
        b_row = bt_ref[pl.ds(n * SUBLANES + NH + h, 1), :]
        gtot = jnp.max(jnp.where(sub_col == L - 1, b_col, NEG_BIG), axis=0, keepdims=True)
        q = conv_silu(q_ref, qpad_ref, cwq_ref, cbq_ref, n, MLSTM_QK_DIM ** -0.5)
        k = conv_silu(k_ref, kpad_ref, cwk_ref, cbk_ref, n, 1.0)
        v = jnp.concatenate([v_ref[0, rows, :], v_ref[1, rows, :]], axis=1)
        qb = q.astype(BF16)
        dlog = jnp.where(causal, b_col - b_row + li_row, NEG_BIG)
        dmax = jnp.max(dlog, axis=1, keepdims=True)
        qk = lax.dot_general(qb, k.astype(BF16), (((1,), (1,)), ((), ())),
                             preferred_element_type=F32)
        return dict(li_col=li_col, b_col=b_col, gtot=gtot, q=q, k=k, qb=qb, v=v, dlog=dlog,
                    dmax=dmax, qk=qk)

    def finish(n, p, state):
        c_prev, n_prev, m_prev = state
        rows = slice(n * L, (n + 1) * L)
        inter_log = p["b_col"] + m_prev
        m_pos = jnp.maximum(inter_log, p["dmax"])
        s_intra = p["qk"] * jnp.exp(p["dlog"] - m_pos)
        inter_scale = jnp.exp(inter_log - m_pos)
        num = (jnp.dot(s_intra.astype(BF16), p["v"], preferred_element_type=F32)
               + inter_scale * jnp.dot(p["qb"], c_prev.astype(BF16), preferred_element_type=F32))
        den = (jnp.sum(s_intra, axis=1, keepdims=True)
               + inter_scale * jnp.sum(p["q"] * n_prev, axis=1, keepdims=True))
        inv = 1.0 / jnp.maximum(jnp.abs(den), jnp.exp(-m_pos))
        ms = jnp.mean(num * num, axis=-1, keepdims=True)
        row_scale = inv * lax.rsqrt(inv * inv * ms + EPS)
        og = jnp.concatenate([og_ref[0, rows, :], og_ref[1, rows, :]], axis=1).astype(F32)
        o_ref[rows, :] = (num * row_scale * nw * jax.nn.sigmoid(og)).astype(o_ref.dtype)
        w_loc = p["gtot"] - p["b_col"] + p["li_col"]
        m_loc = jnp.max(w_loc, axis=0, keepdims=True)
        ke = p["k"] * jnp.exp(w_loc - m_loc)
        c_loc = lax.dot_general(ke.astype(BF16), p["v"], (((0,), (0,)), ((), ())),
                                preferred_element_type=F32)
        n_loc = jnp.sum(ke, axis=0, keepdims=True)
        m_new = jnp.maximum(p["gtot"] + m_prev, m_loc)
        a_dec = jnp.exp(p["gtot"] + m_prev - m_new)
        b_inc = jnp.exp(m_loc - m_new)
        return (a_dec * c_prev + b_inc * c_loc, a_dec * n_prev + b_inc * n_loc, m_new)

    state = (jnp.zeros((MLSTM_QK_DIM, MLSTM_V_DIM), F32), jnp.zeros((1, MLSTM_QK_DIM), F32),
             jnp.zeros((1, 1), F32))
    for n in range(n_chunks):
        state = finish(n, state_free(n), state)


def _mlstm(proj, gpre, gate_b_row, conv_w, conv_b, norm_w, B, S):
    NH, dk, dv = MLSTM_HEADS, MLSTM_QK_DIM, MLSTM_V_DIM
    KC = MLSTM_CONV
    nvb = dv // LANES

    def pspec(base):
        return pl.BlockSpec((None, S, LANES), lambda b, h, base=base: (base + h, b, 0))

    def pspec2(base):
        return pl.BlockSpec((nvb, S, LANES), lambda b, h, base=base: (base // nvb + h, b, 0))

    return pl.pallas_call(
        _mlstm_kernel,
        out_shape=jax.ShapeDtypeStruct((B * S, MLSTM_V_W), BF16),
        grid=(B, NH),
        in_specs=[pspec(_BLK_MQ), pspec(_BLK_MK), pspec2(_BLK_MV), pspec2(_BLK_MO),
                  pl.BlockSpec((S, LANES), lambda b, h: (b, 0)),
                  pl.BlockSpec((1, LANES), lambda b, h: (0, 0)),
                  pl.BlockSpec((KC, dk), lambda b, h: (0, h)),
                  pl.BlockSpec((KC, dk), lambda b, h: (0, NH + h)),
                  pl.BlockSpec((1, dk), lambda b, h: (0, h)),
                  pl.BlockSpec((1, dk), lambda b, h: (0, NH + h)),
                  pl.BlockSpec((1, dv), lambda b, h: (0, h))],
        out_specs=pl.BlockSpec((S, dv), lambda b, h: (b, h)),
        scratch_shapes=[pltpu.VMEM((S + SUBLANES, dk), F32),
                        pltpu.VMEM((S + SUBLANES, dk), F32),
                        pltpu.VMEM((S, LANES), F32),
                        pltpu.VMEM((S, LANES), F32),
                        pltpu.VMEM((S // MLSTM_CHUNK * SUBLANES, MLSTM_CHUNK), F32),
                        pltpu.VMEM((S // MLSTM_CHUNK * SUBLANES, MLSTM_CHUNK), F32)],
        compiler_params=_cparams(("parallel", "arbitrary")),
        name="mlstm",
    )(proj, proj, proj, proj, gpre, gate_b_row, conv_w, conv_w, conv_b, conv_b, norm_w)


def _moba_kernel(q_ref, k_ref, v_ref, bvec_ref, far_ref, o_ref, kmean_ref, gate_ref, lg_ref,
                 bias_ref):
    BS = MOBA_BLOCK
    S = k_ref.shape[0]
    NB = S // BS
    nt = (((1,), (1,)), ((), ()))

    toep = pltpu.roll(jnp.broadcast_to(bvec_ref[...], (BS, 2 * BS)), 0, 1, stride=1, stride_axis=0)
    bias_ref[0] = toep[:, 0:BS]
    bias_ref[1] = toep[:, BS:2 * BS]

    kmean_ref[...] = jnp.zeros_like(kmean_ref)
    ones_rows = jnp.ones((2 * SUBLANES, BS), BF16)
    vts = []

    def prepare_key_block(j):
        kj = k_ref[j * BS:(j + 1) * BS, :].astype(F32)
        kmean_ref[j:j + 1, :] = jnp.mean(kj, axis=0, keepdims=True)
        vts.append(jnp.concatenate([v_ref[j * BS:(j + 1) * BS, :].astype(F32).T.astype(BF16),
                                    ones_rows], axis=0))

    dh = MOBA_HEAD_DIM
    far = far_ref[...]
    kl = lax.broadcasted_iota(jnp.int32, (BS, BS), 0)
    ql = lax.broadcasted_iota(jnp.int32, (BS, BS), 1)
    causal = kl <= ql

    def select_rows(i, q):
        if i <= MOBA_TOPK:
            return [None] * i
        gate_ref[i % 2] = lax.dot_general(kmean_ref[...], q.astype(F32), nt,
                                          preferred_element_type=F32,
                                          precision=lax.Precision.HIGHEST)
        rows = [gate_ref[i % 2, n:n + 1, :] for n in range(i)]
        add_rows = []
        for n in range(i):
            rank = jnp.zeros_like(rows[n])
            for mth in range(i):
                if mth == n:
                    continue
                ahead = rows[mth] > rows[n]
                if mth < n:
                    ahead = ahead | (rows[mth] == rows[n])
                rank = rank + jnp.where(ahead, 1.0, 0.0)
            add_rows.append(jnp.where(rank < MOBA_TOPK, 0.0, NEG_BIG))
        return add_rows

    def merge(a, b, op):
        return b if a is None else op(a, b)

    KT = BS // MOBA_KEY_SPLIT

    def logits_tile(i, j, q, add_rows):
        m = None
        for part in range(MOBA_KEY_SPLIT):
            ks = slice(part * KT, (part + 1) * KT)
            lg = lax.dot_general(k_ref[j * BS + part * KT:j * BS + (part + 1) * KT, :], q, nt,
                                 preferred_element_type=F32)
            if j == i:
                lg = jnp.where(causal[ks, :], lg + bias_ref[0, ks, :], NEG_BIG)
            elif j == i - 1:
                lg = lg + bias_ref[1, ks, :]
                if add_rows[j] is not None:
                    lg = lg + add_rows[j]
            else:
                lg = lg + (far if add_rows[j] is None else far + add_rows[j])
            lg_ref[i % 2, j, ks, :] = lg
            m = merge(m, jnp.max(lg, axis=0, keepdims=True), jnp.maximum)
        return m

    def value_tile(i, j, m):
        pv = None
        for part in range(MOBA_KEY_SPLIT):
            ks = slice(part * KT, (part + 1) * KT)
            p = jnp.exp2(lg_ref[i % 2, j, ks, :] - m)
            pv = merge(pv, jnp.dot(vts[j][:, ks], p.astype(BF16), preferred_element_type=F32),
                       jnp.add)
        return pv

    m_cur = logits_tile(0, 0, q_ref[0:BS, :], [])
    for i in range(NB):
        prepare_key_block(i)
        m_next = None
        if i + 1 < NB:
            q_next = q_ref[(i + 1) * BS:(i + 2) * BS, :]
            rows_next = select_rows(i + 1, q_next)
        acc = None
        lead = MOBA_LOGITS_LEAD
        if i + 1 < NB:
            for j in range(min(lead, i + 2)):
                m_next = merge(m_next, logits_tile(i + 1, j, q_next, rows_next), jnp.maximum)
        for j in range(i + 1):
            if i + 1 < NB and j + lead < i + 2:
                m_next = jnp.maximum(m_next, logits_tile(i + 1, j + lead, q_next, rows_next))
            acc = merge(acc, value_tile(i, j, m_cur), jnp.add)
        l = jnp.max(acc[dh:dh + SUBLANES, :], axis=0, keepdims=True)
        out_t = acc[0:dh, :] * (1.0 / l)
        o_ref[i * BS:(i + 1) * BS, :] = out_t.T.astype(o_ref.dtype)
        m_cur = m_next


def _moba(proj, bias_tiles, far_rows, B, S):
    H, dh, BS = MOBA_HEADS, MOBA_HEAD_DIM, MOBA_BLOCK
    NB = S // BS

    def pspec(base):
        return pl.BlockSpec((None, S, LANES), lambda b, h, base=base: (base + h, b, 0))

    return pl.pallas_call(
        _moba_kernel,
        out_shape=jax.ShapeDtypeStruct((B * S, MOBA_W), BF16),
        grid=(B, H),
        in_specs=[pspec(_BLK_BQ), pspec(_BLK_BK), pspec(_BLK_BV),
                  pl.BlockSpec((None, 1, 2 * BS), lambda b, h: (h, 0, 0)),
                  pl.BlockSpec((None, 1, BS), lambda b, h: (h, 0, 0))],
        out_specs=pl.BlockSpec((S, dh), lambda b, h: (b, h)),
        scratch_shapes=[pltpu.VMEM((NB, dh), F32),
                        pltpu.VMEM((2, NB, BS), F32),
                        pltpu.VMEM((2, NB, BS, BS), F32),
                        pltpu.VMEM((2, BS, BS), F32)],
        compiler_params=_cparams(("parallel", "parallel")),
        name="moba",
    )(proj, proj, proj, bias_tiles, far_rows)


def _merge_kernel(yr_ref, ym_ref, yb_ref, gr_ref, gm_ref, gb_ref, x_ref,
                  wr_ref, wm_ref, wb_ref, wo_ref, o_ref):
    def gate(g_ref):
        return jax.nn.sigmoid(g_ref[...].astype(F32))

    mixed = gate(gr_ref) * jnp.dot(yr_ref[...], wr_ref[...], preferred_element_type=F32)
    mixed = mixed + gate(gm_ref) * jnp.dot(ym_ref[...], wm_ref[...], preferred_element_type=F32)
    mixed = mixed + gate(gb_ref) * jnp.dot(yb_ref[...], wb_ref[...], preferred_element_type=F32)
    o_ref[...] = x_ref[...] + jnp.dot(mixed.astype(BF16), wo_ref[...], preferred_element_type=F32)


def _merge(y_ret, y_ml, y_mb, gate_pre, x2, w_r, w_m, w_b, w_o, layer, *, tm):
    T, D = x2.shape

    def resident(shape):
        return pl.BlockSpec((None,) + shape, lambda i: (layer, 0, 0), pipeline_mode=pl.Buffered(1))

    return pl.pallas_call(
        _merge_kernel,
        out_shape=jax.ShapeDtypeStruct((T, D), F32),
        grid=(T // tm,),
        in_specs=[pl.BlockSpec((tm, RET_W), lambda i: (i, 0)),
                  pl.BlockSpec((tm, MLSTM_V_W), lambda i: (i, 0)),
                  pl.BlockSpec((tm, MOBA_W), lambda i: (i, 0)),
                  pl.BlockSpec((tm, D), lambda i: (i, 0)),
                  pl.BlockSpec((tm, D), lambda i: (i, 1)),
                  pl.BlockSpec((tm, D), lambda i: (i, 2)),
                  pl.BlockSpec((tm, D), lambda i: (i, 0)),
                  resident((RET_W, D)), resident((MLSTM_V_W, D)), resident((MOBA_W, D)),
                  resident((D, D))],
        out_specs=pl.BlockSpec((tm, D), lambda i: (i, 0)),
        compiler_params=_cparams(("parallel",)),
        name="merge",
    )(y_ret, y_ml, y_mb, gate_pre, gate_pre, gate_pre, x2, w_r, w_m, w_b, w_o)


def _ffn_kernel(x_ref, xn_ref, nw_ref, fw_ref, w1_ref, w2_ref, o_ref, hn_ref, *, row_chunk,
                final_norm):
    i = pl.program_id(0)
    j = pl.program_id(1)
    rows_next = xn_ref.shape[0]
    slot = i % 2

    @pl.when((i == 0) & (j == 0))
    def _():
        def body(c, carry):
            r = pl.multiple_of(c * row_chunk, row_chunk)
            hn_ref[0, pl.ds(r, row_chunk), :] = _rms_rows(x_ref[pl.ds(r, row_chunk), :],
                                                          nw_ref[...]).astype(BF16)
            return carry
        lax.fori_loop(0, x_ref.shape[0] // row_chunk, body, 0)

    def step():
        u = jnp.dot(hn_ref[slot], w1_ref[...], preferred_element_type=F32)
        u = jnp.maximum(u, 0.0)
        y = jnp.dot((u * u).astype(BF16), w2_ref[...], preferred_element_type=F32)
        r = pl.multiple_of(j * rows_next, rows_next)
        hn_ref[1 - slot, pl.ds(r, rows_next), :] = _rms_rows(xn_ref[...], nw_ref[...]).astype(BF16)
        return y

    @pl.when(j == 0)
    def _():
        o_ref[...] = x_ref[...] + step()

    @pl.when(j > 0)
    def _():
        o_ref[...] += step()

    if final_norm:
        @pl.when(j == pl.num_programs(1) - 1)
        def _():
            def body(c, carry):
                r = pl.multiple_of(c * row_chunk, row_chunk)
                o_ref[pl.ds(r, row_chunk), :] = _rms_rows(o_ref[pl.ds(r, row_chunk), :], fw_ref[...])
                return carry
            lax.fori_loop(0, x_ref.shape[0] // row_chunk, body, 0)


def _ffn(x2, nw, fw, w1, w2, layer, *, tm, tf, final_norm):
    T, D = x2.shape
    F = w1.shape[2]
    n_i, n_j = T // tm, F // tf
    rows_next = tm // n_j
    assert rows_next * n_j == tm and rows_next % (2 * SUBLANES) == 0
    return pl.pallas_call(
        functools.partial(_ffn_kernel, row_chunk=128, final_norm=final_norm),
        out_shape=jax.ShapeDtypeStruct((T, D), F32),
        grid=(n_i, n_j),
        in_specs=[pl.BlockSpec((tm, D), lambda i, j: (i, 0)),
                  pl.BlockSpec((rows_next, D),
                               lambda i, j: (jnp.minimum(i + 1, n_i - 1) * n_j + j, 0)),
                  pl.BlockSpec((1, D), lambda i, j: (0, 0)),
                  pl.BlockSpec((1, D), lambda i, j: (0, 0)),
                  pl.BlockSpec((None, D, tf), lambda i, j: (layer, 0, j)),
                  pl.BlockSpec((None, tf, D), lambda i, j: (layer, j, 0))],
        out_specs=pl.BlockSpec((tm, D), lambda i, j: (i, 0)),
        scratch_shapes=[pltpu.VMEM((2, tm, D), BF16)],
        compiler_params=_cparams(("arbitrary", "arbitrary")),
        name="ffn",
    )(x2, x2, nw, fw, w1, w2)


def _rope_tables(S):
    half = RET_HEAD_DIM // 2
    inv = ROPE_BASE ** (-jnp.arange(half, dtype=F32) / half)
    ang = jnp.arange(S).astype(F32)[:, None] * inv[None, :]
    cos, sin = jnp.cos(ang), jnp.sin(ang)
    return jnp.concatenate([cos, cos], axis=-1), jnp.concatenate([-sin, sin], axis=-1)


def _retention_tables():
    H, L, dh = RET_HEADS, RET_CHUNK, RET_HEAD_DIM
    log_gamma = jnp.log1p(-jnp.exp2(-5.0 - jnp.arange(H, dtype=F32)))
    idx = jnp.arange(L, dtype=F32)
    diff = idx[:, None] - idx[None, :]
    decay = jnp.where(diff >= 0, jnp.exp(jnp.maximum(diff, 0.0) * log_gamma[:, None, None]), 0.0)
    zeta = jnp.exp((L - 1 - idx)[None, :] * log_gamma[:, None])
    xi = jnp.exp((idx + 1.0)[None, :] * log_gamma[:, None])
    return (decay, jnp.broadcast_to(xi[:, :, None], (H, L, dh)),
            jnp.broadcast_to(zeta[:, :, None], (H, L, dh)))


def _t5_bucket(dist):
    n = jnp.maximum(dist, 0)
    exact = REL_BUCKETS // 2
    nf = jnp.maximum(n, 1).astype(F32)
    large = exact + (jnp.log(nf / exact) / math.log(REL_MAX_DIST / exact)
                     * (REL_BUCKETS - exact)).astype(jnp.int32)
    large = jnp.minimum(large, REL_BUCKETS - 1)
    return jnp.where(n < exact, n, large)


def _moba_bias_tables(rel_bias, S):
    BS = MOBA_BLOCK
    assert BS + 1 >= REL_MAX_DIST and S % BS == 0
    table_t = rel_bias.T.astype(F32) * LOG2E
    onehot = (_t5_bucket(jnp.arange(2 * BS))[:, None] == jnp.arange(REL_BUCKETS)).astype(F32)
    by_dist = jnp.einsum("db,hb->hd", onehot, table_t, precision=lax.Precision.HIGHEST)
    far = table_t[:, REL_BUCKETS - 1]
    far_rows = jnp.broadcast_to(far[:, None, None], (far.shape[0], 1, BS))
    return by_dist[:, None, :], far_rows


def kernel(x, w_in, mlstm_gate_b, mlstm_conv_w, mlstm_conv_b, mlstm_norm_w, w_branch_ret,
           w_branch_mlstm, w_branch_moba, w_out, norm_mix_w, norm_mlp_w, w_ff1, w_ff2, rel_bias,
           final_norm_w):
    B, S, D = x.shape
    T = B * S
    depth = w_in.shape[0]
    x2 = x.reshape(T, D)

    cos_t, sin_t = _rope_tables(S)
    decay, xi_b, zeta_b = _retention_tables()
    bias_by_dist, far_rows = _moba_bias_tables(rel_bias, S)
    final_w = final_norm_w.reshape(1, D)

    scale_row = np.ones((1, w_in.shape[2] - _PACK_SHIFT), np.float32)
    scale_row[:, _OFF_RQ:_OFF_RQ + RET_W] = RET_HEAD_DIM ** -0.5
    scale_row[:, _OFF_MI:_OFF_MI + MOBA_W] = MOBA_HEAD_DIM ** -0.5 * LOG2E
    w_cat, w_gpre = _pack_w_in(jnp.swapaxes(w_in, 1, 2), jnp.asarray(scale_row), **_TILES["pack"])
    n_gate_cols = w_cat.shape[2] - _N_HEADMAJOR
    w_r, w_m, w_b, w_o, w_1, w_2 = [_cast_bf16(w, **_TILES["cast"]) for w in
                                    (w_branch_ret, w_branch_mlstm, w_branch_moba, w_out, w_ff1, w_ff2)]

    for l in range(depth):
        proj, hn, gpre = _in_proj(x2, norm_mix_w[l].reshape(1, D), w_cat, l, w_gpre,
                                  **_TILES["in_proj"])
        gate_pre = _matmul(hn, w_cat, l, _N_HEADMAJOR, n_gate_cols, out_dtype=BF16,
                           name="branch_gates", **_TILES["gates"])

        y_ret = _retention(proj, cos_t, sin_t, decay, xi_b, zeta_b, B, S)
        gate_b_row = jnp.pad(mlstm_gate_b[l].reshape(1, 2 * MLSTM_HEADS),
                             ((0, 0), (0, LANES - 2 * MLSTM_HEADS)))
        y_ml = _mlstm(proj, gpre, gate_b_row, mlstm_conv_w[l], mlstm_conv_b[l].reshape(1, -1),
                      mlstm_norm_w[l].reshape(1, -1), B, S)
        y_mb = _moba(proj, bias_by_dist, far_rows, B, S)

        x2 = _merge(y_ret, y_ml, y_mb, gate_pre, x2, w_r, w_m, w_b, w_o, l, **_TILES["merge"])
        x2 = _ffn(x2, norm_mlp_w[l].reshape(1, D), final_w, w_1, w_2, l,
                  final_norm=(l == depth - 1), **_TILES["ffn"])

    return x2.reshape(B, S, D)
```

```python
import functools
import math

import numpy as np
import jax
import jax.numpy as jnp
from jax import lax
from jax.experimental import pallas as pl
from jax.experimental.pallas import tpu as pltpu

D_MODEL = 2048
RET_HEADS = 8
RET_HEAD_DIM = 128
RET_CHUNK = 128
MLSTM_HEADS = 4
MLSTM_QK_DIM = 128
MLSTM_V_DIM = 256
MLSTM_CHUNK = 128
MLSTM_CONV = 4
MOBA_HEADS = 8
MOBA_HEAD_DIM = 128
MOBA_BLOCK = 256
MOBA_TOPK = 3
REL_BUCKETS = 32
REL_MAX_DIST = 128
D_FF = 4 * D_MODEL
ROPE_BASE = 10000.0
EPS = 1e-6

RET_W = RET_HEADS * RET_HEAD_DIM
MLSTM_QK_W = MLSTM_HEADS * MLSTM_QK_DIM
MLSTM_V_W = MLSTM_HEADS * MLSTM_V_DIM
MOBA_W = MOBA_HEADS * MOBA_HEAD_DIM

LANES = 128
SUBLANES = 8
VMEM_LIMIT_BYTES = 56 * 1024 * 1024

_OFF_RQ = 0
_OFF_MI = 4 * RET_W + 2 * MLSTM_QK_W + 2 * MLSTM_V_W
_OFF_BQ = _OFF_MI + 2 * MLSTM_HEADS
_OFF_GATES = _OFF_BQ + 3 * MOBA_W
_N_HEADMAJOR = _OFF_MI + 3 * MOBA_W
_BLK_RQ, _BLK_RK, _BLK_RV, _BLK_RG = 0, 8, 16, 24
_BLK_MQ, _BLK_MK, _BLK_MV, _BLK_MO = 32, 36, 40, 48
_BLK_BQ, _BLK_BK, _BLK_BV = 56, 64, 72

MOBA_KEY_SPLIT = 1
MOBA_LOGITS_LEAD = 2
RET_HEADS_PER_STEP = 1
RET_SKEW = 1
NEG_BIG = -1e30
LOG2E = 1.0 / math.log(2.0)

_TILES = {
    "in_proj": dict(tm=1024, tn=1280),
    "gates": dict(tm=1024, tn=2048),
    "merge": dict(tm=256),
    "ffn": dict(tm=512, tf=1024),
    "pack": dict(tr=1024),
    "cast": dict(tr=1024, tc=2048),
}

BF16 = jnp.bfloat16
F32 = jnp.float32


def _cparams(sem):
    return pltpu.CompilerParams(dimension_semantics=sem, vmem_limit_bytes=VMEM_LIMIT_BYTES)


def _rms_rows(xf, w_row):
    ms = jnp.mean(xf * xf, axis=-1, keepdims=True)
    return xf * lax.rsqrt(ms + EPS) * w_row


def _norm_rows_to(x_ref, nw_ref, hn_ref, row_chunk):
    def body(c, carry):
        r = pl.multiple_of(c * row_chunk, row_chunk)
        hn_ref[pl.ds(r, row_chunk), :] = _rms_rows(x_ref[pl.ds(r, row_chunk), :],
                                                   nw_ref[...]).astype(BF16)
        return carry
    lax.fori_loop(0, x_ref.shape[0] // row_chunk, body, 0)


def _in_proj_kernel(xs_ref, nw_ref, w_ref, ws_ref, o_ref, hn_ref, os_ref, hn_buf):
    t = pl.program_id(0)
    j = pl.program_id(1)
    rows_s = xs_ref.shape[0]
    slot = t % 2

    def norm_slice():
        r = pl.multiple_of(j * rows_s, rows_s)
        h = _rms_rows(xs_ref[...], nw_ref[...]).astype(BF16)
        hn_buf[slot, pl.ds(r, rows_s), :] = h
        hn_ref[pl.ds(r, rows_s), :] = h

    @pl.when(t == 0)
    def _():
        norm_slice()

    @pl.when(t > 0)
    def _():
        @pl.when(j == 0)
        def _():
            os_ref[...] = jnp.dot(hn_buf[1 - slot], ws_ref[...], preferred_element_type=F32)

        acc = jnp.dot(hn_buf[1 - slot], w_ref[...], preferred_element_type=F32)
        for c in range(o_ref.shape[0]):
            o_ref[c] = acc[:, c * LANES:(c + 1) * LANES].astype(o_ref.dtype)
        norm_slice()


def _in_proj(x2, nw, w_cat, layer, w_side, *, tm, tn):
    T, D = x2.shape
    N = _N_HEADMAJOR
    NS = w_side.shape[2]
    n_i, n_j = T // tm, N // tn
    rows_s = tm // n_j
    assert rows_s * n_j == tm and rows_s % (2 * SUBLANES) == 0
    return pl.pallas_call(
        _in_proj_kernel,
        out_shape=(jax.ShapeDtypeStruct((N // LANES, T, LANES), BF16),
                   jax.ShapeDtypeStruct((T, D), BF16),
                   jax.ShapeDtypeStruct((T, NS), F32)),
        grid=(n_i + 1, n_j),
        in_specs=[pl.BlockSpec((rows_s, D), lambda t, j: (jnp.minimum(t, n_i - 1) * n_j + j, 0)),
                  pl.BlockSpec((1, D), lambda t, j: (0, 0)),
                  pl.BlockSpec((None, D, tn), lambda t, j: (layer, 0, j)),
                  pl.BlockSpec((None, D, NS), lambda t, j: (layer, 0, 0))],
        out_specs=(pl.BlockSpec((tn // LANES, tm, LANES),
                                lambda t, j: (jnp.where(t == 0, 0, j), jnp.maximum(t - 1, 0), 0)),
                   pl.BlockSpec((tm, D), lambda t, j: (jnp.minimum(t, n_i - 1), 0)),
                   pl.BlockSpec((tm, NS), lambda t, j: (jnp.maximum(t - 1, 0), 0))),
        scratch_shapes=[pltpu.VMEM((2, tm, D), BF16)],
        compiler_params=_cparams(("arbitrary", "arbitrary")),
        name="in_proj",
    )(x2, nw, w_cat, w_side)


def _matmul_kernel(h_ref, w_ref, o_ref):
    o_ref[...] = jnp.dot(h_ref[...], w_ref[...], preferred_element_type=F32).astype(o_ref.dtype)


def _matmul(hn, w_cat, layer, col0, n_cols, *, tm, tn, out_dtype, name):
    T, D = hn.shape
    assert col0 % tn == 0 and n_cols % tn == 0
    return pl.pallas_call(
        _matmul_kernel,
        out_shape=jax.ShapeDtypeStruct((T, n_cols), out_dtype),
        grid=(T // tm, n_cols // tn),
        in_specs=[pl.BlockSpec((tm, D), lambda i, j: (i, 0)),
                  pl.BlockSpec((None, D, tn), lambda i, j: (layer, 0, col0 // tn + j))],
        out_specs=pl.BlockSpec((tm, tn), lambda i, j: (i, j)),
        compiler_params=_cparams(("parallel", "parallel")),
        name=name,
    )(hn, w_cat)


def _cast_kernel(w_ref, o_ref, *, row_chunk):
    def body(c, carry):
        r = pl.multiple_of(c * row_chunk, row_chunk)
        o_ref[pl.ds(r, row_chunk), :] = w_ref[pl.ds(r, row_chunk), :].astype(BF16)
        return carry
    lax.fori_loop(0, w_ref.shape[0] // row_chunk, body, 0)


def _cast_bf16(w3, *, tr, tc):
    Lw, R, C = w3.shape
    return pl.pallas_call(
        functools.partial(_cast_kernel, row_chunk=128),
        out_shape=jax.ShapeDtypeStruct((Lw, R, C), BF16),
        grid=(Lw, R // tr, C // tc),
        in_specs=[pl.BlockSpec((None, tr, tc), lambda l, i, j: (l, i, j))],
        out_specs=pl.BlockSpec((None, tr, tc), lambda l, i, j: (l, i, j)),
        compiler_params=_cparams(("parallel", "parallel", "parallel")),
        name="cast_bf16",
    )(w3)


_PACK_TC = 1024
_PACK_SHIFT = 2 * MLSTM_HEADS
_PACK_ALIGNED_TILES = _OFF_MI // _PACK_TC


def _pack_w_in_kernel(a_ref, b_ref, s_ref, o_ref, g_ref):
    j = pl.program_id(2)
    tc, tr = a_ref.shape
    blk = LANES

    def emit(shift):
        for rb in range(tc // blk):
            lo, hi = rb * blk + shift, (rb + 1) * blk + shift
            for cb in range(tr // blk):
                cols = slice(cb * blk, (cb + 1) * blk)
                if hi <= tc:
                    x = a_ref[lo:hi, cols]
                else:
                    x = jnp.concatenate([a_ref[lo:tc, cols], b_ref[0:hi - tc, cols]], axis=0)
                y = x.T * s_ref[:, rb * blk:(rb + 1) * blk]
                o_ref[cols, rb * blk:(rb + 1) * blk] = y.astype(BF16)

    @pl.when(j < _PACK_ALIGNED_TILES)
    def _():
        emit(0)

    @pl.when(j >= _PACK_ALIGNED_TILES)
    def _():
        emit(_PACK_SHIFT)

    @pl.when(j == _PACK_ALIGNED_TILES)
    def _():
        zeros = jnp.zeros((blk - _PACK_SHIFT, blk), F32)
        for cb in range(tr // blk):
            cols = slice(cb * blk, (cb + 1) * blk)
            x = jnp.concatenate([a_ref[0:_PACK_SHIFT, cols], zeros], axis=0)
            g_ref[cols, :] = x.T.astype(BF16)


def _pack_w_in(w_in_t, scale_row, *, tr):
    depth, d_in, D = w_in_t.shape
    tc = _PACK_TC
    n_out = d_in - _PACK_SHIFT
    assert _OFF_MI % tc == 0 and n_out % tc == 0 and tc % LANES == 0 and _PACK_SHIFT == SUBLANES
    return pl.pallas_call(
        _pack_w_in_kernel,
        out_shape=(jax.ShapeDtypeStruct((depth, D, n_out), BF16),
                   jax.ShapeDtypeStruct((depth, D, LANES), BF16)),
        grid=(depth, D // tr, n_out // tc),
        in_specs=[pl.BlockSpec((None, tc, tr), lambda l, i, j: (l, j, i)),
                  pl.BlockSpec((None, SUBLANES, tr),
                               lambda l, i, j: (l, (j + 1) * (tc // SUBLANES), i)),
                  pl.BlockSpec((1, tc), lambda l, i, j: (0, j))],
        out_specs=(pl.BlockSpec((None, tr, tc), lambda l, i, j: (l, i, j)),
                   pl.BlockSpec((None, tr, LANES), lambda l, i, j: (l, i, 0))),
        compiler_params=_cparams(("parallel", "parallel", "arbitrary")),
        name="pack_w_in",
    )(w_in_t, w_in_t, scale_row)


def _retention_kernel(q_ref, k_ref, v_ref, g_ref, cos_ref, sin_ref, decay_ref, xi_ref, zeta_ref,
                      o_ref):
    L = RET_CHUNK
    dh = RET_HEAD_DIM
    HP, S = q_ref.shape[0], q_ref.shape[1]

    def rope(x, cos, sin_signed):
        return x * cos + pltpu.roll(x, dh // 2, 1) * sin_signed

    def state_free(hh, n):
        rows = slice(n * L, (n + 1) * L)
        cos = cos_ref[rows, :]
        sin = sin_ref[rows, :]
        q = rope(q_ref[hh, rows, :].astype(F32), cos, sin)
        k = rope(k_ref[hh, rows, :].astype(F32), cos, sin)
        v = v_ref[hh, rows, :]
        qb = q.astype(BF16)
        scores = lax.dot_general(qb, k.astype(BF16), (((1,), (1,)), ((), ())),
                                 preferred_element_type=F32) * decay_ref[hh]
        kv = lax.dot_general((k * zeta_ref[hh]).astype(BF16), v, (((0,), (0,)), ((), ())),
                             preferred_element_type=F32)
        return qb, scores.astype(BF16), v, kv

    def finish(hh, n, parts, r_prev):
        qb, scores, v, kv = parts
        rows = slice(n * L, (n + 1) * L)
        xi = xi_ref[hh]
        intra = jnp.dot(scores, v, preferred_element_type=F32)
        inter = jnp.dot(qb, r_prev.astype(BF16), preferred_element_type=F32) * xi
        o = intra + inter
        o = o * lax.rsqrt(jnp.mean(o * o, axis=-1, keepdims=True) + EPS)
        g = g_ref[hh, rows, :].astype(F32)
        o_ref[rows, hh * dh:(hh + 1) * dh] = (o * (g * jax.nn.sigmoid(g))).astype(o_ref.dtype)
        return r_prev * xi[L - 1:L, :] + kv

    n_chunks = S // L
    for hh in range(HP):
        r_prev = jnp.zeros((dh, dh), F32)
        pending = [state_free(hh, n) for n in range(min(RET_SKEW, n_chunks))]
        for n in range(n_chunks):
            if n + RET_SKEW < n_chunks:
                pending.append(state_free(hh, n + RET_SKEW))
            r_prev = finish(hh, n, pending.pop(0), r_prev)


def _retention(proj, cos_t, sin_t, decay, xi_b, zeta_b, B, S):
    H, dh, L, HP = RET_HEADS, RET_HEAD_DIM, RET_CHUNK, RET_HEADS_PER_STEP

    def pspec(base):
        return pl.BlockSpec((HP, S, LANES), lambda b, h, base=base: (base // HP + h, b, 0))

    def tspec(width):
        return pl.BlockSpec((HP, L, width), lambda b, h: (h, 0, 0))

    assert H % HP == 0 and all(blk % HP == 0 for blk in (_BLK_RQ, _BLK_RK, _BLK_RV, _BLK_RG))
    return pl.pallas_call(
        _retention_kernel,
        out_shape=jax.ShapeDtypeStruct((B * S, RET_W), BF16),
        grid=(B, H // HP),
        in_specs=[pspec(_BLK_RQ), pspec(_BLK_RK), pspec(_BLK_RV), pspec(_BLK_RG),
                  pl.BlockSpec((S, dh), lambda b, h: (0, 0)),
                  pl.BlockSpec((S, dh), lambda b, h: (0, 0)),
                  tspec(L), tspec(dh), tspec(dh)],
        out_specs=pl.BlockSpec((S, HP * dh), lambda b, h: (b, h)),
        compiler_params=_cparams(("parallel", "parallel")),
        name="retention",
    )(proj, proj, proj, proj, cos_t, sin_t, decay, xi_b, zeta_b)


def _mlstm_kernel(q_ref, k_ref, v_ref, og_ref, gp_ref, gb_ref, cwq_ref, cwk_ref, cbq_ref, cbk_ref,
                  nw_ref, o_ref, qpad_ref, kpad_ref, a_ref, bc_ref, at_ref, bt_ref):
    L = MLSTM_CHUNK
    S = q_ref.shape[0]
    NH = MLSTM_HEADS
    KC = MLSTM_CONV
    h = pl.program_id(1)
    n_chunks = S // L

    qpad_ref[0:SUBLANES, :] = jnp.zeros((SUBLANES, LANES), F32)
    kpad_ref[0:SUBLANES, :] = jnp.zeros((SUBLANES, LANES), F32)

    def conv_silu(src_ref, pad_ref, cw_ref, cb_ref, c, scale):
        pad_ref[SUBLANES + c * L:SUBLANES + (c + 1) * L, :] = src_ref[c * L:(c + 1) * L, :].astype(F32)
        cw = cw_ref[...]
        acc = cb_ref[...]
        for t in range(KC):
            r0 = SUBLANES + c * L - (KC - 1) + t
            acc = acc + pad_ref[r0:r0 + L, :] * cw[t:t + 1, :]
        return acc * jax.nn.sigmoid(acc) * scale

    lane = lax.broadcasted_iota(jnp.int32, (L, LANES), 1)
    sub = lax.broadcasted_iota(jnp.int32, (L, LANES), 0)
    sub_col = lax.broadcasted_iota(jnp.int32, (L, 1), 0)
    causal = sub >= lane
    nw = nw_ref[...]

    @pl.when(h == 0)
    def _():
        tri = (sub >= lane).astype(F32)
        is_f = (lane >= NH) & (lane < 2 * NH)
        for n in range(n_chunks):
            rows = slice(n * L, (n + 1) * L)
            gp = gp_ref[rows, :] + gb_ref[...]
            logsig = jnp.minimum(gp, 0.0) - jnp.log1p(jnp.exp(-jnp.abs(gp)))
            a = jnp.where(is_f, logsig, gp)
            bc = jnp.dot(tri, a, preferred_element_type=F32, precision=lax.Precision.HIGHEST)
            a_ref[rows, :] = a
            bc_ref[rows, :] = bc
            at_ref[n * SUBLANES:(n + 1) * SUBLANES, :] = a.T[0:SUBLANES, :]
            bt_ref[n * SUBLANES:(n + 1) * SUBLANES, :] = bc.T[0:SUBLANES, :]

    def state_free(n):
        rows = slice(n * L, (n + 1) * L)
        li_col = jnp.sum(jnp.where(lane == h, a_ref[rows, :], 0.0), axis=1, keepdims=True)
        b_col = jnp.sum(jnp.where(lane == h + NH, bc_ref[rows, :], 0.0), axis=1, keepdims=True)
        li_row = at_ref[pl.ds(n * SUBLANES + h, 1), :]
        b_row = bt_ref[pl.ds(n * SUBLANES + NH + h, 1), :]
        gtot = jnp.max(jnp.where(sub_col == L - 1, b_col, NEG_BIG), axis=0, keepdims=True)
        q = conv_silu(q_ref, qpad_ref, cwq_ref, cbq_ref, n, MLSTM_QK_DIM ** -0.5)
        k = conv_silu(k_ref, kpad_ref, cwk_ref, cbk_ref, n, 1.0)
        v = jnp.concatenate([v_ref[0, rows, :], v_ref[1, rows, :]], axis=1)
        qb = q.astype(BF16)
        dlog = jnp.where(causal, b_col - b_row + li_row, NEG_BIG)
        dmax = jnp.max(dlog, axis=1, keepdims=True)
        qk = lax.dot_general(qb, k.astype(BF16), (((1,), (1,)), ((), ())),
                             preferred_element_type=F32)
        return dict(li_col=li_col, b_col=b_col, gtot=gtot, q=q, k=k, qb=qb, v=v, dlog=dlog,
                    dmax=dmax, qk=qk)

    def finish(n, p, state):
        c_prev, n_prev, m_prev = state
        rows = slice(n * L, (n + 1) * L)
        inter_log = p["b_col"] + m_prev
        m_pos = jnp.maximum(inter_log, p["dmax"])
        s_intra = p["qk"] * jnp.exp(p["dlog"] - m_pos)
        inter_scale = jnp.exp(inter_log - m_pos)
        num = (jnp.dot(s_intra.astype(BF16), p["v"], preferred_element_type=F32)
               + inter_scale * jnp.dot(p["qb"], c_prev.astype(BF16), preferred_element_type=F32))
        den = (jnp.sum(s_intra, axis=1, keepdims=True)
               + inter_scale * jnp.sum(p["q"] * n_prev, axis=1, keepdims=True))
        inv = 1.0 / jnp.maximum(jnp.abs(den), jnp.exp(-m_pos))
        ms = jnp.mean(num * num, axis=-1, keepdims=True)
        row_scale = inv * lax.rsqrt(inv * inv * ms + EPS)
        og = jnp.concatenate([og_ref[0, rows, :], og_ref[1, rows, :]], axis=1).astype(F32)
        o_ref[rows, :] = (num * row_scale * nw * jax.nn.sigmoid(og)).astype(o_ref.dtype)
        w_loc = p["gtot"] - p["b_col"] + p["li_col"]
        m_loc = jnp.max(w_loc, axis=0, keepdims=True)
        ke = p["k"] * jnp.exp(w_loc - m_loc)
        c_loc = lax.dot_general(ke.astype(BF16), p["v"], (((0,), (0,)), ((), ())),
                                preferred_element_type=F32)
        n_loc = jnp.sum(ke, axis=0, keepdims=True)
        m_new = jnp.maximum(p["gtot"] + m_prev, m_loc)
        a_dec = jnp.exp(p["gtot"] + m_prev - m_new)
        b_inc = jnp.exp(m_loc - m_new)
        return (a_dec * c_prev + b_inc * c_loc, a_dec * n_prev + b_inc * n_loc, m_new)

    state = (jnp.zeros((MLSTM_QK_DIM, MLSTM_V_DIM), F32), jnp.zeros((1, MLSTM_QK_DIM), F32),
             jnp.zeros((1, 1), F32))
    for n in range(n_chunks):
        state = finish(n, state_free(n), state)


def _mlstm(proj, gpre, gate_b_row, conv_w, conv_b, norm_w, B, S):
    NH, dk, dv = MLSTM_HEADS, MLSTM_QK_DIM, MLSTM_V_DIM
    KC = MLSTM_CONV
    nvb = dv // LANES

    def pspec(base):
        return pl.BlockSpec((None, S, LANES), lambda b, h, base=base: (base + h, b, 0))

    def pspec2(base):
        return pl.BlockSpec((nvb, S, LANES), lambda b, h, base=base: (base // nvb + h, b, 0))

    return pl.pallas_call(
        _mlstm_kernel,
        out_shape=jax.ShapeDtypeStruct((B * S, MLSTM_V_W), BF16),
        grid=(B, NH),
        in_specs=[pspec(_BLK_MQ), pspec(_BLK_MK), pspec2(_BLK_MV), pspec2(_BLK_MO),
                  pl.BlockSpec((S, LANES), lambda b, h: (b, 0)),
                  pl.BlockSpec((1, LANES), lambda b, h: (0, 0)),
                  pl.BlockSpec((KC, dk), lambda b, h: (0, h)),
                  pl.BlockSpec((KC, dk), lambda b, h: (0, NH + h)),
                  pl.BlockSpec((1, dk), lambda b, h: (0, h)),
                  pl.BlockSpec((1, dk), lambda b, h: (0, NH + h)),
                  pl.BlockSpec((1, dv), lambda b, h: (0, h))],
        out_specs=pl.BlockSpec((S, dv), lambda b, h: (b, h)),
        scratch_shapes=[pltpu.VMEM((S + SUBLANES, dk), F32),
                        pltpu.VMEM((S + SUBLANES, dk), F32),
                        pltpu.VMEM((S, LANES), F32),
                        pltpu.VMEM((S, LANES), F32),
                        pltpu.VMEM((S // MLSTM_CHUNK * SUBLANES, MLSTM_CHUNK), F32),
                        pltpu.VMEM((S // MLSTM_CHUNK * SUBLANES, MLSTM_CHUNK), F32)],
        compiler_params=_cparams(("parallel", "arbitrary")),
        name="mlstm",
    )(proj, proj, proj, proj, gpre, gate_b_row, conv_w, conv_w, conv_b, conv_b, norm_w)


def _moba_kernel(q_ref, k_ref, v_ref, bvec_ref, far_ref, o_ref, kmean_ref, gate_ref, lg_ref,
                 bias_ref):
    BS = MOBA_BLOCK
    S = k_ref.shape[0]
    NB = S // BS
    nt = (((1,), (1,)), ((), ()))

    toep = pltpu.roll(jnp.broadcast_to(bvec_ref[...], (BS, 2 * BS)), 0, 1, stride=1, stride_axis=0)
    bias_ref[0] = toep[:, 0:BS]
    bias_ref[1] = toep[:, BS:2 * BS]

    kmean_ref[...] = jnp.zeros_like(kmean_ref)
    ones_rows = jnp.ones((2 * SUBLANES, BS), BF16)
    vts = []

    def prepare_key_block(j):
        kj = k_ref[j * BS:(j + 1) * BS, :].astype(F32)
        kmean_ref[j:j + 1, :] = jnp.mean(kj, axis=0, keepdims=True)
        vts.append(jnp.concatenate([v_ref[j * BS:(j + 1) * BS, :].astype(F32).T.astype(BF16),
                                    ones_rows], axis=0))

    dh = MOBA_HEAD_DIM
    far = far_ref[...]
    kl = lax.broadcasted_iota(jnp.int32, (BS, BS), 0)
    ql = lax.broadcasted_iota(jnp.int32, (BS, BS), 1)
    causal = kl <= ql

    def select_rows(i, q):
        if i <= MOBA_TOPK:
            return [None] * i
        gate_ref[i % 2] = lax.dot_general(kmean_ref[...], q.astype(F32), nt,
                                          preferred_element_type=F32,
                                          precision=lax.Precision.HIGHEST)
        rows = [gate_ref[i % 2, n:n + 1, :] for n in range(i)]
        add_rows = []
        for n in range(i):
            rank = jnp.zeros_like(rows[n])
            for mth in range(i):
                if mth == n:
                    continue
                ahead = rows[mth] > rows[n]
                if mth < n:
                    ahead = ahead | (rows[mth] == rows[n])
                rank = rank + jnp.where(ahead, 1.0, 0.0)
            add_rows.append(jnp.where(rank < MOBA_TOPK, 0.0, NEG_BIG))
        return add_rows

    def merge(a, b, op):
        return b if a is None else op(a, b)

    KT = BS // MOBA_KEY_SPLIT

    def logits_tile(i, j, q, add_rows):
        m = None
        for part in range(MOBA_KEY_SPLIT):
            ks = slice(part * KT, (part + 1) * KT)
            lg = lax.dot_general(k_ref[j * BS + part * KT:j * BS + (part + 1) * KT, :], q, nt,
                                 preferred_element_type=F32)
            if j == i:
                lg = jnp.where(causal[ks, :], lg + bias_ref[0, ks, :], NEG_BIG)
            elif j == i - 1:
                lg = lg + bias_ref[1, ks, :]
                if add_rows[j] is not None:
                    lg = lg + add_rows[j]
            else:
                lg = lg + (far if add_rows[j] is None else far + add_rows[j])
            lg_ref[i % 2, j, ks, :] = lg
            m = merge(m, jnp.max(lg, axis=0, keepdims=True), jnp.maximum)
        return m

    def value_tile(i, j, m):
        pv = None
        for part in range(MOBA_KEY_SPLIT):
            ks = slice(part * KT, (part + 1) * KT)
            p = jnp.exp2(lg_ref[i % 2, j, ks, :] - m)
            pv = merge(pv, jnp.dot(vts[j][:, ks], p.astype(BF16), preferred_element_type=F32),
                       jnp.add)
        return pv

    m_cur = logits_tile(0, 0, q_ref[0:BS, :], [])
    for i in range(NB):
        prepare_key_block(i)
        m_next = None
        if i + 1 < NB:
            q_next = q_ref[(i + 1) * BS:(i + 2) * BS, :]
            rows_next = select_rows(i + 1, q_next)
        acc = None
        lead = MOBA_LOGITS_LEAD
        if i + 1 < NB:
            for j in range(min(lead, i + 2)):
                m_next = merge(m_next, logits_tile(i + 1, j, q_next, rows_next), jnp.maximum)
        for j in range(i + 1):
            if i + 1 < NB and j + lead < i + 2:
                m_next = jnp.maximum(m_next, logits_tile(i + 1, j + lead, q_next, rows_next))
            acc = merge(acc, value_tile(i, j, m_cur), jnp.add)
        l = jnp.max(acc[dh:dh + SUBLANES, :], axis=0, keepdims=True)
        out_t = acc[0:dh, :] * (1.0 / l)
        o_ref[i * BS:(i + 1) * BS, :] = out_t.T.astype(o_ref.dtype)
        m_cur = m_next


def _moba(proj, bias_tiles, far_rows, B, S):
    H, dh, BS = MOBA_HEADS, MOBA_HEAD_DIM, MOBA_BLOCK
    NB = S // BS

    def pspec(base):
        return pl.BlockSpec((None, S, LANES), lambda b, h, base=base: (base + h, b, 0))

    return pl.pallas_call(
        _moba_kernel,
        out_shape=jax.ShapeDtypeStruct((B * S, MOBA_W), BF16),
        grid=(B, H),
        in_specs=[pspec(_BLK_BQ), pspec(_BLK_BK), pspec(_BLK_BV),
                  pl.BlockSpec((None, 1, 2 * BS), lambda b, h: (h, 0, 0)),
                  pl.BlockSpec((None, 1, BS), lambda b, h: (h, 0, 0))],
        out_specs=pl.BlockSpec((S, dh), lambda b, h: (b, h)),
        scratch_shapes=[pltpu.VMEM((NB, dh), F32),
                        pltpu.VMEM((2, NB, BS), F32),
                        pltpu.VMEM((2, NB, BS, BS), F32),
                        pltpu.VMEM((2, BS, BS), F32)],
        compiler_params=_cparams(("parallel", "parallel")),
        name="moba",
    )(proj, proj, proj, bias_tiles, far_rows)


def _merge_kernel(yr_ref, ym_ref, yb_ref, gr_ref, gm_ref, gb_ref, x_ref,
                  wr_ref, wm_ref, wb_ref, wo_ref, o_ref):
    def gate(g_ref):
        return jax.nn.sigmoid(g_ref[...].astype(F32))

    mixed = gate(gr_ref) * jnp.dot(yr_ref[...], wr_ref[...], preferred_element_type=F32)
    mixed = mixed + gate(gm_ref) * jnp.dot(ym_ref[...], wm_ref[...], preferred_element_type=F32)
    mixed = mixed + gate(gb_ref) * jnp.dot(yb_ref[...], wb_ref[...], preferred_element_type=F32)
    o_ref[...] = x_ref[...] + jnp.dot(mixed.astype(BF16), wo_ref[...], preferred_element_type=F32)


def _merge(y_ret, y_ml, y_mb, gate_pre, x2, w_r, w_m, w_b, w_o, layer, *, tm):
    T, D = x2.shape

    def resident(shape):
        return pl.BlockSpec((None,) + shape, lambda i: (layer, 0, 0), pipeline_mode=pl.Buffered(1))

    return pl.pallas_call(
        _merge_kernel,
        out_shape=jax.ShapeDtypeStruct((T, D), F32),
        grid=(T // tm,),
        in_specs=[pl.BlockSpec((tm, RET_W), lambda i: (i, 0)),
                  pl.BlockSpec((tm, MLSTM_V_W), lambda i: (i, 0)),
                  pl.BlockSpec((tm, MOBA_W), lambda i: (i, 0)),
                  pl.BlockSpec((tm, D), lambda i: (i, 0)),
                  pl.BlockSpec((tm, D), lambda i: (i, 1)),
                  pl.BlockSpec((tm, D), lambda i: (i, 2)),
                  pl.BlockSpec((tm, D), lambda i: (i, 0)),
                  resident((RET_W, D)), resident((MLSTM_V_W, D)), resident((MOBA_W, D)),
                  resident((D, D))],
        out_specs=pl.BlockSpec((tm, D), lambda i: (i, 0)),
        compiler_params=_cparams(("parallel",)),
        name="merge",
    )(y_ret, y_ml, y_mb, gate_pre, gate_pre, gate_pre, x2, w_r, w_m, w_b, w_o)


def _ffn_kernel(x_ref, xn_ref, nw_ref, fw_ref, w1_ref, w2_ref, o_ref, hn_ref, *, row_chunk,
                final_norm):
    i = pl.program_id(0)
    j = pl.program_id(1)
    rows_next = xn_ref.shape[0]
    slot = i % 2

    @pl.when((i == 0) & (j == 0))
    def _():
        def body(c, carry):
            r = pl.multiple_of(c * row_chunk, row_chunk)
            hn_ref[0, pl.ds(r, row_chunk), :] = _rms_rows(x_ref[pl.ds(r, row_chunk), :],
                                                          nw_ref[...]).astype(BF16)
            return carry
        lax.fori_loop(0, x_ref.shape[0] // row_chunk, body, 0)

    def step():
        u = jnp.dot(hn_ref[slot], w1_ref[...], preferred_element_type=F32)
        u = jnp.maximum(u, 0.0)
        y = jnp.dot((u * u).astype(BF16), w2_ref[...], preferred_element_type=F32)
        r = pl.multiple_of(j * rows_next, rows_next)
        hn_ref[1 - slot, pl.ds(r, rows_next), :] = _rms_rows(xn_ref[...], nw_ref[...]).astype(BF16)
        return y

    @pl.when(j == 0)
    def _():
        o_ref[...] = x_ref[...] + step()

    @pl.when(j > 0)
    def _():
        o_ref[...] += step()

    if final_norm:
        @pl.when(j == pl.num_programs(1) - 1)
        def _():
            def body(c, carry):
                r = pl.multiple_of(c * row_chunk, row_chunk)
                o_ref[pl.ds(r, row_chunk), :] = _rms_rows(o_ref[pl.ds(r, row_chunk), :], fw_ref[...])
                return carry
            lax.fori_loop(0, x_ref.shape[0] // row_chunk, body, 0)


def _ffn(x2, nw, fw, w1, w2, layer, *, tm, tf, final_norm):
    T, D = x2.shape
    F = w1.shape[2]
    n_i, n_j = T // tm, F // tf
    rows_next = tm // n_j
    assert rows_next * n_j == tm and rows_next % (2 * SUBLANES) == 0
    return pl.pallas_call(
        functools.partial(_ffn_kernel, row_chunk=128, final_norm=final_norm),
        out_shape=jax.ShapeDtypeStruct((T, D), F32),
        grid=(n_i, n_j),
        in_specs=[pl.BlockSpec((tm, D), lambda i, j: (i, 0)),
                  pl.BlockSpec((rows_next, D),
                               lambda i, j: (jnp.minimum(i + 1, n_i - 1) * n_j + j, 0)),
                  pl.BlockSpec((1, D), lambda i, j: (0, 0)),
                  pl.BlockSpec((1, D), lambda i, j: (0, 0)),
                  pl.BlockSpec((None, D, tf), lambda i, j: (layer, 0, j)),
                  pl.BlockSpec((None, tf, D), lambda i, j: (layer, j, 0))],
        out_specs=pl.BlockSpec((tm, D), lambda i, j: (i, 0)),
        scratch_shapes=[pltpu.VMEM((2, tm, D), BF16)],
        compiler_params=_cparams(("arbitrary", "arbitrary")),
        name="ffn",
    )(x2, x2, nw, fw, w1, w2)


def _rope_tables(S):
    half = RET_HEAD_DIM // 2
    inv = ROPE_BASE ** (-jnp.arange(half, dtype=F32) / half)
    ang = jnp.arange(S).astype(F32)[:, None] * inv[None, :]
    cos, sin = jnp.cos(ang), jnp.sin(ang)
    return jnp.concatenate([cos, cos], axis=-1), jnp.concatenate([-sin, sin], axis=-1)


def _retention_tables():
    H, L, dh = RET_HEADS, RET_CHUNK, RET_HEAD_DIM
    log_gamma = jnp.log1p(-jnp.exp2(-5.0 - jnp.arange(H, dtype=F32)))
    idx = jnp.arange(L, dtype=F32)
    diff = idx[:, None] - idx[None, :]
    decay = jnp.where(diff >= 0, jnp.exp(jnp.maximum(diff, 0.0) * log_gamma[:, None, None]), 0.0)
    zeta = jnp.exp((L - 1 - idx)[None, :] * log_gamma[:, None])
    xi = jnp.exp((idx + 1.0)[None, :] * log_gamma[:, None])
    return (decay, jnp.broadcast_to(xi[:, :, None], (H, L, dh)),
            jnp.broadcast_to(zeta[:, :, None], (H, L, dh)))


def _t5_bucket(dist):
    n = jnp.maximum(dist, 0)
    exact = REL_BUCKETS // 2
    nf = jnp.maximum(n, 1).astype(F32)
    large = exact + (jnp.log(nf / exact) / math.log(REL_MAX_DIST / exact)
                     * (REL_BUCKETS - exact)).astype(jnp.int32)
    large = jnp.minimum(large, REL_BUCKETS - 1)
    return jnp.where(n < exact, n, large)


def _moba_bias_tables(rel_bias, S):
    BS = MOBA_BLOCK
    assert BS + 1 >= REL_MAX_DIST and S % BS == 0
    table_t = rel_bias.T.astype(F32) * LOG2E
    onehot = (_t5_bucket(jnp.arange(2 * BS))[:, None] == jnp.arange(REL_BUCKETS)).astype(F32)
    by_dist = jnp.einsum("db,hb->hd", onehot, table_t, precision=lax.Precision.HIGHEST)
    far = table_t[:, REL_BUCKETS - 1]
    far_rows = jnp.broadcast_to(far[:, None, None], (far.shape[0], 1, BS))
    return by_dist[:, None, :], far_rows


def kernel(x, w_in, mlstm_gate_b, mlstm_conv_w, mlstm_conv_b, mlstm_norm_w, w_branch_ret,
           w_branch_mlstm, w_branch_moba, w_out, norm_mix_w, norm_mlp_w, w_ff1, w_ff2, rel_bias,
           final_norm_w):
    B, S, D = x.shape
    T = B * S
    depth = w_in.shape[0]
    x2 = x.reshape(T, D)

    cos_t, sin_t = _rope_tables(S)
    decay, xi_b, zeta_b = _retention_tables()
    bias_by_dist, far_rows = _moba_bias_tables(rel_bias, S)
    final_w = final_norm_w.reshape(1, D)

    scale_row = np.ones((1, w_in.shape[2] - _PACK_SHIFT), np.float32)
    scale_row[:, _OFF_RQ:_OFF_RQ + RET_W] = RET_HEAD_DIM ** -0.5
    scale_row[:, _OFF_MI:_OFF_MI + MOBA_W] = MOBA_HEAD_DIM ** -0.5 * LOG2E
    w_cat, w_gpre = _pack_w_in(jnp.swapaxes(w_in, 1, 2), jnp.asarray(scale_row), **_TILES["pack"])
    n_gate_cols = w_cat.shape[2] - _N_HEADMAJOR
    w_r, w_m, w_b, w_o, w_1, w_2 = [_cast_bf16(w, **_TILES["cast"]) for w in
                                    (w_branch_ret, w_branch_mlstm, w_branch_moba, w_out, w_ff1, w_ff2)]

    for l in range(depth):
        proj, hn, gpre = _in_proj(x2, norm_mix_w[l].reshape(1, D), w_cat, l, w_gpre,
                                  **_TILES["in_proj"])
        gate_pre = _matmul(hn, w_cat, l, _N_HEADMAJOR, n_gate_cols, out_dtype=BF16,
                           name="branch_gates", **_TILES["gates"])

        y_ret = _retention(proj, cos_t, sin_t, decay, xi_b, zeta_b, B, S)
        gate_b_row = jnp.pad(mlstm_gate_b[l].reshape(1, 2 * MLSTM_HEADS),
                             ((0, 0), (0, LANES - 2 * MLSTM_HEADS)))
        y_ml = _mlstm(proj, gpre, gate_b_row, mlstm_conv_w[l], mlstm_conv_b[l].reshape(1, -1),
                      mlstm_norm_w[l].reshape(1, -1), B, S)
        y_mb = _moba(proj, bias_by_dist, far_rows, B, S)

        x2 = _merge(y_ret, y_ml, y_mb, gate_pre, x2, w_r, w_m, w_b, w_o, l, **_TILES["merge"])
        x2 = _ffn(x2, norm_mlp_w[l].reshape(1, D), final_w, w_1, w_2, l,
                  final_norm=(l == depth - 1), **_TILES["ffn"])

    return x2.reshape(B, S, D)
```

```python
import functools
import math

import numpy as np
import jax
import jax.numpy as jnp
from jax import lax
from jax.experimental import pallas as pl
from jax.experimental.pallas import tpu as pltpu

D_MODEL = 2048
RET_HEADS = 8
RET_HEAD_DIM = 128
RET_CHUNK = 128
MLSTM_HEADS = 4
MLSTM_QK_DIM = 128
MLSTM_V_DIM = 256
MLSTM_CHUNK = 128
MLSTM_CONV = 4
MOBA_HEADS = 8
MOBA_HEAD_DIM = 128
MOBA_BLOCK = 256
MOBA_TOPK = 3
REL_BUCKETS = 32
REL_MAX_DIST = 128
D_FF = 4 * D_MODEL
ROPE_BASE = 10000.0
EPS = 1e-6

RET_W = RET_HEADS * RET_HEAD_DIM
MLSTM_QK_W = MLSTM_HEADS * MLSTM_QK_DIM
MLSTM_V_W = MLSTM_HEADS * MLSTM_V_DIM
MOBA_W = MOBA_HEADS * MOBA_HEAD_DIM

LANES = 128
SUBLANES = 8
VMEM_LIMIT_BYTES = 56 * 1024 * 1024

_OFF_RQ = 0
_OFF_MI = 4 * RET_W + 2 * MLSTM_QK_W + 2 * MLSTM_V_W
_OFF_BQ = _OFF_MI + 2 * MLSTM_HEADS
_OFF_GATES = _OFF_BQ + 3 * MOBA_W
_N_HEADMAJOR = _OFF_MI + 3 * MOBA_W
_BLK_RQ, _BLK_RK, _BLK_RV, _BLK_RG = 0, 8, 16, 24
_BLK_MQ, _BLK_MK, _BLK_MV, _BLK_MO = 32, 36, 40, 48
_BLK_BQ, _BLK_BK, _BLK_BV = 56, 64, 72

MOBA_KEY_SPLIT = 1
MOBA_LOGITS_LEAD = 2
RET_HEADS_PER_STEP = 1
RET_SKEW = 1
NEG_BIG = -1e30
LOG2E = 1.0 / math.log(2.0)

_TILES = {
    "in_proj": dict(tm=1024, tn=1280),
    "gates": dict(tm=1024, tn=2048),
    "merge": dict(tm=256),
    "ffn": dict(tm=512, tf=1024),
    "pack": dict(tr=1024),
    "cast": dict(tr=1024, tc=2048),
}

BF16 = jnp.bfloat16
F32 = jnp.float32


def _cparams(sem):
    return pltpu.CompilerParams(dimension_semantics=sem, vmem_limit_bytes=VMEM_LIMIT_BYTES)


def _rms_rows(xf, w_row):
    ms = jnp.mean(xf * xf, axis=-1, keepdims=True)
    return xf * lax.rsqrt(ms + EPS) * w_row


def _norm_rows_to(x_ref, nw_ref, hn_ref, row_chunk):
    def body(c, carry):
        r = pl.multiple_of(c * row_chunk, row_chunk)
        hn_ref[pl.ds(r, row_chunk), :] = _rms_rows(x_ref[pl.ds(r, row_chunk), :],
                                                   nw_ref[...]).astype(BF16)
        return carry
    lax.fori_loop(0, x_ref.shape[0] // row_chunk, body, 0)


def _in_proj_kernel(xs_ref, nw_ref, w_ref, ws_ref, o_ref, hn_ref, os_ref, hn_buf):
    t = pl.program_id(0)
    j = pl.program_id(1)
    rows_s = xs_ref.shape[0]
    slot = t % 2

    def norm_slice():
        r = pl.multiple_of(j * rows_s, rows_s)
        h = _rms_rows(xs_ref[...], nw_ref[...]).astype(BF16)
        hn_buf[slot, pl.ds(r, rows_s), :] = h
        hn_ref[pl.ds(r, rows_s), :] = h

    @pl.when(t == 0)
    def _():
        norm_slice()

    @pl.when(t > 0)
    def _():
        @pl.when(j == 0)
        def _():
            os_ref[...] = jnp.dot(hn_buf[1 - slot], ws_ref[...], preferred_element_type=F32)

        acc = jnp.dot(hn_buf[1 - slot], w_ref[...], preferred_element_type=F32)
        for c in range(o_ref.shape[0]):
            o_ref[c] = acc[:, c * LANES:(c + 1) * LANES].astype(o_ref.dtype)
        norm_slice()


def _in_proj(x2, nw, w_cat, layer, w_side, *, tm, tn):
    T, D = x2.shape
    N = _N_HEADMAJOR
    NS = w_side.shape[2]
    n_i, n_j = T // tm, N // tn
    rows_s = tm // n_j
    assert rows_s * n_j == tm and rows_s % (2 * SUBLANES) == 0
    return pl.pallas_call(
        _in_proj_kernel,
        out_shape=(jax.ShapeDtypeStruct((N // LANES, T, LANES), BF16),
                   jax.ShapeDtypeStruct((T, D), BF16),
                   jax.ShapeDtypeStruct((T, NS), F32)),
        grid=(n_i + 1, n_j),
        in_specs=[pl.BlockSpec((rows_s, D), lambda t, j: (jnp.minimum(t, n_i - 1) * n_j + j, 0)),
                  pl.BlockSpec((1, D), lambda t, j: (0, 0)),
                  pl.BlockSpec((None, D, tn), lambda t, j: (layer, 0, j)),
                  pl.BlockSpec((None, D, NS), lambda t, j: (layer, 0, 0))],
        out_specs=(pl.BlockSpec((tn // LANES, tm, LANES),
                                lambda t, j: (jnp.where(t == 0, 0, j), jnp.maximum(t - 1, 0), 0)),
                   pl.BlockSpec((tm, D), lambda t, j: (jnp.minimum(t, n_i - 1), 0)),
                   pl.BlockSpec((tm, NS), lambda t, j: (jnp.maximum(t - 1, 0), 0))),
        scratch_shapes=[pltpu.VMEM((2, tm, D), BF16)],
        compiler_params=_cparams(("arbitrary", "arbitrary")),
        name="in_proj",
    )(x2, nw, w_cat, w_side)


def _matmul_kernel(h_ref, w_ref, o_ref):
    o_ref[...] = jnp.dot(h_ref[...], w_ref[...], preferred_element_type=F32).astype(o_ref.dtype)


def _matmul(hn, w_cat, layer, col0, n_cols, *, tm, tn, out_dtype, name):
    T, D = hn.shape
    assert col0 % tn == 0 and n_cols % tn == 0
    return pl.pallas_call(
        _matmul_kernel,
        out_shape=jax.ShapeDtypeStruct((T, n_cols), out_dtype),
        grid=(T // tm, n_cols // tn),
        in_specs=[pl.BlockSpec((tm, D), lambda i, j: (i, 0)),
                  pl.BlockSpec((None, D, tn), lambda i, j: (layer, 0, col0 // tn + j))],
        out_specs=pl.BlockSpec((tm, tn), lambda i, j: (i, j)),
        compiler_params=_cparams(("parallel", "parallel")),
        name=name,
    )(hn, w_cat)


def _cast_kernel(w_ref, o_ref, *, row_chunk):
    def body(c, carry):
        r = pl.multiple_of(c * row_chunk, row_chunk)
        o_ref[pl.ds(r, row_chunk), :] = w_ref[pl.ds(r, row_chunk), :].astype(BF16)
        return carry
    lax.fori_loop(0, w_ref.shape[0] // row_chunk, body, 0)


def _cast_bf16(w3, *, tr, tc):
    Lw, R, C = w3.shape
    return pl.pallas_call(
        functools.partial(_cast_kernel, row_chunk=128),
        out_shape=jax.ShapeDtypeStruct((Lw, R, C), BF16),
        grid=(Lw, R // tr, C // tc),
        in_specs=[pl.BlockSpec((None, tr, tc), lambda l, i, j: (l, i, j))],
        out_specs=pl.BlockSpec((None, tr, tc), lambda l, i, j: (l, i, j)),
        compiler_params=_cparams(("parallel", "parallel", "parallel")),
        name="cast_bf16",
    )(w3)


_PACK_TC = 1024
_PACK_SHIFT = 2 * MLSTM_HEADS
_PACK_ALIGNED_TILES = _OFF_MI // _PACK_TC


def _pack_w_in_kernel(a_ref, b_ref, s_ref, o_ref, g_ref):
    j = pl.program_id(2)
    tc, tr = a_ref.shape
    blk = LANES

    def emit(shift):
        for rb in range(tc // blk):
            lo, hi = rb * blk + shift, (rb + 1) * blk + shift
            for cb in range(tr // blk):
                cols = slice(cb * blk, (cb + 1) * blk)
                if hi <= tc:
                    x = a_ref[lo:hi, cols]
                else:
                    x = jnp.concatenate([a_ref[lo:tc, cols], b_ref[0:hi - tc, cols]], axis=0)
                y = x.T * s_ref[:, rb * blk:(rb + 1) * blk]
                o_ref[cols, rb * blk:(rb + 1) * blk] = y.astype(BF16)

    @pl.when(j < _PACK_ALIGNED_TILES)
    def _():
        emit(0)

    @pl.when(j >= _PACK_ALIGNED_TILES)
    def _():
        emit(_PACK_SHIFT)

    @pl.when(j == _PACK_ALIGNED_TILES)
    def _():
        zeros = jnp.zeros((blk - _PACK_SHIFT, blk), F32)
        for cb in range(tr // blk):
            cols = slice(cb * blk, (cb + 1) * blk)
            x = jnp.concatenate([a_ref[0:_PACK_SHIFT, cols], zeros], axis=0)
            g_ref[cols, :] = x.T.astype(BF16)


def _pack_w_in(w_in_t, scale_row, *, tr):
    depth, d_in, D = w_in_t.shape
    tc = _PACK_TC
    n_out = d_in - _PACK_SHIFT
    assert _OFF_MI % tc == 0 and n_out % tc == 0 and tc % LANES == 0 and _PACK_SHIFT == SUBLANES
    return pl.pallas_call(
        _pack_w_in_kernel,
        out_shape=(jax.ShapeDtypeStruct((depth, D, n_out), BF16),
                   jax.ShapeDtypeStruct((depth, D, LANES), BF16)),
        grid=(depth, D // tr, n_out // tc),
        in_specs=[pl.BlockSpec((None, tc, tr), lambda l, i, j: (l, j, i)),
                  pl.BlockSpec((None, SUBLANES, tr),
                               lambda l, i, j: (l, (j + 1) * (tc // SUBLANES), i)),
                  pl.BlockSpec((1, tc), lambda l, i, j: (0, j))],
        out_specs=(pl.BlockSpec((None, tr, tc), lambda l, i, j: (l, i, j)),
                   pl.BlockSpec((None, tr, LANES), lambda l, i, j: (l, i, 0))),
        compiler_params=_cparams(("parallel", "parallel", "arbitrary")),
        name="pack_w_in",
    )(w_in_t, w_in_t, scale_row)


def _retention_kernel(q_ref, k_ref, v_ref, g_ref, cos_ref, sin_ref, decay_ref, xi_ref, zeta_ref,
                      o_ref):
    L = RET_CHUNK
    dh = RET_HEAD_DIM
    HP, S = q_ref.shape[0], q_ref.shape[1]

    def rope(x, cos, sin_signed):
        return x * cos + pltpu.roll(x, dh // 2, 1) * sin_signed

    def state_free(hh, n):
        rows = slice(n * L, (n + 1) * L)
        cos = cos_ref[rows, :]
        sin = sin_ref[rows, :]
        q = rope(q_ref[hh, rows, :].astype(F32), cos, sin)
        k = rope(k_ref[hh, rows, :].astype(F32), cos, sin)
        v = v_ref[hh, rows, :]
        qb = q.astype(BF16)
        scores = lax.dot_general(qb, k.astype(BF16), (((1,), (1,)), ((), ())),
                                 preferred_element_type=F32) * decay_ref[hh]
        kv = lax.dot_general((k * zeta_ref[hh]).astype(BF16), v, (((0,), (0,)), ((), ())),
                             preferred_element_type=F32)
        return qb, scores.astype(BF16), v, kv

    def finish(hh, n, parts, r_prev):
        qb, scores, v, kv = parts
        rows = slice(n * L, (n + 1) * L)
        xi = xi_ref[hh]
        intra = jnp.dot(scores, v, preferred_element_type=F32)
        inter = jnp.dot(qb, r_prev.astype(BF16), preferred_element_type=F32) * xi
        o = intra + inter
        o = o * lax.rsqrt(jnp.mean(o * o, axis=-1, keepdims=True) + EPS)
        g = g_ref[hh, rows, :].astype(F32)
        o_ref[rows, hh * dh:(hh + 1) * dh] = (o * (g * jax.nn.sigmoid(g))).astype(o_ref.dtype)
        return r_prev * xi[L - 1:L, :] + kv

    n_chunks = S // L
    for hh in range(HP):
        r_prev = jnp.zeros((dh, dh), F32)
        pending = [state_free(hh, n) for n in range(min(RET_SKEW, n_chunks))]
        for n in range(n_chunks):
            if n + RET_SKEW < n_chunks:
                pending.append(state_free(hh, n + RET_SKEW))
            r_prev = finish(hh, n, pending.pop(0), r_prev)


def _retention(proj, cos_t, sin_t, decay, xi_b, zeta_b, B, S):
    H, dh, L, HP = RET_HEADS, RET_HEAD_DIM, RET_CHUNK, RET_HEADS_PER_STEP

    def pspec(base):
        return pl.BlockSpec((HP, S, LANES), lambda b, h, base=base: (base // HP + h, b, 0))

    def tspec(width):
        return pl.BlockSpec((HP, L, width), lambda b, h: (h, 0, 0))

    assert H % HP == 0 and all(blk % HP == 0 for blk in (_BLK_RQ, _BLK_RK, _BLK_RV, _BLK_RG))
    return pl.pallas_call(
        _retention_kernel,
        out_shape=jax.ShapeDtypeStruct((B * S, RET_W), BF16),
        grid=(B, H // HP),
        in_specs=[pspec(_BLK_RQ), pspec(_BLK_RK), pspec(_BLK_RV), pspec(_BLK_RG),
                  pl.BlockSpec((S, dh), lambda b, h: (0, 0)),
                  pl.BlockSpec((S, dh), lambda b, h: (0, 0)),
                  tspec(L), tspec(dh), tspec(dh)],
        out_specs=pl.BlockSpec((S, HP * dh), lambda b, h: (b, h)),
        compiler_params=_cparams(("parallel", "parallel")),
        name="retention",
    )(proj, proj, proj, proj, cos_t, sin_t, decay, xi_b, zeta_b)


def _mlstm_kernel(q_ref, k_ref, v_ref, og_ref, gp_ref, gb_ref, cwq_ref, cwk_ref, cbq_ref, cbk_ref,
                  nw_ref, o_ref, qpad_ref, kpad_ref, a_ref, bc_ref, at_ref, bt_ref):
    L = MLSTM_CHUNK
    S = q_ref.shape[0]
    NH = MLSTM_HEADS
    KC = MLSTM_CONV
    h = pl.program_id(1)
    n_chunks = S // L

    qpad_ref[0:SUBLANES, :] = jnp.zeros((SUBLANES, LANES), F32)
    kpad_ref[0:SUBLANES, :] = jnp.zeros((SUBLANES, LANES), F32)

    def conv_silu(src_ref, pad_ref, cw_ref, cb_ref, c, scale):
        pad_ref[SUBLANES + c * L:SUBLANES + (c + 1) * L, :] = src_ref[c * L:(c + 1) * L, :].astype(F32)
        cw = cw_ref[...]
        acc = cb_ref[...]
        for t in range(KC):
            r0 = SUBLANES + c * L - (KC - 1) + t
            acc = acc + pad_ref[r0:r0 + L, :] * cw[t:t + 1, :]
        return acc * jax.nn.sigmoid(acc) * scale

    lane = lax.broadcasted_iota(jnp.int32, (L, LANES), 1)
    sub = lax.broadcasted_iota(jnp.int32, (L, LANES), 0)
    sub_col = lax.broadcasted_iota(jnp.int32, (L, 1), 0)
    causal = sub >= lane
    nw = nw_ref[...]

    @pl.when(h == 0)
    def _():
        tri = (sub >= lane).astype(F32)
        is_f = (lane >= NH) & (lane < 2 * NH)
        for n in range(n_chunks):
            rows = slice(n * L, (n + 1) * L)
            gp = gp_ref[rows, :] + gb_ref[...]
            logsig = jnp.minimum(gp, 0.0) - jnp.log1p(jnp.exp(-jnp.abs(gp)))
            a = jnp.where(is_f, logsig, gp)
            bc = jnp.dot(tri, a, preferred_element_type=F32, precision=lax.Precision.HIGHEST)
            a_ref[rows, :] = a
            bc_ref[rows, :] = bc
            at_ref[n * SUBLANES:(n + 1) * SUBLANES, :] = a.T[0:SUBLANES, :]
            bt_ref[n * SUBLANES:(n + 1) * SUBLANES, :] = bc.T[0:SUBLANES, :]

    def state_free(n):
        rows = slice(n * L, (n + 1) * L)
        li_col = jnp.sum(jnp.where(lane == h, a_ref[rows, :], 0.0), axis=1, keepdims=True)
        b_col = jnp.sum(jnp.where(lane == h + NH, bc_ref[rows, :], 0.0), axis=1, keepdims=True)
        li_row = at_ref[pl.ds(n * SUBLANES + h, 1), :]
        b_row = bt_ref[pl.ds(n * SUBLANES + NH + h, 1), :]
        gtot = jnp.max(jnp.where(sub_col == L - 1, b_col, NEG_BIG), axis=0, keepdims=True)
        q = conv_silu(q_ref, qpad_ref, cwq_ref, cbq_ref, n, MLSTM_QK_DIM ** -0.5)
        k = conv_silu(k_ref, kpad_ref, cwk_ref, cbk_ref, n, 1.0)
        v = jnp.concatenate([v_ref[0, rows, :], v_ref[1, rows, :]], axis=1)
        qb = q.astype(BF16)
        dlog = jnp.where(causal, b_col - b_row + li_row, NEG_BIG)
        dmax = jnp.max(dlog, axis=1, keepdims=True)
        qk = lax.dot_general(qb, k.astype(BF16), (((1,), (1,)), ((), ())),
                             preferred_element_type=F32)
        return dict(li_col=li_col, b_col=b_col, gtot=gtot, q=q, k=k, qb=qb, v=v, dlog=dlog,
                    dmax=dmax, qk=qk)

    def finish(n, p, state):
        c_prev, n_prev, m_prev = state
        rows = slice(n * L, (n + 1) * L)
        inter_log = p["b_col"] + m_prev
        m_pos = jnp.maximum(inter_log, p["dmax"])
        s_intra = p["qk"] * jnp.exp(p["dlog"] - m_pos)
        inter_scale = jnp.exp(inter_log - m_pos)
        num = (jnp.dot(s_intra.astype(BF16), p["v"], preferred_element_type=F32)
               + inter_scale * jnp.dot(p["qb"], c_prev.astype(BF16), preferred_element_type=F32))
        den = (jnp.sum(s_intra, axis=1, keepdims=True)
               + inter_scale * jnp.sum(p["q"] * n_prev, axis=1, keepdims=True))
        inv = 1.0 / jnp.maximum(jnp.abs(den), jnp.exp(-m_pos))
        ms = jnp.mean(num * num, axis=-1, keepdims=True)
        row_scale = inv * lax.rsqrt(inv * inv * ms + EPS)
        og = jnp.concatenate([og_ref[0, rows, :], og_ref[1, rows, :]], axis=1).astype(F32)
        o_ref[rows, :] = (num * row_scale * nw * jax.nn.sigmoid(og)).astype(o_ref.dtype)
        w_loc = p["gtot"] - p["b_col"] + p["li_col"]
        m_loc = jnp.max(w_loc, axis=0, keepdims=True)
        ke = p["k"] * jnp.exp(w_loc - m_loc)
        c_loc = lax.dot_general(ke.astype(BF16), p["v"], (((0,), (0,)), ((), ())),
                                preferred_element_type=F32)
        n_loc = jnp.sum(ke, axis=0, keepdims=True)
        m_new = jnp.maximum(p["gtot"] + m_prev, m_loc)
        a_dec = jnp.exp(p["gtot"] + m_prev - m_new)
        b_inc = jnp.exp(m_loc - m_new)
        return (a_dec * c_prev + b_inc * c_loc, a_dec * n_prev + b_inc * n_loc, m_new)

    state = (jnp.zeros((MLSTM_QK_DIM, MLSTM_V_DIM), F32), jnp.zeros((1, MLSTM_QK_DIM), F32),
             jnp.zeros((1, 1), F32))
    for n in range(n_chunks):
        state = finish(n, state_free(n), state)


def _mlstm(proj, gpre, gate_b_row, conv_w, conv_b, norm_w, B, S):
    NH, dk, dv = MLSTM_HEADS, MLSTM_QK_DIM, MLSTM_V_DIM
    KC = MLSTM_CONV
    nvb = dv // LANES

    def pspec(base):
        return pl.BlockSpec((None, S, LANES), lambda b, h, base=base: (base + h, b, 0))

    def pspec2(base):
        return pl.BlockSpec((nvb, S, LANES), lambda b, h, base=base: (base // nvb + h, b, 0))

    return pl.pallas_call(
        _mlstm_kernel,
        out_shape=jax.ShapeDtypeStruct((B * S, MLSTM_V_W), BF16),
        grid=(B, NH),
        in_specs=[pspec(_BLK_MQ), pspec(_BLK_MK), pspec2(_BLK_MV), pspec2(_BLK_MO),
                  pl.BlockSpec((S, LANES), lambda b, h: (b, 0)),
                  pl.BlockSpec((1, LANES), lambda b, h: (0, 0)),
                  pl.BlockSpec((KC, dk), lambda b, h: (0, h)),
                  pl.BlockSpec((KC, dk), lambda b, h: (0, NH + h)),
                  pl.BlockSpec((1, dk), lambda b, h: (0, h)),
                  pl.BlockSpec((1, dk), lambda b, h: (0, NH + h)),
                  pl.BlockSpec((1, dv), lambda b, h: (0, h))],
        out_specs=pl.BlockSpec((S, dv), lambda b, h: (b, h)),
        scratch_shapes=[pltpu.VMEM((S + SUBLANES, dk), F32),
                        pltpu.VMEM((S + SUBLANES, dk), F32),
                        pltpu.VMEM((S, LANES), F32),
                        pltpu.VMEM((S, LANES), F32),
                        pltpu.VMEM((S // MLSTM_CHUNK * SUBLANES, MLSTM_CHUNK), F32),
                        pltpu.VMEM((S // MLSTM_CHUNK * SUBLANES, MLSTM_CHUNK), F32)],
        compiler_params=_cparams(("parallel", "arbitrary")),
        name="mlstm",
    )(proj, proj, proj, proj, gpre, gate_b_row, conv_w, conv_w, conv_b, conv_b, norm_w)


def _moba_kernel(q_ref, k_ref, v_ref, bvec_ref, far_ref, o_ref, kmean_ref, gate_ref, lg_ref,
                 bias_ref):
    BS = MOBA_BLOCK
    S = k_ref.shape[0]
    NB = S // BS
    nt = (((1,), (1,)), ((), ()))

    toep = pltpu.roll(jnp.broadcast_to(bvec_ref[...], (BS, 2 * BS)), 0, 1, stride=1, stride_axis=0)
    bias_ref[0] = toep[:, 0:BS]
    bias_ref[1] = toep[:, BS:2 * BS]

    kmean_ref[...] = jnp.zeros_like(kmean_ref)
    ones_rows = jnp.ones((2 * SUBLANES, BS), BF16)
    vts = []

    def prepare_key_block(j):
        kj = k_ref[j * BS:(j + 1) * BS, :].astype(F32)
        kmean_ref[j:j + 1, :] = jnp.mean(kj, axis=0, keepdims=True)
        vts.append(jnp.concatenate([v_ref[j * BS:(j + 1) * BS, :].astype(F32).T.astype(BF16),
                                    ones_rows], axis=0))

    dh = MOBA_HEAD_DIM
    far = far_ref[...]
    kl = lax.broadcasted_iota(jnp.int32, (BS, BS), 0)
    ql = lax.broadcasted_iota(jnp.int32, (BS, BS), 1)
    causal = kl <= ql

    def select_rows(i, q):
        if i <= MOBA_TOPK:
            return [None] * i
        gate_ref[i % 2] = lax.dot_general(kmean_ref[...], q.astype(F32), nt,
                                          preferred_element_type=F32,
                                          precision=lax.Precision.HIGHEST)
        rows = [gate_ref[i % 2, n:n + 1, :] for n in range(i)]
        add_rows = []
        for n in range(i):
            rank = jnp.zeros_like(rows[n])
            for mth in range(i):
                if mth == n:
                    continue
                ahead = rows[mth] > rows[n]
                if mth < n:
                    ahead = ahead | (rows[mth] == rows[n])
                rank = rank + jnp.where(ahead, 1.0, 0.0)
            add_rows.append(jnp.where(rank < MOBA_TOPK, 0.0, NEG_BIG))
        return add_rows

    def merge(a, b, op):
        return b if a is None else op(a, b)

    KT = BS // MOBA_KEY_SPLIT

    def logits_tile(i, j, q, add_rows):
        m = None
        for part in range(MOBA_KEY_SPLIT):
            ks = slice(part * KT, (part + 1) * KT)
            lg = lax.dot_general(k_ref[j * BS + part * KT:j * BS + (part + 1) * KT, :], q, nt,
                                 preferred_element_type=F32)
            if j == i:
                lg = jnp.where(causal[ks, :], lg + bias_ref[0, ks, :], NEG_BIG)
            elif j == i - 1:
                lg = lg + bias_ref[1, ks, :]
                if add_rows[j] is not None:
                    lg = lg + add_rows[j]
            else:
                lg = lg + (far if add_rows[j] is None else far + add_rows[j])
            lg_ref[i % 2, j, ks, :] = lg
            m = merge(m, jnp.max(lg, axis=0, keepdims=True), jnp.maximum)
        return m

    def value_tile(i, j, m):
        pv = None
        for part in range(MOBA_KEY_SPLIT):
            ks = slice(part * KT, (part + 1) * KT)
            p = jnp.exp2(lg_ref[i % 2, j, ks, :] - m)
            pv = merge(pv, jnp.dot(vts[j][:, ks], p.astype(BF16), preferred_element_type=F32),
                       jnp.add)
        return pv

    m_cur = logits_tile(0, 0, q_ref[0:BS, :], [])
    for i in range(NB):
        prepare_key_block(i)
        m_next = None
        if i + 1 < NB:
            q_next = q_ref[(i + 1) * BS:(i + 2) * BS, :]
            rows_next = select_rows(i + 1, q_next)
        acc = None
        lead = MOBA_LOGITS_LEAD
        if i + 1 < NB:
            for j in range(min(lead, i + 2)):
                m_next = merge(m_next, logits_tile(i + 1, j, q_next, rows_next), jnp.maximum)
        for j in range(i + 1):
            if i + 1 < NB and j + lead < i + 2:
                m_next = jnp.maximum(m_next, logits_tile(i + 1, j + lead, q_next, rows_next))
            acc = merge(acc, value_tile(i, j, m_cur), jnp.add)
        l = jnp.max(acc[dh:dh + SUBLANES, :], axis=0, keepdims=True)
        out_t = acc[0:dh, :] * (1.0 / l)
        o_ref[i * BS:(i + 1) * BS, :] = out_t.T.astype(o_ref.dtype)
        m_cur = m_next


def _ret_moba_kernel(rq, rk, rv, rg, cos, sin, dec, xi, zeta, bq, bk, bv, bvec, far,
                     o_ret, o_mb, *moba_scratch):
    _retention_kernel(rq, rk, rv, rg, cos, sin, dec, xi, zeta, o_ret)
    _moba_kernel(bq, bk, bv, bvec, far, o_mb, *moba_scratch)


def _ret_moba(proj, cos_t, sin_t, decay, xi_b, zeta_b, bias_by_dist, far_rows, B, S):
    H, dh, L, BS = RET_HEADS, RET_HEAD_DIM, RET_CHUNK, MOBA_BLOCK
    NB = S // BS
    assert MOBA_HEADS == H and MOBA_HEAD_DIM == dh and RET_HEADS_PER_STEP == 1

    def pspec(base, lead):
        return pl.BlockSpec((lead, S, LANES), lambda b, h, base=base: (base + h, b, 0))

    def tspec(width):
        return pl.BlockSpec((1, L, width), lambda b, h: (h, 0, 0))

    out = jax.ShapeDtypeStruct((B * S, H * dh), BF16)
    return pl.pallas_call(
        _ret_moba_kernel,
        out_shape=(out, out),
        grid=(B, H),
        in_specs=[pspec(_BLK_RQ, 1), pspec(_BLK_RK, 1), pspec(_BLK_RV, 1), pspec(_BLK_RG, 1),
                  pl.BlockSpec((S, dh), lambda b, h: (0, 0)),
                  pl.BlockSpec((S, dh), lambda b, h: (0, 0)),
                  tspec(L), tspec(dh), tspec(dh),
                  pspec(_BLK_BQ, None), pspec(_BLK_BK, None), pspec(_BLK_BV, None),
                  pl.BlockSpec((None, 1, 2 * BS), lambda b, h: (h, 0, 0)),
                  pl.BlockSpec((None, 1, BS), lambda b, h: (h, 0, 0))],
        out_specs=(pl.BlockSpec((S, dh), lambda b, h: (b, h)),
                   pl.BlockSpec((S, dh), lambda b, h: (b, h))),
        scratch_shapes=[pltpu.VMEM((NB, dh), F32),
                        pltpu.VMEM((2, NB, BS), F32),
                        pltpu.VMEM((2, NB, BS, BS), F32),
                        pltpu.VMEM((2, BS, BS), F32)],
        compiler_params=_cparams(("parallel", "parallel")),
        name="ret_moba",
    )(proj, proj, proj, proj, cos_t, sin_t, decay, xi_b, zeta_b, proj, proj, proj,
      bias_by_dist, far_rows)


def _moba(proj, bias_tiles, far_rows, B, S):
    H, dh, BS = MOBA_HEADS, MOBA_HEAD_DIM, MOBA_BLOCK
    NB = S // BS

    def pspec(base):
        return pl.BlockSpec((None, S, LANES), lambda b, h, base=base: (base + h, b, 0))

    return pl.pallas_call(
        _moba_kernel,
        out_shape=jax.ShapeDtypeStruct((B * S, MOBA_W), BF16),
        grid=(B, H),
        in_specs=[pspec(_BLK_BQ), pspec(_BLK_BK), pspec(_BLK_BV),
                  pl.BlockSpec((None, 1, 2 * BS), lambda b, h: (h, 0, 0)),
                  pl.BlockSpec((None, 1, BS), lambda b, h: (h, 0, 0))],
        out_specs=pl.BlockSpec((S, dh), lambda b, h: (b, h)),
        scratch_shapes=[pltpu.VMEM((NB, dh), F32),
                        pltpu.VMEM((2, NB, BS), F32),
                        pltpu.VMEM((2, NB, BS, BS), F32),
                        pltpu.VMEM((2, BS, BS), F32)],
        compiler_params=_cparams(("parallel", "parallel")),
        name="moba",
    )(proj, proj, proj, bias_tiles, far_rows)


def _merge_kernel(yr_ref, ym_ref, yb_ref, gr_ref, gm_ref, gb_ref, x_ref,
                  wr_ref, wm_ref, wb_ref, wo_ref, o_ref):
    def gate(g_ref):
        return jax.nn.sigmoid(g_ref[...].astype(F32))

    mixed = gate(gr_ref) * jnp.dot(yr_ref[...], wr_ref[...], preferred_element_type=F32)
    mixed = mixed + gate(gm_ref) * jnp.dot(ym_ref[...], wm_ref[...], preferred_element_type=F32)
    mixed = mixed + gate(gb_ref) * jnp.dot(yb_ref[...], wb_ref[...], preferred_element_type=F32)
    o_ref[...] = x_ref[...] + jnp.dot(mixed.astype(BF16), wo_ref[...], preferred_element_type=F32)


def _merge(y_ret, y_ml, y_mb, gate_pre, x2, w_r, w_m, w_b, w_o, layer, *, tm):
    T, D = x2.shape

    def resident(shape):
        return pl.BlockSpec((None,) + shape, lambda i: (layer, 0, 0), pipeline_mode=pl.Buffered(1))

    return pl.pallas_call(
        _merge_kernel,
        out_shape=jax.ShapeDtypeStruct((T, D), F32),
        grid=(T // tm,),
        in_specs=[pl.BlockSpec((tm, RET_W), lambda i: (i, 0)),
                  pl.BlockSpec((tm, MLSTM_V_W), lambda i: (i, 0)),
                  pl.BlockSpec((tm, MOBA_W), lambda i: (i, 0)),
                  pl.BlockSpec((tm, D), lambda i: (i, 0)),
                  pl.BlockSpec((tm, D), lambda i: (i, 1)),
                  pl.BlockSpec((tm, D), lambda i: (i, 2)),
                  pl.BlockSpec((tm, D), lambda i: (i, 0)),
                  resident((RET_W, D)), resident((MLSTM_V_W, D)), resident((MOBA_W, D)),
                  resident((D, D))],
        out_specs=pl.BlockSpec((tm, D), lambda i: (i, 0)),
        compiler_params=_cparams(("parallel",)),
        name="merge",
    )(y_ret, y_ml, y_mb, gate_pre, gate_pre, gate_pre, x2, w_r, w_m, w_b, w_o)


def _ffn_kernel(x_ref, xn_ref, nw_ref, fw_ref, w1_ref, w2_ref, o_ref, hn_ref, *, row_chunk,
                final_norm):
    i = pl.program_id(0)
    j = pl.program_id(1)
    rows_next = xn_ref.shape[0]
    slot = i % 2

    @pl.when((i == 0) & (j == 0))
    def _():
        def body(c, carry):
            r = pl.multiple_of(c * row_chunk, row_chunk)
            hn_ref[0, pl.ds(r, row_chunk), :] = _rms_rows(x_ref[pl.ds(r, row_chunk), :],
                                                          nw_ref[...]).astype(BF16)
            return carry
        lax.fori_loop(0, x_ref.shape[0] // row_chunk, body, 0)

    def step():
        u = jnp.dot(hn_ref[slot], w1_ref[...], preferred_element_type=F32)
        u = jnp.maximum(u, 0.0)
        y = jnp.dot((u * u).astype(BF16), w2_ref[...], preferred_element_type=F32)
        r = pl.multiple_of(j * rows_next, rows_next)
        hn_ref[1 - slot, pl.ds(r, rows_next), :] = _rms_rows(xn_ref[...], nw_ref[...]).astype(BF16)
        return y

    @pl.when(j == 0)
    def _():
        o_ref[...] = x_ref[...] + step()

    @pl.when(j > 0)
    def _():
        o_ref[...] += step()

    if final_norm:
        @pl.when(j == pl.num_programs(1) - 1)
        def _():
            def body(c, carry):
                r = pl.multiple_of(c * row_chunk, row_chunk)
                o_ref[pl.ds(r, row_chunk), :] = _rms_rows(o_ref[pl.ds(r, row_chunk), :], fw_ref[...])
                return carry
            lax.fori_loop(0, x_ref.shape[0] // row_chunk, body, 0)


def _ffn(x2, nw, fw, w1, w2, layer, *, tm, tf, final_norm):
    T, D = x2.shape
    F = w1.shape[2]
    n_i, n_j = T // tm, F // tf
    rows_next = tm // n_j
    assert rows_next * n_j == tm and rows_next % (2 * SUBLANES) == 0
    return pl.pallas_call(
        functools.partial(_ffn_kernel, row_chunk=128, final_norm=final_norm),
        out_shape=jax.ShapeDtypeStruct((T, D), F32),
        grid=(n_i, n_j),
        in_specs=[pl.BlockSpec((tm, D), lambda i, j: (i, 0)),
                  pl.BlockSpec((rows_next, D),
                               lambda i, j: (jnp.minimum(i + 1, n_i - 1) * n_j + j, 0)),
                  pl.BlockSpec((1, D), lambda i, j: (0, 0)),
                  pl.BlockSpec((1, D), lambda i, j: (0, 0)),
                  pl.BlockSpec((None, D, tf), lambda i, j: (layer, 0, j)),
                  pl.BlockSpec((None, tf, D), lambda i, j: (layer, j, 0))],
        out_specs=pl.BlockSpec((tm, D), lambda i, j: (i, 0)),
        scratch_shapes=[pltpu.VMEM((2, tm, D), BF16)],
        compiler_params=_cparams(("arbitrary", "arbitrary")),
        name="ffn",
    )(x2, x2, nw, fw, w1, w2)


def _rope_tables(S):
    half = RET_HEAD_DIM // 2
    inv = ROPE_BASE ** (-jnp.arange(half, dtype=F32) / half)
    ang = jnp.arange(S).astype(F32)[:, None] * inv[None, :]
    cos, sin = jnp.cos(ang), jnp.sin(ang)
    return jnp.concatenate([cos, cos], axis=-1), jnp.concatenate([-sin, sin], axis=-1)


def _retention_tables():
    H, L, dh = RET_HEADS, RET_CHUNK, RET_HEAD_DIM
    log_gamma = jnp.log1p(-jnp.exp2(-5.0 - jnp.arange(H, dtype=F32)))
    idx = jnp.arange(L, dtype=F32)
    diff = idx[:, None] - idx[None, :]
    decay = jnp.where(diff >= 0, jnp.exp(jnp.maximum(diff, 0.0) * log_gamma[:, None, None]), 0.0)
    zeta = jnp.exp((L - 1 - idx)[None, :] * log_gamma[:, None])
    xi = jnp.exp((idx + 1.0)[None, :] * log_gamma[:, None])
    return (decay, jnp.broadcast_to(xi[:, :, None], (H, L, dh)),
            jnp.broadcast_to(zeta[:, :, None], (H, L, dh)))


def _t5_bucket(dist):
    n = jnp.maximum(dist, 0)
    exact = REL_BUCKETS // 2
    nf = jnp.maximum(n, 1).astype(F32)
    large = exact + (jnp.log(nf / exact) / math.log(REL_MAX_DIST / exact)
                     * (REL_BUCKETS - exact)).astype(jnp.int32)
    large = jnp.minimum(large, REL_BUCKETS - 1)
    return jnp.where(n < exact, n, large)


def _moba_bias_tables(rel_bias, S):
    BS = MOBA_BLOCK
    assert BS + 1 >= REL_MAX_DIST and S % BS == 0
    table_t = rel_bias.T.astype(F32) * LOG2E
    onehot = (_t5_bucket(jnp.arange(2 * BS))[:, None] == jnp.arange(REL_BUCKETS)).astype(F32)
    by_dist = jnp.einsum("db,hb->hd", onehot, table_t, precision=lax.Precision.HIGHEST)
    far = table_t[:, REL_BUCKETS - 1]
    far_rows = jnp.broadcast_to(far[:, None, None], (far.shape[0], 1, BS))
    return by_dist[:, None, :], far_rows


def kernel(x, w_in, mlstm_gate_b, mlstm_conv_w, mlstm_conv_b, mlstm_norm_w, w_branch_ret,
           w_branch_mlstm, w_branch_moba, w_out, norm_mix_w, norm_mlp_w, w_ff1, w_ff2, rel_bias,
           final_norm_w):
    B, S, D = x.shape
    T = B * S
    depth = w_in.shape[0]
    x2 = x.reshape(T, D)

    cos_t, sin_t = _rope_tables(S)
    decay, xi_b, zeta_b = _retention_tables()
    bias_by_dist, far_rows = _moba_bias_tables(rel_bias, S)
    final_w = final_norm_w.reshape(1, D)

    scale_row = np.ones((1, w_in.shape[2] - _PACK_SHIFT), np.float32)
    scale_row[:, _OFF_RQ:_OFF_RQ + RET_W] = RET_HEAD_DIM ** -0.5
    scale_row[:, _OFF_MI:_OFF_MI + MOBA_W] = MOBA_HEAD_DIM ** -0.5 * LOG2E
    w_cat, w_gpre = _pack_w_in(jnp.swapaxes(w_in, 1, 2), jnp.asarray(scale_row), **_TILES["pack"])
    n_gate_cols = w_cat.shape[2] - _N_HEADMAJOR
    w_r, w_m, w_b, w_o, w_1, w_2 = [_cast_bf16(w, **_TILES["cast"]) for w in
                                    (w_branch_ret, w_branch_mlstm, w_branch_moba, w_out, w_ff1, w_ff2)]

    for l in range(depth):
        proj, hn, gpre = _in_proj(x2, norm_mix_w[l].reshape(1, D), w_cat, l, w_gpre,
                                  **_TILES["in_proj"])
        gate_pre = _matmul(hn, w_cat, l, _N_HEADMAJOR, n_gate_cols, out_dtype=BF16,
                           name="branch_gates", **_TILES["gates"])

        y_ret, y_mb = _ret_moba(proj, cos_t, sin_t, decay, xi_b, zeta_b, bias_by_dist, far_rows, B, S)
        gate_b_row = jnp.pad(mlstm_gate_b[l].reshape(1, 2 * MLSTM_HEADS),
                             ((0, 0), (0, LANES - 2 * MLSTM_HEADS)))
        y_ml = _mlstm(proj, gpre, gate_b_row, mlstm_conv_w[l], mlstm_conv_b[l].reshape(1, -1),
                      mlstm_norm_w[l].reshape(1, -1), B, S)

        x2 = _merge(y_ret, y_ml, y_mb, gate_pre, x2, w_r, w_m, w_b, w_o, l, **_TILES["merge"])
        x2 = _ffn(x2, norm_mlp_w[l].reshape(1, D), final_w, w_1, w_2, l,
                  final_norm=(l == depth - 1), **_TILES["ffn"])

    return x2.reshape(B, S, D)
```
